```python
import jax
import jax.numpy as jnp
from jax import lax
import numpy as np

D_MODEL = 1024
BATCH = 8
SEQ = 2048
DEPTH = 2

CTX_LEN = 256
GRID_W = 64
N_BRANCH = 4
BR_W = D_MODEL // 2
HEAD_DIM = 128
A_HEADS = BR_W // HEAD_DIM
A_KV_HEADS = A_HEADS // 2
A_GROUP = A_HEADS // A_KV_HEADS
Q_BLOCK = 128
ROPE_THETA = 10000.0
AXIS_DIM = HEAD_DIM // 2
B_CONV = 3
C_HEADS = 4
C_HEAD_K = BR_W // (2 * C_HEADS)
C_HEAD_V = BR_W // C_HEADS
C_KEY_W = C_HEADS * C_HEAD_K
C_GATE_RANK = 16
C_GATE_TAU = 16.0
C_CHUNK = 64
D_CONV = 31
ALPHA = (2 * DEPTH) ** 0.25
BETA = (8 * DEPTH) ** -0.25
EPS = 1e-6
IN_WIDTHS = (
    A_HEADS * HEAD_DIM, A_KV_HEADS * HEAD_DIM, A_KV_HEADS * HEAD_DIM, BR_W,
    BR_W, BR_W, BR_W, BR_W,
    C_KEY_W, C_KEY_W, C_HEADS * C_HEAD_V, BR_W, 2 * C_GATE_RANK,
    2 * BR_W, BR_W,
    N_BRANCH * D_MODEL,
)

kernel_name = 'hybrid_parallel_branch_diffusion_block'


def layer_norm(x):
    xf = x.astype(jnp.float32)
    mu = jnp.mean(xf, -1, keepdims=True)
    var = jnp.mean(jnp.square(xf - mu), -1, keepdims=True)
    return ((xf - mu) * lax.rsqrt(var + EPS)).astype(x.dtype)


def rms_norm(x, g):
    xf = x.astype(jnp.float32)
    y = xf * lax.rsqrt(jnp.mean(jnp.square(xf), -1, keepdims=True) + EPS)
    return y.astype(x.dtype) * g


def split_cols(w):
    cuts = [int(i) for i in np.cumsum(IN_WIDTHS)[:-1]]
    return jnp.split(w, cuts, axis=-1)


def dwconv(x, w):
    k, ch = w.shape
    return lax.conv_general_dilated(x, w[:, None, :].astype(x.dtype), (1,), [(k // 2, k // 2)],
                                    dimension_numbers=('NWC', 'WIO', 'NWC'), feature_group_count=ch)


def apply_rope(x, cos, sin):
    xp = x.reshape(x.shape[:-1] + (HEAD_DIM // 2, 2))
    x0, x1 = xp[..., 0], xp[..., 1]
    c = cos[:, None, :].astype(x.dtype)
    s = sin[:, None, :].astype(x.dtype)
    return jnp.stack([x0 * c - x1 * s, x0 * s + x1 * c], -1).reshape(x.shape)


def attn_heads(q, k, v, q_g, k_g, cos, sin):
    b, t, _ = q.shape
    qh = rms_norm(q.reshape(b, t, A_HEADS, HEAD_DIM), q_g)
    kh = rms_norm(k.reshape(b, t, A_KV_HEADS, HEAD_DIM), k_g)
    if cos is not None:
        qh = apply_rope(qh, cos, sin)
        kh = apply_rope(kh, cos, sin)
    qh = qh.reshape(b, t, A_KV_HEADS, A_GROUP, HEAD_DIM).transpose(0, 2, 3, 1, 4)
    kh = kh.transpose(0, 2, 1, 3)
    vh = v.reshape(b, t, A_KV_HEADS, HEAD_DIM).transpose(0, 2, 1, 3)
    return qh, kh, vh


def softmax_attention(q, k, v):
    s = jnp.einsum('bkgqd,bksd->bkgqs', q, k).astype(jnp.float32) * HEAD_DIM ** -0.5
    p = jax.nn.softmax(s, axis=-1).astype(v.dtype)
    return jnp.einsum('bkgqs,bksd->bkgqd', p, v)


def block_attention(q, k, v):
    b, kv, g, t, hd = q.shape
    nb = t // Q_BLOCK
    qb = q.reshape(b, kv, g, nb, Q_BLOCK, hd).transpose(3, 0, 1, 2, 4, 5)
    o = lax.map(lambda qi: softmax_attention(qi, k, v), qb)
    return o.transpose(1, 0, 4, 2, 3, 5).reshape(b, t, kv * g * hd)


def merge_heads(o):
    b, kv, g, t, hd = o.shape
    return o.transpose(0, 3, 1, 2, 4).reshape(b, t, kv * g * hd)


def gla_chunked(q, k, v, g, s0, want_out):
    b, h, l, _ = q.shape
    dv = v.shape[-1]
    n = l // C_CHUNK

    def chunks(a):
        return a.astype(jnp.float32).reshape(b, h, n, C_CHUNK, a.shape[-1]).transpose(2, 0, 1, 3, 4)

    mask = jnp.tril(jnp.ones((C_CHUNK, C_CHUNK), dtype=bool))[:, :, None]

    def step(s, inp):
        qc, kc, vc, gc = inp
        cum = jnp.cumsum(gc, axis=2)
        last = cum[:, :, -1:, :]
        s_new = jnp.exp(last[:, :, 0, :])[..., None] * s + jnp.einsum('bhjd,bhjv->bhdv', kc * jnp.exp(last - cum), vc)
        if not want_out:
            return s_new, None
        diff = cum[:, :, :, None, :] - cum[:, :, None, :, :]
        decay = jnp.exp(jnp.where(mask, diff, -jnp.inf))
        scores = jnp.einsum('bhid,bhjd,bhijd->bhij', qc, kc, decay)
        o = jnp.einsum('bhij,bhjv->bhiv', scores, vc) + jnp.einsum('bhid,bhdv->bhiv', qc * jnp.exp(cum), s)
        return s_new, o

    s_fin, o = lax.scan(step, s0, (chunks(q), chunks(k), chunks(v), chunks(g)))
    if not want_out:
        return None, s_fin
    return o.transpose(1, 2, 0, 3, 4).reshape(b, h, l, dv).astype(v.dtype), s_fin


def gla_branch(side_c, side_l, w2, b2, norm_g, want_ctx):
    def heads(a, d):
        return a.reshape(a.shape[0], a.shape[1], C_HEADS, d).transpose(0, 2, 1, 3)

    def prep(q, k, v, r):
        gates = [heads(jax.nn.log_sigmoid((r[..., i * C_GATE_RANK:(i + 1) * C_GATE_RANK] @ w2[i] + b2[i])
                                          .astype(jnp.float32)) / C_GATE_TAU, C_HEAD_K) for i in range(2)]
        return heads(q, C_HEAD_K) * C_HEAD_K ** -0.5, heads(k, C_HEAD_K), heads(v, C_HEAD_V), gates[0], gates[1]

    def flip(a):
        return jnp.flip(a, axis=2)

    qc, kc, vc, gcf, gcb = prep(*side_c)
    ql, kl, vl, glf, glb = prep(*side_l)
    s0 = jnp.zeros((qc.shape[0], C_HEADS, C_HEAD_K, C_HEAD_V), jnp.float32)
    oc_f, sc_f = gla_chunked(qc, kc, vc, gcf, s0, want_ctx)
    oc_b, sc_b = gla_chunked(flip(qc), flip(kc), flip(vc), flip(gcb), s0, want_ctx)
    ol_f, _ = gla_chunked(ql, kl, vl, glf, sc_f, True)
    ol_b, _ = gla_chunked(flip(ql), flip(kl), flip(vl), flip(glb), sc_b, True)

    def finish(of, ob):
        o = rms_norm(of + flip(ob), norm_g.reshape(C_HEADS, 1, C_HEAD_V))
        return o.transpose(0, 2, 1, 3).reshape(o.shape[0], o.shape[2], C_HEADS * C_HEAD_V)

    y_ctx = finish(oc_f, oc_b) if want_ctx else None
    return y_ctx, finish(ol_f, ol_b)


def short_conv(bg, cg, xin, z, w):
    return bg * dwconv(cg * xin, w) * jax.nn.silu(z)


def conformer_conv(glu, z, w, bias, g, beta):
    a, gt = jnp.split(glu, 2, axis=-1)
    hh = dwconv(a * jax.nn.sigmoid(gt), w) + bias
    hh = layer_norm(hh) * g + beta
    return jax.nn.silu(hh) * jax.nn.silu(z)


def merge(branches, mg, w_br, w_out):
    b, t, _ = mg.shape
    gates = jax.nn.sigmoid(mg.reshape(b, t, N_BRANCH, D_MODEL))
    acc = gates[:, :, 0] * (branches[0] @ w_br[0])
    for i in range(1, N_BRANCH):
        acc = acc + gates[:, :, i] * (branches[i] @ w_br[i])
    return acc @ w_out


def mixer(u_ctx, u_lat, cos, sin, w_in, q_g, k_g, b_w, c_w2, c_b2, c_g, d_w, d_b, d_g, d_beta, w_br, w_out, want_ctx):
    w_parts = split_cols(w_in)
    pc = [u_ctx @ w for w in w_parts]
    pl = [u_lat @ w for w in w_parts]
    qa_l, ka_l, va_l = attn_heads(pl[0], pl[1], pl[2], q_g, k_g, cos, sin)
    qa_c, ka_c, va_c = attn_heads(pc[0], pc[1], pc[2], q_g, k_g, None, None)
    ya_l = block_attention(qa_l, jnp.concatenate([ka_l, ka_c], 2), jnp.concatenate([va_l, va_c], 2)) * jax.nn.silu(pl[3])
    yc_c, yc_l = gla_branch((pc[8], pc[9], pc[10], pc[12]), (pl[8], pl[9], pl[10], pl[12]), c_w2, c_b2, c_g, want_ctx)
    yc_l = yc_l * jax.nn.silu(pl[11])

    def local_branches(p):
        yb = short_conv(p[4], p[5], p[6], p[7], b_w)
        yd = conformer_conv(p[13], p[14], d_w, d_b, d_g, d_beta)
        return yb, yd

    yb_l, yd_l = local_branches(pl)
    y_lat = merge([ya_l, yb_l, yc_l, yd_l], pl[15], w_br, w_out)
    if not want_ctx:
        return None, y_lat
    ya_c = merge_heads(softmax_attention(qa_c, ka_c, va_c)) * jax.nn.silu(pc[3])
    yb_c, yd_c = local_branches(pc)
    y_ctx = merge([ya_c, yb_c, yc_c * jax.nn.silu(pc[11]), yd_c], pc[15], w_br, w_out)
    return y_ctx, y_lat


def setup_inputs(seed: int = 0) -> dict:
    key = jax.random.key(seed)
    ks = jax.random.split(key, 24)

    def nrm(k, shape, s):
        return jax.random.normal(k, shape, jnp.float32) * s

    total_in = sum(IN_WIDTHS)
    return {
        'x': nrm(ks[0], (BATCH, SEQ, D_MODEL), 1.0),
        'c': nrm(ks[1], (BATCH, D_MODEL), 1.0),
        'ctx': nrm(ks[2], (BATCH, CTX_LEN, D_MODEL), 1.0),
        'c_ctx': nrm(ks[3], (D_MODEL,), 1.0),
        'w_mod': nrm(ks[4], (DEPTH, D_MODEL, 3 * D_MODEL), D_MODEL ** -0.5),
        'b_mod': nrm(ks[5], (DEPTH, 3 * D_MODEL), 0.02),
        'w_in': nrm(ks[6], (DEPTH, D_MODEL, total_in), D_MODEL ** -0.5),
        'q_norm': 1.0 + nrm(ks[7], (DEPTH, HEAD_DIM), 0.02),
        'k_norm': 1.0 + nrm(ks[8], (DEPTH, HEAD_DIM), 0.02),
        'b_conv': nrm(ks[9], (DEPTH, B_CONV, BR_W), B_CONV ** -0.5),
        'c_gate_w2': nrm(ks[10], (DEPTH, 2, C_GATE_RANK, C_KEY_W), C_GATE_RANK ** -0.5),
        'c_gate_b': nrm(ks[11], (DEPTH, 2, C_KEY_W), 0.1),
        'c_norm': 1.0 + nrm(ks[12], (DEPTH, BR_W), 0.02),
        'd_conv_w': nrm(ks[13], (DEPTH, D_CONV, BR_W), D_CONV ** -0.5),
        'd_conv_b': nrm(ks[14], (DEPTH, BR_W), 0.02),
        'd_norm_g': 1.0 + nrm(ks[15], (DEPTH, BR_W), 0.02),
        'd_norm_b': nrm(ks[16], (DEPTH, BR_W), 0.02),
        'w_br': nrm(ks[17], (DEPTH, N_BRANCH, BR_W, D_MODEL), BETA * BR_W ** -0.5),
        'w_out': nrm(ks[18], (DEPTH, D_MODEL, D_MODEL), BETA * D_MODEL ** -0.5),
        'ln_g': 1.0 + nrm(ks[19], (DEPTH, D_MODEL), 0.02),
        'ln_b': nrm(ks[20], (DEPTH, D_MODEL), 0.02),
    }


def reference(x, c, ctx, c_ctx, w_mod, b_mod, w_in, q_norm, k_norm, b_conv, c_gate_w2, c_gate_b, c_norm,
              d_conv_w, d_conv_b, d_norm_g, d_norm_b, w_br, w_out, ln_g, ln_b):
    rows = x.shape[1] // GRID_W
    row = jnp.repeat(jnp.arange(rows), GRID_W).astype(jnp.float32)
    col = jnp.tile(jnp.arange(GRID_W), rows).astype(jnp.float32)
    inv = ROPE_THETA ** (-jnp.arange(0, AXIS_DIM, 2, dtype=jnp.float32) / AXIS_DIM)
    ang = jnp.concatenate([row[:, None] * inv, col[:, None] * inv], -1)
    cos, sin = jnp.cos(ang), jnp.sin(ang)
    sc = jax.nn.silu(c)
    scc = jax.nn.silu(c_ctx)
    h_lat, h_ctx = x, ctx
    for l in range(DEPTH):
        want_ctx = l < DEPTH - 1
        shift, scale, gate = jnp.split(sc @ w_mod[l] + b_mod[l], 3, axis=-1)
        shift_c, scale_c, gate_c = jnp.split(scc @ w_mod[l] + b_mod[l], 3, axis=-1)
        u_lat = layer_norm(h_lat) * (1.0 + scale[:, None]) + shift[:, None]
        u_ctx = layer_norm(h_ctx) * (1.0 + scale_c) + shift_c
        y_ctx, y_lat = mixer(u_ctx, u_lat, cos, sin, w_in[l], q_norm[l], k_norm[l], b_conv[l], c_gate_w2[l],
                             c_gate_b[l], c_norm[l], d_conv_w[l], d_conv_b[l], d_norm_g[l], d_norm_b[l],
                             w_br[l], w_out[l], want_ctx)
        h_lat = layer_norm(ALPHA * h_lat + gate[:, None] * y_lat) * ln_g[l] + ln_b[l]
        if want_ctx:
            h_ctx = layer_norm(ALPHA * h_ctx + gate_c * y_ctx) * ln_g[l] + ln_b[l]
    return h_lat
```

```python
import functools

import numpy as np
import jax
import jax.numpy as jnp
from jax import lax
from jax.experimental import pallas as pl
from jax.experimental.pallas import tpu as pltpu

F32 = jnp.float32
BF16 = jnp.bfloat16

D_MODEL = 1024
GRID_W = 64
N_BRANCH = 4
BR_W = D_MODEL // 2
HEAD_DIM = 128
A_HEADS = BR_W // HEAD_DIM
A_KV_HEADS = A_HEADS // 2
ROPE_THETA = 10000.0
AXIS_DIM = HEAD_DIM // 2
B_CONV = 3
C_HEADS = 4
C_HEAD_K = BR_W // (2 * C_HEADS)
C_HEAD_V = BR_W // C_HEADS
C_KEY_W = C_HEADS * C_HEAD_K
C_GATE_RANK = 16
C_GATE_TAU = 16.0
C_CHUNK = 64
D_CONV = 31
EPS = 1e-6
IN_WIDTHS = (
    A_HEADS * HEAD_DIM, A_KV_HEADS * HEAD_DIM, A_KV_HEADS * HEAD_DIM, BR_W,
    BR_W, BR_W, BR_W, BR_W,
    C_KEY_W, C_KEY_W, C_HEADS * C_HEAD_V, BR_W, 2 * C_GATE_RANK,
    2 * BR_W, BR_W,
    N_BRANCH * D_MODEL,
)

COL_MG = 0
COL_AQ = 4096
COL_AK = COL_AQ + 512
COL_AV = COL_AK + 256
COL_AZ = COL_AV + 256
COL_BG = COL_AZ + 512
COL_BC = COL_BG + 512
COL_BX = COL_BC + 512
COL_BZ = COL_BX + 512
COL_CQ = COL_BZ + 512
COL_CK = COL_CQ + 256
COL_CV = COL_CK + 256
COL_CZ = COL_CV + 512
COL_DA = COL_CZ + 512
COL_DG = COL_DA + 512
COL_DZ = COL_DG + 512
N_MAIN = COL_DZ + 512
LANES = 128
PROJ_TN = 512
NEG_BIG = -1e30
VMEM_LIMIT = 56 * 1024 * 1024


def _cparams(n_axes):
    return pltpu.CompilerParams(dimension_semantics=("arbitrary",) * n_axes,
                                vmem_limit_bytes=VMEM_LIMIT)


def _silu(x):
    return x * jax.nn.sigmoid(x)


def _dot(a, b):
    return jnp.dot(a, b, preferred_element_type=F32)


def _dot_nt(a, b):
    return lax.dot_general(a, b, (((1,), (1,)), ((), ())), preferred_element_type=F32)


def _dot_tn(a, b):
    return lax.dot_general(a, b, (((0,), (0,)), ((), ())), preferred_element_type=F32)


def _mod_kernel(c_ref, w_ref, b_ref, o_ref):
    s = _silu(c_ref[...])
    o_ref[0] = _dot(s.astype(BF16), w_ref[0].astype(BF16)) + b_ref[0]


def _modulation(cc, w_mod, b_mod):
    depth = w_mod.shape[0]
    n_rows = cc.shape[0]
    return pl.pallas_call(
        _mod_kernel,
        grid=(depth, 3),
        in_specs=[
            pl.BlockSpec((n_rows, D_MODEL), lambda l, j: (0, 0)),
            pl.BlockSpec((1, D_MODEL, D_MODEL), lambda l, j: (l, 0, j)),
            pl.BlockSpec((1, 1, D_MODEL), lambda l, j: (l, 0, j)),
        ],
        out_specs=pl.BlockSpec((1, n_rows, D_MODEL), lambda l, j: (l, 0, j)),
        out_shape=jax.ShapeDtypeStruct((depth, n_rows, 3 * D_MODEL), F32),
        compiler_params=_cparams(2),
        name="modulation",
    )(cc, w_mod, b_mod.reshape(depth, 1, 3 * D_MODEL))


def _inproj_kernel(x_ref, shift_ref, scale_ref, w_ref, wr_ref, o_ref, or_ref, u_ref, *, tm, sub):
    j = pl.program_id(1)

    @pl.when(j == 0)
    def _():
        one_plus = 1.0 + scale_ref[0]
        shift = shift_ref[0]

        def body(t, carry):
            r0 = pl.multiple_of(t * sub, sub)
            x = x_ref[pl.ds(r0, sub), :]
            mu = jnp.mean(x, axis=-1, keepdims=True)
            xc = x - mu
            var = jnp.mean(xc * xc, axis=-1, keepdims=True)
            u = xc * lax.rsqrt(var + EPS) * one_plus + shift
            u_ref[pl.ds(r0, sub), :] = u.astype(BF16)
            return carry

        lax.fori_loop(0, tm // sub, body, 0)
        or_ref[...] = _dot(u_ref[...], wr_ref[...])

    o_ref[...] = _dot(u_ref[...], w_ref[...])


def _in_projection(h, shift, scale, rows_per_mod, w_main, w_r):
    m = h.shape[0]
    tm = min(1024, rows_per_mod)
    assert m % tm == 0 and rows_per_mod % tm == 0
    tiles_per_mod = rows_per_mod // tm
    n_tiles = N_MAIN // PROJ_TN
    kern = functools.partial(_inproj_kernel, tm=tm, sub=min(128, tm))
    return pl.pallas_call(
        kern,
        grid=(m // tm, n_tiles),
        in_specs=[
            pl.BlockSpec((tm, D_MODEL), lambda i, j: (i, 0)),
            pl.BlockSpec((1, 1, D_MODEL), lambda i, j: (i // tiles_per_mod, 0, 0)),
            pl.BlockSpec((1, 1, D_MODEL), lambda i, j: (i // tiles_per_mod, 0, 0)),
            pl.BlockSpec((D_MODEL, PROJ_TN), lambda i, j: (0, j)),
            pl.BlockSpec((D_MODEL, LANES), lambda i, j: (0, 0)),
        ],
        out_specs=[
            pl.BlockSpec((tm, PROJ_TN), lambda i, j: (i, j)),
            pl.BlockSpec((tm, LANES), lambda i, j: (i, 0)),
        ],
        out_shape=[
            jax.ShapeDtypeStruct((m, N_MAIN), F32),
            jax.ShapeDtypeStruct((m, LANES), F32),
        ],
        scratch_shapes=[pltpu.VMEM((tm, D_MODEL), BF16)],
        compiler_params=_cparams(2),
        name="in_projection",
    )(h, shift, scale, w_main, w_r)


def _rms(x):
    return x * lax.rsqrt(jnp.mean(x * x, axis=-1, keepdims=True) + EPS)


def _rope(x, cos, sin):
    return x * cos + pltpu.roll(x, HEAD_DIM // 2, 1) * sin


def _attn_kernel(*refs, n_lat, n_ctx, tq, kblk):
    if n_lat:
        (q_ref, z_ref, kl_ref, vl_ref, kc_ref, vc_ref, qg_ref, kg_ref, cos_ref, sin_ref,
         o_ref, kn_ref, vn_ref) = refs
    else:
        q_ref, z_ref, kc_ref, vc_ref, qg_ref, kg_ref, o_ref, kn_ref, vn_ref = refs
    qi = pl.program_id(2)
    kg = kg_ref[...]

    @pl.when(qi == 0)
    def _prep():
        if n_lat:
            def body(t, carry):
                r0 = pl.multiple_of(t * kblk, kblk)
                kn = _rms(kl_ref[0, pl.ds(r0, kblk), :]) * kg
                kn = _rope(kn, cos_ref[pl.ds(r0, kblk), :], sin_ref[pl.ds(r0, kblk), :])
                kn_ref[pl.ds(r0, kblk), :] = kn.astype(BF16)
                vn_ref[pl.ds(r0, kblk), :] = vl_ref[0, pl.ds(r0, kblk), :].astype(BF16)
                return carry

            lax.fori_loop(0, n_lat // kblk, body, 0)
        kn_ref[n_lat:n_lat + n_ctx, :] = (_rms(kc_ref[0]) * kg).astype(BF16)
        vn_ref[n_lat:n_lat + n_ctx, :] = vc_ref[0].astype(BF16)

    qg = qg_ref[...]
    if n_lat:
        q0 = pl.multiple_of(qi * tq, tq)
        cos = cos_ref[pl.ds(q0, tq), :]
        sin = sin_ref[pl.ds(q0, tq), :]
    for g in range(2):
        lanes = slice(g * HEAD_DIM, (g + 1) * HEAD_DIM)
        qn = _rms(q_ref[0, :, lanes]) * qg
        if n_lat:
            qn = _rope(qn, cos, sin)
        qn = (qn * HEAD_DIM ** -0.5).astype(BF16)
        s = _dot_nt(qn, kn_ref[...])
        p = jnp.exp(s - jnp.max(s, axis=-1, keepdims=True))
        l = jnp.sum(p, axis=-1, keepdims=True)
        o = _dot(p.astype(BF16), vn_ref[...]) / l
        o_ref[0, :, lanes] = (o * _silu(z_ref[0, :, lanes])).astype(BF16)


def _attention(p_q, p_lat, p_ctx, q_gain, k_gain, cos, sin):
    b, t_q, _ = p_q.shape
    n_ctx = p_ctx.shape[1]
    n_lat = 0 if p_lat is None else p_lat.shape[1]
    tq = min(256, t_q)
    assert t_q % tq == 0
    kblk = min(256, n_lat) if n_lat else 0
    hw = 2 * HEAD_DIM
    q_spec = pl.BlockSpec((1, tq, hw), lambda i, kv, qi: (i, qi, COL_AQ // hw + kv))
    z_spec = pl.BlockSpec((1, tq, hw), lambda i, kv, qi: (i, qi, COL_AZ // hw + kv))

    def kv_spec(n, col):
        return pl.BlockSpec((1, n, HEAD_DIM), lambda i, kv, qi: (i, 0, col // HEAD_DIM + kv))

    gain_spec = pl.BlockSpec((1, HEAD_DIM), lambda i, kv, qi: (0, 0))
    args = [p_q, p_q]
    specs = [q_spec, z_spec]
    if n_lat:
        args += [p_lat, p_lat]
        specs += [kv_spec(n_lat, COL_AK), kv_spec(n_lat, COL_AV)]
    args += [p_ctx, p_ctx, q_gain, k_gain]
    specs += [kv_spec(n_ctx, COL_AK), kv_spec(n_ctx, COL_AV), gain_spec, gain_spec]
    if n_lat:
        tab_spec = pl.BlockSpec((n_lat, HEAD_DIM), lambda i, kv, qi: (0, 0))
        args += [cos, sin]
        specs += [tab_spec, tab_spec]
    kern = functools.partial(_attn_kernel, n_lat=n_lat, n_ctx=n_ctx, tq=tq, kblk=kblk)
    return pl.pallas_call(
        kern,
        grid=(b, A_KV_HEADS, t_q // tq),
        in_specs=specs,
        out_specs=pl.BlockSpec((1, tq, hw), lambda i, kv, qi: (i, qi, kv)),
        out_shape=jax.ShapeDtypeStruct((b, t_q, BR_W), BF16),
        scratch_shapes=[pltpu.VMEM((n_lat + n_ctx, HEAD_DIM), BF16),
                        pltpu.VMEM((n_lat + n_ctx, HEAD_DIM), BF16)],
        compiler_params=_cparams(3),
        name="attention_lat" if n_lat else "attention_ctx",
    )(*args)


D_HALO = 16
B_HALO = 8
CONV_RB = 64


def _local_kernel(bg_ref, bc_ref, bcp_ref, bcn_ref, bx_ref, bxp_ref, bxn_ref, bz_ref,
                  da_ref, dap_ref, dan_ref, dg_ref, dgp_ref, dgn_ref, dz_ref,
                  bw_ref, dw_ref, db_ref, dgain_ref, dbeta_ref,
                  yb_ref, yd_ref, tbuf, gbuf, hbuf, *, tt, tiles_per_seq):
    i = pl.program_id(0)
    pos = i % tiles_per_seq
    keep_prev = (pos != 0).astype(F32)
    keep_next = (pos != tiles_per_seq - 1).astype(F32)

    t_cur = bc_ref[...] * bx_ref[...]
    tbuf[0:B_HALO, :] = bcp_ref[...] * bxp_ref[...] * keep_prev
    tbuf[B_HALO:B_HALO + tt, :] = t_cur
    tbuf[B_HALO + tt:2 * B_HALO + tt, :] = bcn_ref[...] * bxn_ref[...] * keep_next
    conv = (bw_ref[0:1, :] * tbuf[B_HALO - 1:B_HALO - 1 + tt, :] + bw_ref[1:2, :] * t_cur
            + bw_ref[2:3, :] * tbuf[B_HALO + 1:B_HALO + 1 + tt, :])
    yb_ref[...] = (bg_ref[...] * conv * _silu(bz_ref[...])).astype(BF16)

    gbuf[0:D_HALO, :] = dap_ref[...] * jax.nn.sigmoid(dgp_ref[...]) * keep_prev
    gbuf[D_HALO:D_HALO + tt, :] = da_ref[...] * jax.nn.sigmoid(dg_ref[...])
    gbuf[D_HALO + tt:2 * D_HALO + tt, :] = dan_ref[...] * jax.nn.sigmoid(dgn_ref[...]) * keep_next
    base = D_HALO - D_CONV // 2
    for cb in range(BR_W // LANES):
        lanes = slice(cb * LANES, (cb + 1) * LANES)
        for rb in range(tt // CONV_RB):
            r0 = rb * CONV_RB
            acc = dw_ref[0:1, lanes] * gbuf[base + r0:base + r0 + CONV_RB, lanes]
            for k in range(1, D_CONV):
                acc = acc + dw_ref[k:k + 1, lanes] * gbuf[base + r0 + k:base + r0 + k + CONV_RB, lanes]
            hbuf[r0:r0 + CONV_RB, lanes] = acc
    hh = hbuf[...] + db_ref[...]
    mu = jnp.mean(hh, axis=-1, keepdims=True)
    hc = hh - mu
    var = jnp.mean(hc * hc, axis=-1, keepdims=True)
    hn = hc * lax.rsqrt(var + EPS) * dgain_ref[...] + dbeta_ref[...]
    yd_ref[...] = (_silu(hn) * _silu(dz_ref[...])).astype(BF16)


def _local_branches(p2, seq_len, b_w, d_w, d_b, d_g, d_beta):
    m = p2.shape[0]
    tt = min(256, seq_len)
    assert seq_len % tt == 0 and m % seq_len == 0
    tiles_per_seq = seq_len // tt
    n_tiles = m // tt

    def cur(col):
        return pl.BlockSpec((tt, BR_W), lambda i: (i, col // BR_W))

    def prev(col, halo):
        per = tt // halo
        return pl.BlockSpec((halo, BR_W), lambda i: (jnp.maximum(i * per - 1, 0), col // BR_W))

    def nxt(col, halo):
        per = tt // halo
        last = m // halo - 1
        return pl.BlockSpec((halo, BR_W), lambda i: (jnp.minimum((i + 1) * per, last), col // BR_W))

    def small(rows):
        return pl.BlockSpec((rows, BR_W), lambda i: (0, 0))

    specs = [cur(COL_BG),
             cur(COL_BC), prev(COL_BC, B_HALO), nxt(COL_BC, B_HALO),
             cur(COL_BX), prev(COL_BX, B_HALO), nxt(COL_BX, B_HALO),
             cur(COL_BZ),
             cur(COL_DA), prev(COL_DA, D_HALO), nxt(COL_DA, D_HALO),
             cur(COL_DG), prev(COL_DG, D_HALO), nxt(COL_DG, D_HALO),
             cur(COL_DZ),
             small(B_CONV), small(D_CONV), small(1), small(1), small(1)]
    kern = functools.partial(_local_kernel, tt=tt, tiles_per_seq=tiles_per_seq)
    out_spec = pl.BlockSpec((tt, BR_W), lambda i: (i, 0))
    return pl.pallas_call(
        kern,
        grid=(n_tiles,),
        in_specs=specs,
        out_specs=[out_spec, out_spec],
        out_shape=[jax.ShapeDtypeStruct((m, BR_W), BF16), jax.ShapeDtypeStruct((m, BR_W), BF16)],
        scratch_shapes=[pltpu.VMEM((tt + 2 * B_HALO, BR_W), F32),
                        pltpu.VMEM((tt + 2 * D_HALO, BR_W), F32),
                        pltpu.VMEM((tt, BR_W), F32)],
        compiler_params=_cparams(1),
        name="local_branches",
    )(*([p2] * 15), b_w, d_w, d_b.reshape(1, BR_W), d_g.reshape(1, BR_W), d_beta.reshape(1, BR_W))


CH = C_CHUNK
PAIR_K = 2 * C_HEAD_K
PAIR_V = 2 * C_HEAD_V
DIAG = 8
LEVELS = (64, 32, 16)


def _gla_constants():
    idx = np.arange(CH)
    tri_f = (idx[None, :] <= idx[:, None]).astype(np.float32)
    tri_b = (idx[None, :] >= idx[:, None]).astype(np.float32)
    cum_f = np.concatenate([tri_f] * 3, axis=1)
    cum_b = np.concatenate([tri_b] * 3, axis=1)
    rows = np.arange(DIAG * PAIR_K)
    cols = np.arange(2 * CH)
    ones = ((rows[:, None] // PAIR_K == (cols[None, :] % CH) % DIAG)
            & ((rows[:, None] % PAIR_K) // C_HEAD_K == cols[None, :] // CH)).astype(np.float32)
    masks = [(idx[:, None] // g == (cols[None, :] % CH) // g).astype(np.float32) for g in (DIAG, 16, 32)]
    return (jnp.asarray(cum_f, BF16), jnp.asarray(cum_b, BF16), jnp.asarray(ones, BF16),
            jnp.asarray(np.stack(masks), F32))


def _split3(g):
    g1 = g.astype(BF16)
    r1 = g - g1.astype(F32)
    g2 = r1.astype(BF16)
    g3 = (r1 - g2.astype(F32)).astype(BF16)
    return jnp.concatenate([g1, g2, g3], axis=0)


def _log_sigmoid(x):
    return jnp.minimum(x, 0.0) - jnp.log1p(jnp.exp(-jnp.abs(x)))


def _gla_chunk(q, k, v, g, st_ref, cum_mat, ones_mat, masks_ref, reverse):
    lane_k = lax.broadcasted_iota(jnp.int32, (CH, PAIR_K), 1)
    row = lax.broadcasted_iota(jnp.int32, (CH, PAIR_K), 0)
    head0_k = lane_k < C_HEAD_K
    cum = _dot(cum_mat, _split3(g))

    st = st_ref[...]
    o = _dot_nt((q * jnp.exp(cum)).astype(BF16), st.astype(BF16))
    last = cum[0:1, :] if reverse else cum[CH - 1:CH, :]
    ke = k * jnp.exp(last - cum)
    upd = _dot_tn(v.astype(BF16), ke.astype(BF16))
    st_row = lax.broadcasted_iota(jnp.int32, (PAIR_V, PAIR_K), 0)
    st_col = lax.broadcasted_iota(jnp.int32, (PAIR_V, PAIR_K), 1)
    same_head = (st_row // C_HEAD_V) == (st_col // C_HEAD_K)
    st_ref[...] = jnp.exp(last) * st + jnp.where(same_head, upd, 0.0)

    scores = None
    for lvl, grp in enumerate(LEVELS):
        half = grp // 2
        in_q_half = (row % grp < half) if reverse else (row % grp >= half)
        refs = []
        for gi in range(CH // grp):
            ridx = gi * grp + (half if reverse else half - 1)
            refs.append(jnp.broadcast_to(cum[ridx:ridx + 1, :], (grp, PAIR_K)))
        ref_rows = refs[0] if len(refs) == 1 else jnp.concatenate(refs, axis=0)
        d = cum - ref_rows
        e = jnp.exp(jnp.where(in_q_half, d, -d))
        qf = jnp.where(in_q_half, q * e, 0.0).astype(BF16)
        kf = jnp.where(in_q_half, 0.0, k * e)
        kk = jnp.concatenate([jnp.where(head0_k, kf, 0.0), jnp.where(head0_k, 0.0, kf)], axis=0)
        s = _dot_nt(qf, kk.astype(BF16))
        if grp < CH:
            s = s * masks_ref[3 - lvl]
        scores = s if scores is None else scores + s

    q3 = q.reshape(CH // DIAG, DIAG, PAIR_K)
    k3 = k.reshape(CH // DIAG, DIAG, PAIR_K)
    c3 = cum.reshape(CH // DIAG, DIAG, PAIR_K)
    sub = lax.broadcasted_iota(jnp.int32, (CH // DIAG, DIAG, PAIR_K), 1)
    pieces = []
    for jj in range(DIAG):
        kb = jnp.broadcast_to(k3[:, jj:jj + 1, :], q3.shape)
        cb = jnp.broadcast_to(c3[:, jj:jj + 1, :], q3.shape)
        valid = (sub <= jj) if reverse else (sub >= jj)
        piece = q3 * kb * jnp.exp(jnp.where(valid, c3 - cb, NEG_BIG))
        pieces.append(piece.reshape(CH, PAIR_K).astype(BF16))
    diag = _dot(jnp.concatenate(pieces, axis=1), ones_mat)
    scores = scores + diag * masks_ref[0]

    lane_v = lax.broadcasted_iota(jnp.int32, (CH, PAIR_V), 1)
    head0_v = lane_v < C_HEAD_V
    v_bd = jnp.concatenate([jnp.where(head0_v, v, 0.0), jnp.where(head0_v, 0.0, v)], axis=0)
    return o + _dot(scores.astype(BF16), v_bd.astype(BF16))


def _gla_kernel(*refs, t_lat, t_ctx, want_ctx):
    (ql_ref, kl_ref, vl_ref, zl_ref, rl_ref, qc_ref, kc_ref, vc_ref, zc_ref, rc_ref,
     w2f_ref, w2b_ref, b2f_ref, b2b_ref, gain_ref, cumf_ref, cumb_ref, ones_ref, masks_ref) = refs[:19]
    if want_ctx:
        yl_ref, yc_ref, stf_ref, stb_ref, of_ref, ob_ref = refs[19:]
    else:
        yl_ref, stf_ref, stb_ref, of_ref, ob_ref = refs[19:]
        yc_ref = zc_ref = None

    def gates(r, w_ref, b_ref):
        x = _dot(r.astype(BF16), w_ref[...]) + b_ref[...]
        return _log_sigmoid(x) * (1.0 / C_GATE_TAU)

    def run(q_ref, k_ref, v_ref, r_ref, n_chunks, store):
        def body(i, carry):
            for reverse in (False, True):
                c = (n_chunks - 1 - i) if reverse else i
                r0 = pl.multiple_of(c * CH, CH)
                rows = pl.ds(r0, CH)
                q = q_ref[0, rows, :] * C_HEAD_K ** -0.5
                k = k_ref[0, rows, :]
                v = v_ref[0, rows, :]
                g = gates(r_ref[0, rows, :], w2b_ref if reverse else w2f_ref,
                          b2b_ref if reverse else b2f_ref)
                o = _gla_chunk(q, k, v, g, stb_ref if reverse else stf_ref,
                               cumb_ref[...] if reverse else cumf_ref[...], ones_ref[...],
                               masks_ref, reverse)
                if store:
                    (ob_ref if reverse else of_ref)[rows, :] = o
            return carry

        lax.fori_loop(0, n_chunks, body, 0)

    def finish(z_ref, y_ref, n_rows):
        blk = min(256, n_rows)

        def body(t, carry):
            r0 = pl.multiple_of(t * blk, blk)
            rows = pl.ds(r0, blk)
            o = of_ref[rows, :] + ob_ref[rows, :]
            z = z_ref[0, rows, :]
            gain = gain_ref[...]
            for h in range(2):
                lanes = slice(h * C_HEAD_V, (h + 1) * C_HEAD_V)
                y = _rms(o[:, lanes]) * gain[:, lanes]
                y_ref[0, rows, lanes] = (y * _silu(z[:, lanes])).astype(BF16)
            return carry

        lax.fori_loop(0, n_rows // blk, body, 0)

    stf_ref[...] = jnp.zeros_like(stf_ref)
    stb_ref[...] = jnp.zeros_like(stb_ref)
    run(qc_ref, kc_ref, vc_ref, rc_ref, t_ctx // CH, want_ctx)
    if want_ctx:
        finish(zc_ref, yc_ref, t_ctx)
    run(ql_ref, kl_ref, vl_ref, rl_ref, t_lat // CH, True)
    finish(zl_ref, yl_ref, t_lat)


def _gla(p_lat, r_lat, p_ctx, r_ctx, w2f, w2b, b2f, b2b, gain, consts, want_ctx):
    b, t_lat, _ = p_lat.shape
    t_ctx = p_ctx.shape[1]
    n_pairs = C_HEADS // 2

    def side(t, with_z):
        out = [pl.BlockSpec((1, t, PAIR_K), lambda i, hp: (i, 0, COL_CQ // PAIR_K + hp)),
               pl.BlockSpec((1, t, PAIR_K), lambda i, hp: (i, 0, COL_CK // PAIR_K + hp)),
               pl.BlockSpec((1, t, PAIR_V), lambda i, hp: (i, 0, COL_CV // PAIR_V + hp))]
        if with_z:
            out.append(pl.BlockSpec((1, t, PAIR_V), lambda i, hp: (i, 0, COL_CZ // PAIR_V + hp)))
        else:
            out.append(pl.BlockSpec((1, 8, PAIR_V), lambda i, hp: (i, 0, COL_CZ // PAIR_V + hp)))
        out.append(pl.BlockSpec((1, t, LANES), lambda i, hp: (i, 0, 0)))
        return out

    def whole(shape):
        nd = len(shape)
        return pl.BlockSpec(shape, lambda i, hp: (0,) * nd)

    cum_f, cum_b, ones, masks = consts
    specs = side(t_lat, True) + side(t_ctx, want_ctx) + [
        pl.BlockSpec((1, LANES, PAIR_K), lambda i, hp: (hp, 0, 0)),
        pl.BlockSpec((1, LANES, PAIR_K), lambda i, hp: (hp, 0, 0)),
        pl.BlockSpec((1, 1, PAIR_K), lambda i, hp: (hp, 0, 0)),
        pl.BlockSpec((1, 1, PAIR_K), lambda i, hp: (hp, 0, 0)),
        pl.BlockSpec((1, 1, PAIR_V), lambda i, hp: (hp, 0, 0)),
        whole(cum_f.shape), whole(cum_b.shape), whole(ones.shape), whole(masks.shape)]
    out_specs = [pl.BlockSpec((1, t_lat, PAIR_V), lambda i, hp: (i, 0, hp))]
    out_shape = [jax.ShapeDtypeStruct((b, t_lat, BR_W), BF16)]
    if want_ctx:
        out_specs.append(pl.BlockSpec((1, t_ctx, PAIR_V), lambda i, hp: (i, 0, hp)))
        out_shape.append(jax.ShapeDtypeStruct((b, t_ctx, BR_W), BF16))

    def kern(*refs):
        refs = list(refs)
        for n in range(10, 15):
            refs[n] = refs[n].at[0]
        _gla_kernel(*refs, t_lat=t_lat, t_ctx=t_ctx, want_ctx=want_ctx)

    res = pl.pallas_call(
        kern,
        grid=(b, n_pairs),
        in_specs=specs,
        out_specs=out_specs,
        out_shape=out_shape,
        scratch_shapes=[pltpu.VMEM((PAIR_V, PAIR_K), F32), pltpu.VMEM((PAIR_V, PAIR_K), F32),
                        pltpu.VMEM((t_lat, PAIR_V), F32), pltpu.VMEM((t_lat, PAIR_V), F32)],
        compiler_params=_cparams(2),
        name="gla",
    )(p_lat, p_lat, p_lat, p_lat, r_lat, p_ctx, p_ctx, p_ctx, p_ctx, r_ctx,
      w2f, w2b, b2f, b2b, gain, cum_f, cum_b, ones, masks)
    return (res[0], res[1]) if want_ctx else (res[0], None)


def _merge_kernel(mg_ref, ya_ref, yb_ref, yc_ref, yd_ref, h_ref, gate_ref, wbr_ref, wout_ref,
                  lng_ref, lnb_ref, o_ref, *, alpha):
    acc = None
    for n, y_ref in enumerate((ya_ref, yb_ref, yc_ref, yd_ref)):
        gate = jax.nn.sigmoid(mg_ref[:, n * D_MODEL:(n + 1) * D_MODEL])
        term = gate * _dot(y_ref[...], wbr_ref[n])
        acc = term if acc is None else acc + term
    y = _dot(acc.astype(BF16), wout_ref[...])
    t = alpha * h_ref[...] + gate_ref[0] * y
    mu = jnp.mean(t, axis=-1, keepdims=True)
    tc = t - mu
    var = jnp.mean(tc * tc, axis=-1, keepdims=True)
    o_ref[...] = tc * lax.rsqrt(var + EPS) * lng_ref[...] + lnb_ref[...]


def _merge(p2, ya, yb, yc, yd, h, gate, rows_per_mod, w_br, w_out, ln_g, ln_b, alpha):
    m = h.shape[0]
    tm = min(256, m)
    assert m % tm == 0 and rows_per_mod % tm == 0
    tiles_per_mod = rows_per_mod // tm
    br_spec = pl.BlockSpec((tm, BR_W), lambda i: (i, 0))
    vec_spec = pl.BlockSpec((1, D_MODEL), lambda i: (0, 0))
    return pl.pallas_call(
        functools.partial(_merge_kernel, alpha=alpha),
        grid=(m // tm,),
        in_specs=[
            pl.BlockSpec((tm, N_BRANCH * D_MODEL), lambda i: (i, COL_MG)),
            br_spec, br_spec, br_spec, br_spec,
            pl.BlockSpec((tm, D_MODEL), lambda i: (i, 0)),
            pl.BlockSpec((1, 1, D_MODEL), lambda i: (i // tiles_per_mod, 0, 0)),
            pl.BlockSpec((N_BRANCH, BR_W, D_MODEL), lambda i: (0, 0, 0)),
            pl.BlockSpec((D_MODEL, D_MODEL), lambda i: (0, 0)),
            vec_spec, vec_spec,
        ],
        out_specs=pl.BlockSpec((tm, D_MODEL), lambda i: (i, 0)),
        out_shape=jax.ShapeDtypeStruct((m, D_MODEL), F32),
        compiler_params=_cparams(1),
        name="merge",
    )(p2, ya, yb, yc, yd, h, gate, w_br, w_out, ln_g.reshape(1, D_MODEL), ln_b.reshape(1, D_MODEL))


def _layer_weights(w_in_l, q_norm_l, k_norm_l, c_w2_l, c_b2_l, c_norm_l):
    cuts = [int(c) for c in np.cumsum(IN_WIDTHS)[:-1]]
    parts = jnp.split(w_in_l, cuts, axis=-1)
    perm = np.concatenate([np.arange(0, HEAD_DIM, 2), np.arange(1, HEAD_DIM, 2)])

    def deinterleave(w, heads):
        return w.reshape(D_MODEL, heads, HEAD_DIM)[:, :, perm].reshape(D_MODEL, heads * HEAD_DIM)

    main = [parts[15], deinterleave(parts[0], A_HEADS), deinterleave(parts[1], A_KV_HEADS), parts[2], parts[3],
            parts[4], parts[5], parts[6], parts[7],
            parts[8], parts[9], parts[10], parts[11],
            parts[13], parts[14]]
    w_main = jnp.concatenate(main, axis=-1).astype(BF16)
    w_r = jnp.pad(parts[12], ((0, 0), (0, LANES - 2 * C_GATE_RANK))).astype(BF16)
    n_pairs = C_HEADS // 2

    def gate_w(i):
        w = jnp.zeros((LANES, C_KEY_W), F32).at[i * C_GATE_RANK:(i + 1) * C_GATE_RANK].set(c_w2_l[i])
        return w.reshape(LANES, n_pairs, PAIR_K).transpose(1, 0, 2).astype(BF16)

    return dict(
        w_main=w_main, w_r=w_r,
        q_gain=q_norm_l[perm].reshape(1, HEAD_DIM), k_gain=k_norm_l[perm].reshape(1, HEAD_DIM),
        w2f=gate_w(0), w2b=gate_w(1),
        b2f=c_b2_l[0].reshape(n_pairs, 1, PAIR_K), b2b=c_b2_l[1].reshape(n_pairs, 1, PAIR_K),
        c_gain=c_norm_l.reshape(n_pairs, 1, PAIR_V))


def _rope_tables(t):
    rows = t // GRID_W
    row = jnp.repeat(jnp.arange(rows), GRID_W).astype(F32)
    col = jnp.tile(jnp.arange(GRID_W), rows).astype(F32)
    inv = ROPE_THETA ** (-jnp.arange(0, AXIS_DIM, 2, dtype=F32) / AXIS_DIM)
    ang = jnp.concatenate([row[:, None] * inv, col[:, None] * inv], -1)
    cos, sin = jnp.cos(ang), jnp.sin(ang)
    return jnp.concatenate([cos, cos], -1), jnp.concatenate([-sin, sin], -1)


def kernel(x, c, ctx, c_ctx, w_mod, b_mod, w_in, q_norm, k_norm, b_conv, c_gate_w2, c_gate_b, c_norm,
           d_conv_w, d_conv_b, d_norm_g, d_norm_b, w_br, w_out, ln_g, ln_b):
    b, t, d = x.shape
    t_ctx = ctx.shape[1]
    depth = w_mod.shape[0]
    assert d == D_MODEL and b < 16
    alpha = (2 * depth) ** 0.25

    cc = jnp.zeros((16, D_MODEL), F32).at[:b].set(c).at[b].set(c_ctx)
    mod = _modulation(cc, w_mod, b_mod)
    cos, sin = _rope_tables(t)
    consts = _gla_constants()

    h_lat = x.reshape(b * t, D_MODEL)
    h_ctx = ctx.reshape(b * t_ctx, D_MODEL)
    for l in range(depth):
        want_ctx = l < depth - 1
        lw = _layer_weights(w_in[l], q_norm[l], k_norm[l], c_gate_w2[l], c_gate_b[l], c_norm[l])
        shift, scale, gate = [mod[l, :, n * D_MODEL:(n + 1) * D_MODEL].reshape(16, 1, D_MODEL) for n in range(3)]
        p_lat, r_lat = _in_projection(h_lat, shift, scale, t, lw["w_main"], lw["w_r"])
        p_ctx, r_ctx = _in_projection(h_ctx, shift[b:], scale[b:], b * t_ctx, lw["w_main"], lw["w_r"])
        p_lat3 = p_lat.reshape(b, t, N_MAIN)
        p_ctx3 = p_ctx.reshape(b, t_ctx, N_MAIN)
        r_lat3 = r_lat.reshape(b, t, LANES)
        r_ctx3 = r_ctx.reshape(b, t_ctx, LANES)
        w_br_l = w_br[l].astype(BF16)
        w_out_l = w_out[l].astype(BF16)

        ya_l = _attention(p_lat3, p_lat3, p_ctx3, lw["q_gain"], lw["k_gain"], cos, sin)
        yb_l, yd_l = _local_branches(p_lat, t, b_conv[l], d_conv_w[l], d_conv_b[l], d_norm_g[l], d_norm_b[l])
        yc_l, yc_c = _gla(p_lat3, r_lat3, p_ctx3, r_ctx3, lw["w2f"], lw["w2b"], lw["b2f"], lw["b2b"],
                          lw["c_gain"], consts, want_ctx)
        h_lat_new = _merge(p_lat, ya_l.reshape(b * t, BR_W), yb_l, yc_l.reshape(b * t, BR_W), yd_l,
                           h_lat, gate, t, w_br_l, w_out_l, ln_g[l], ln_b[l], alpha)
        if want_ctx:
            ya_c = _attention(p_ctx3, None, p_ctx3, lw["q_gain"], lw["k_gain"], None, None)
            yb_c, yd_c = _local_branches(p_ctx, t_ctx, b_conv[l], d_conv_w[l], d_conv_b[l], d_norm_g[l],
                                         d_norm_b[l])
            h_ctx = _merge(p_ctx, ya_c.reshape(b * t_ctx, BR_W), yb_c, yc_c.reshape(b * t_ctx, BR_W), yd_c,
                           h_ctx, gate[b:], b * t_ctx, w_br_l, w_out_l, ln_g[l], ln_b[l], alpha)
        h_lat = h_lat_new
    return h_lat.reshape(b, t, D_MODEL)
```

```python
import functools

import numpy as np
import jax
import jax.numpy as jnp
from jax import lax
from jax.experimental import pallas as pl
from jax.experimental.pallas import tpu as pltpu

F32 = jnp.float32
BF16 = jnp.bfloat16

D_MODEL = 1024
GRID_W = 64
N_BRANCH = 4
BR_W = D_MODEL // 2
HEAD_DIM = 128
A_HEADS = BR_W // HEAD_DIM
A_KV_HEADS = A_HEADS // 2
ROPE_THETA = 10000.0
AXIS_DIM = HEAD_DIM // 2
B_CONV = 3
C_HEADS = 4
C_HEAD_K = BR_W // (2 * C_HEADS)
C_HEAD_V = BR_W // C_HEADS
C_KEY_W = C_HEADS * C_HEAD_K
C_GATE_RANK = 16
C_GATE_TAU = 16.0
C_CHUNK = 64
D_CONV = 31
EPS = 1e-6
IN_WIDTHS = (
    A_HEADS * HEAD_DIM, A_KV_HEADS * HEAD_DIM, A_KV_HEADS * HEAD_DIM, BR_W,
    BR_W, BR_W, BR_W, BR_W,
    C_KEY_W, C_KEY_W, C_HEADS * C_HEAD_V, BR_W, 2 * C_GATE_RANK,
    2 * BR_W, BR_W,
    N_BRANCH * D_MODEL,
)

COL_MG = 0
COL_AQ = 4096
COL_AK = COL_AQ + 512
COL_AV = COL_AK + 256
COL_AZ = COL_AV + 256
COL_BG = COL_AZ + 512
COL_BC = COL_BG + 512
COL_BX = COL_BC + 512
COL_BZ = COL_BX + 512
COL_CQ = COL_BZ + 512
COL_CK = COL_CQ + 256
COL_CV = COL_CK + 256
COL_CZ = COL_CV + 512
COL_DA = COL_CZ + 512
COL_DG = COL_DA + 512
COL_DZ = COL_DG + 512
N_MAIN = COL_DZ + 512
LANES = 128
PROJ_TN = 512
VMEM_LIMIT = 56 * 1024 * 1024


def _cparams(n_axes):
    return pltpu.CompilerParams(dimension_semantics=("arbitrary",) * n_axes,
                                vmem_limit_bytes=VMEM_LIMIT)


def _silu(x):
    return x * jax.nn.sigmoid(x)


def _dot(a, b):
    return jnp.dot(a, b, preferred_element_type=F32)


def _dot_nt(a, b):
    return lax.dot_general(a, b, (((1,), (1,)), ((), ())), preferred_element_type=F32)


def _dot_tn(a, b):
    return lax.dot_general(a, b, (((0,), (0,)), ((), ())), preferred_element_type=F32)


def _mod_kernel(c_ref, w_ref, b_ref, o_ref):
    s = _silu(c_ref[...])
    o_ref[0] = _dot(s.astype(BF16), w_ref[0].astype(BF16)) + b_ref[0]


def _modulation(cc, w_mod, b_mod):
    depth = w_mod.shape[0]
    n_rows = cc.shape[0]
    return pl.pallas_call(
        _mod_kernel,
        grid=(depth, 3),
        in_specs=[
            pl.BlockSpec((n_rows, D_MODEL), lambda l, j: (0, 0)),
            pl.BlockSpec((1, D_MODEL, D_MODEL), lambda l, j: (l, 0, j)),
            pl.BlockSpec((1, 1, D_MODEL), lambda l, j: (l, 0, j)),
        ],
        out_specs=pl.BlockSpec((1, n_rows, D_MODEL), lambda l, j: (l, 0, j)),
        out_shape=jax.ShapeDtypeStruct((depth, n_rows, 3 * D_MODEL), F32),
        compiler_params=_cparams(2),
        name="modulation",
    )(cc, w_mod, b_mod.reshape(depth, 1, 3 * D_MODEL))


def _inproj_kernel(x_ref, shift_ref, scale_ref, w_ref, wr_ref, o_ref, or_ref, u_ref, *, tm, sub):
    j = pl.program_id(1)

    @pl.when(j == 0)
    def _():
        one_plus = 1.0 + scale_ref[0]
        shift = shift_ref[0]

        def body(t, carry):
            r0 = pl.multiple_of(t * sub, sub)
            x = x_ref[pl.ds(r0, sub), :]
            mu = jnp.mean(x, axis=-1, keepdims=True)
            xc = x - mu
            var = jnp.mean(xc * xc, axis=-1, keepdims=True)
            u = xc * lax.rsqrt(var + EPS) * one_plus + shift
            u_ref[pl.ds(r0, sub), :] = u.astype(BF16)
            return carry

        lax.fori_loop(0, tm // sub, body, 0)
        or_ref[...] = _dot(u_ref[...], wr_ref[...]).astype(BF16)

    o_ref[...] = _dot(u_ref[...], w_ref[...]).astype(BF16)


def _in_projection(h, shift, scale, rows_per_mod, w_main, w_r):
    m = h.shape[0]
    tm = min(2048, rows_per_mod)
    assert m % tm == 0 and rows_per_mod % tm == 0
    tiles_per_mod = rows_per_mod // tm
    n_tiles = N_MAIN // PROJ_TN
    kern = functools.partial(_inproj_kernel, tm=tm, sub=min(128, tm))
    return pl.pallas_call(
        kern,
        grid=(m // tm, n_tiles),
        in_specs=[
            pl.BlockSpec((tm, D_MODEL), lambda i, j: (i, 0)),
            pl.BlockSpec((1, 1, D_MODEL), lambda i, j: (i // tiles_per_mod, 0, 0)),
            pl.BlockSpec((1, 1, D_MODEL), lambda i, j: (i // tiles_per_mod, 0, 0)),
            pl.BlockSpec((D_MODEL, PROJ_TN), lambda i, j: (0, j)),
            pl.BlockSpec((D_MODEL, LANES), lambda i, j: (0, 0)),
        ],
        out_specs=[
            pl.BlockSpec((tm, PROJ_TN), lambda i, j: (i, j)),
            pl.BlockSpec((tm, LANES), lambda i, j: (i, 0)),
        ],
        out_shape=[
            jax.ShapeDtypeStruct((m, N_MAIN), BF16),
            jax.ShapeDtypeStruct((m, LANES), BF16),
        ],
        scratch_shapes=[pltpu.VMEM((tm, D_MODEL), BF16)],
        compiler_params=_cparams(2),
        name="in_projection",
    )(h, shift, scale, w_main, w_r)


def _rms(x):
    return x * lax.rsqrt(jnp.mean(x * x, axis=-1, keepdims=True) + EPS)


def _rope(x, cos, sin):
    return x * cos + pltpu.roll(x, HEAD_DIM // 2, 1) * sin


def _attn_kernel(*refs, n_lat, n_ctx, tq, kblk):
    if n_lat:
        (q_ref, z_ref, kl_ref, vl_ref, kc_ref, vc_ref, qg_ref, kg_ref, cos_ref, sin_ref,
         o_ref, kn_ref, vn_ref) = refs
    else:
        q_ref, z_ref, kc_ref, vc_ref, qg_ref, kg_ref, o_ref, kn_ref, vn_ref = refs
    qi = pl.program_id(2)
    kg = kg_ref[...]

    @pl.when(qi == 0)
    def _prep():
        if n_lat:
            def body(t, carry):
                r0 = pl.multiple_of(t * kblk, kblk)
                kn = _rms(kl_ref[0, pl.ds(r0, kblk), :].astype(F32)) * kg
                kn = _rope(kn, cos_ref[pl.ds(r0, kblk), :], sin_ref[pl.ds(r0, kblk), :])
                kn_ref[pl.ds(r0, kblk), :] = kn.astype(BF16)
                vn_ref[pl.ds(r0, kblk), :] = vl_ref[0, pl.ds(r0, kblk), :]
                return carry

            lax.fori_loop(0, n_lat // kblk, body, 0)
        kn_ref[n_lat:n_lat + n_ctx, :] = (_rms(kc_ref[0].astype(F32)) * kg).astype(BF16)
        vn_ref[n_lat:n_lat + n_ctx, :] = vc_ref[0]

    qg = qg_ref[...]
    if n_lat:
        q0 = pl.multiple_of(qi * tq, tq)
        cos = cos_ref[pl.ds(q0, tq), :]
        sin = sin_ref[pl.ds(q0, tq), :]
    for g in range(2):
        lanes = slice(g * HEAD_DIM, (g + 1) * HEAD_DIM)
        qn = _rms(q_ref[0, :, lanes].astype(F32)) * qg
        if n_lat:
            qn = _rope(qn, cos, sin)
        qn = (qn * HEAD_DIM ** -0.5).astype(BF16)
        s = _dot_nt(qn, kn_ref[...])
        p = jnp.exp(s - jnp.max(s, axis=-1, keepdims=True))
        l = jnp.sum(p, axis=-1, keepdims=True)
        o = _dot(p.astype(BF16), vn_ref[...]) / l
        o_ref[0, :, lanes] = (o * _silu(z_ref[0, :, lanes].astype(F32))).astype(BF16)


def _attention(p_q, p_lat, p_ctx, q_gain, k_gain, cos, sin):
    b, t_q, _ = p_q.shape
    n_ctx = p_ctx.shape[1]
    n_lat = 0 if p_lat is None else p_lat.shape[1]
    tq = min(256, t_q)
    assert t_q % tq == 0
    kblk = min(256, n_lat) if n_lat else 0
    hw = 2 * HEAD_DIM
    q_spec = pl.BlockSpec((1, tq, hw), lambda i, kv, qi: (i, qi, COL_AQ // hw + kv))
    z_spec = pl.BlockSpec((1, tq, hw), lambda i, kv, qi: (i, qi, COL_AZ // hw + kv))

    def kv_spec(n, col):
        return pl.BlockSpec((1, n, HEAD_DIM), lambda i, kv, qi: (i, 0, col // HEAD_DIM + kv))

    gain_spec = pl.BlockSpec((1, HEAD_DIM), lambda i, kv, qi: (0, 0))
    args = [p_q, p_q]
    specs = [q_spec, z_spec]
    if n_lat:
        args += [p_lat, p_lat]
        specs += [kv_spec(n_lat, COL_AK), kv_spec(n_lat, COL_AV)]
    args += [p_ctx, p_ctx, q_gain, k_gain]
    specs += [kv_spec(n_ctx, COL_AK), kv_spec(n_ctx, COL_AV), gain_spec, gain_spec]
    if n_lat:
        tab_spec = pl.BlockSpec((n_lat, HEAD_DIM), lambda i, kv, qi: (0, 0))
        args += [cos, sin]
        specs += [tab_spec, tab_spec]
    kern = functools.partial(_attn_kernel, n_lat=n_lat, n_ctx=n_ctx, tq=tq, kblk=kblk)
    return pl.pallas_call(
        kern,
        grid=(b, A_KV_HEADS, t_q // tq),
        in_specs=specs,
        out_specs=pl.BlockSpec((1, tq, hw), lambda i, kv, qi: (i, qi, kv)),
        out_shape=jax.ShapeDtypeStruct((b, t_q, BR_W), BF16),
        scratch_shapes=[pltpu.VMEM((n_lat + n_ctx, HEAD_DIM), BF16),
                        pltpu.VMEM((n_lat + n_ctx, HEAD_DIM), BF16)],
        compiler_params=_cparams(3),
        name="attention_lat" if n_lat else "attention_ctx",
    )(*args)


D_HALO = 16
B_HALO = 16
CONV_RB = 64


def _local_kernel(bg_ref, bc_ref, bcp_ref, bcn_ref, bx_ref, bxp_ref, bxn_ref, bz_ref,
                  da_ref, dap_ref, dan_ref, dg_ref, dgp_ref, dgn_ref, dz_ref,
                  bw_ref, dw_ref, db_ref, dgain_ref, dbeta_ref,
                  yb_ref, yd_ref, tbuf, gbuf, hbuf, *, tt, tiles_per_seq):
    i = pl.program_id(0)
    pos = i % tiles_per_seq
    keep_prev = (pos != 0).astype(F32)
    keep_next = (pos != tiles_per_seq - 1).astype(F32)

    def ld(ref):
        return ref[...].astype(F32)

    t_cur = ld(bc_ref) * ld(bx_ref)
    tbuf[0:B_HALO, :] = ld(bcp_ref) * ld(bxp_ref) * keep_prev
    tbuf[B_HALO:B_HALO + tt, :] = t_cur
    tbuf[B_HALO + tt:2 * B_HALO + tt, :] = ld(bcn_ref) * ld(bxn_ref) * keep_next
    conv = (bw_ref[0:1, :] * tbuf[B_HALO - 1:B_HALO - 1 + tt, :] + bw_ref[1:2, :] * t_cur
            + bw_ref[2:3, :] * tbuf[B_HALO + 1:B_HALO + 1 + tt, :])
    yb_ref[...] = (ld(bg_ref) * conv * _silu(ld(bz_ref))).astype(BF16)

    gbuf[0:D_HALO, :] = ld(dap_ref) * jax.nn.sigmoid(ld(dgp_ref)) * keep_prev
    gbuf[D_HALO:D_HALO + tt, :] = ld(da_ref) * jax.nn.sigmoid(ld(dg_ref))
    gbuf[D_HALO + tt:2 * D_HALO + tt, :] = ld(dan_ref) * jax.nn.sigmoid(ld(dgn_ref)) * keep_next
    base = D_HALO - D_CONV // 2
    for cb in range(BR_W // LANES):
        lanes = slice(cb * LANES, (cb + 1) * LANES)
        for rb in range(tt // CONV_RB):
            r0 = rb * CONV_RB
            acc = dw_ref[0:1, lanes] * gbuf[base + r0:base + r0 + CONV_RB, lanes]
            for k in range(1, D_CONV):
                acc = acc + dw_ref[k:k + 1, lanes] * gbuf[base + r0 + k:base + r0 + k + CONV_RB, lanes]
            hbuf[r0:r0 + CONV_RB, lanes] = acc
    hh = hbuf[...] + db_ref[...]
    mu = jnp.mean(hh, axis=-1, keepdims=True)
    hc = hh - mu
    var = jnp.mean(hc * hc, axis=-1, keepdims=True)
    hn = hc * lax.rsqrt(var + EPS) * dgain_ref[...] + dbeta_ref[...]
    yd_ref[...] = (_silu(hn) * _silu(ld(dz_ref))).astype(BF16)


def _local_branches(p2, seq_len, b_w, d_w, d_b, d_g, d_beta):
    m = p2.shape[0]
    tt = min(256, seq_len)
    assert seq_len % tt == 0 and m % seq_len == 0
    tiles_per_seq = seq_len // tt
    n_tiles = m // tt

    def cur(col):
        return pl.BlockSpec((tt, BR_W), lambda i: (i, col // BR_W))

    def prev(col, halo):
        per = tt // halo
        return pl.BlockSpec((halo, BR_W), lambda i: (jnp.maximum(i * per - 1, 0), col // BR_W))

    def nxt(col, halo):
        per = tt // halo
        last = m // halo - 1
        return pl.BlockSpec((halo, BR_W), lambda i: (jnp.minimum((i + 1) * per, last), col // BR_W))

    def small(rows):
        return pl.BlockSpec((rows, BR_W), lambda i: (0, 0))

    specs = [cur(COL_BG),
             cur(COL_BC), prev(COL_BC, B_HALO), nxt(COL_BC, B_HALO),
             cur(COL_BX), prev(COL_BX, B_HALO), nxt(COL_BX, B_HALO),
             cur(COL_BZ),
             cur(COL_DA), prev(COL_DA, D_HALO), nxt(COL_DA, D_HALO),
             cur(COL_DG), prev(COL_DG, D_HALO), nxt(COL_DG, D_HALO),
             cur(COL_DZ),
             small(B_CONV), small(D_CONV), small(1), small(1), small(1)]
    kern = functools.partial(_local_kernel, tt=tt, tiles_per_seq=tiles_per_seq)
    out_spec = pl.BlockSpec((tt, BR_W), lambda i: (i, 0))
    return pl.pallas_call(
        kern,
        grid=(n_tiles,),
        in_specs=specs,
        out_specs=[out_spec, out_spec],
        out_shape=[jax.ShapeDtypeStruct((m, BR_W), BF16), jax.ShapeDtypeStruct((m, BR_W), BF16)],
        scratch_shapes=[pltpu.VMEM((tt + 2 * B_HALO, BR_W), F32),
                        pltpu.VMEM((tt + 2 * D_HALO, BR_W), F32),
                        pltpu.VMEM((tt, BR_W), F32)],
        compiler_params=_cparams(1),
        name="local_branches",
    )(*([p2] * 15), b_w, d_w, d_b.reshape(1, BR_W), d_g.reshape(1, BR_W), d_beta.reshape(1, BR_W))


CH = C_CHUNK
PAIR_K = 2 * C_HEAD_K
PAIR_V = 2 * C_HEAD_V
LEVELS = (64, 32, 16, 8, 4, 2)
GLA_UNROLL = 4
LOG2E = 1.4426950408889634
ROW_EQ = 0
ROW_EK = CH
ROW_LAST = 2 * CH
ROW_LVL = 2 * CH + 8
N_EROWS = ROW_LVL + len(LEVELS) * CH


def _gla_constants():
    idx = np.arange(CH)
    cols = np.arange(2 * CH) % CH
    emats, masks = [], []
    for reverse in (False, True):
        tri = (idx[None, :] >= idx[:, None]) if reverse else (idx[None, :] <= idx[:, None])
        tri = tri.astype(np.float32)
        edge = tri[0] if reverse else tri[CH - 1]
        blocks = [tri, edge[None, :] - tri, np.tile(edge[None, :], (8, 1))]
        lvl_masks = []
        for grp in LEVELS:
            half = grp // 2
            in_q = (idx % grp < half) if reverse else (idx % grp >= half)
            ref = (idx // grp) * grp + (half if reverse else half - 1)
            d = tri - tri[ref]
            blocks.append(np.where(in_q[:, None], d, -d))
            in_q_col = (cols % grp < half) if reverse else (cols % grp >= half)
            lvl_masks.append((idx[:, None] // grp == cols[None, :] // grp) & in_q[:, None] & ~in_q_col[None, :])
        lvl_masks.append(idx[:, None] == cols[None, :])
        emat = np.concatenate(blocks, axis=0)
        assert emat.shape == (N_EROWS, CH)
        emats.append(np.concatenate([emat] * 3, axis=1))
        masks.append(np.stack(lvl_masks).astype(np.float32))
    return (jnp.asarray(emats[0], BF16), jnp.asarray(emats[1], BF16), jnp.asarray(np.stack(masks), F32))


def _split3(g):
    g1 = g.astype(BF16)
    r1 = g - g1.astype(F32)
    g2 = r1.astype(BF16)
    g3 = (r1 - g2.astype(F32)).astype(BF16)
    return g1, g2, g3


def _log2_sigmoid(x):
    e = jnp.exp2(jnp.abs(x) * -LOG2E)
    return jnp.minimum(x, 0.0) * LOG2E - jnp.log2(1.0 + e)


def _gla_kernel(*refs, t_lat, t_ctx, want_ctx):
    (ql_ref, kl_ref, vl_ref, zl_ref, rl_ref, qc_ref, kc_ref, vc_ref, zc_ref, rc_ref,
     w2_ref, b2_ref, gain_ref, ematf_ref, ematb_ref, masks_ref) = refs[:16]
    if want_ctx:
        yl_ref, yc_ref = refs[16:18]
        scratch = refs[18:]
    else:
        yl_ref = refs[16]
        scratch = refs[17:]
        yc_ref = zc_ref = None
    stf_ref, stb_ref, of_ref, ob_ref, g1_ref, g2_ref, g3_ref = scratch
    unroll = GLA_UNROLL

    def gates(r_ref, n_rows):
        blk = min(256, n_rows)

        def body(t, carry):
            rows = pl.ds(pl.multiple_of(t * blk, blk), blk)
            x = _dot(r_ref[0, rows, :], w2_ref[...]) + b2_ref[...]
            g1, g2, g3 = _split3(_log2_sigmoid(x) * (1.0 / C_GATE_TAU))
            g1_ref[rows, :] = g1
            g2_ref[rows, :] = g2
            g3_ref[rows, :] = g3
            return carry

        lax.fori_loop(0, n_rows // blk, body, 0)

    def run(q_ref, k_ref, v_ref, n_chunks, want_out):
        assert n_chunks % unroll == 0
        lane_k = lax.broadcasted_iota(jnp.int32, (CH, PAIR_K), 1)
        head0 = lane_k < C_HEAD_K
        zero_v = jnp.zeros((CH, C_HEAD_V), BF16)

        def body(i, carry):
            work = []
            for reverse in (False, True):
                lanes = slice(PAIR_K, 2 * PAIR_K) if reverse else slice(0, PAIR_K)
                rows = []
                for u in range(unroll):
                    c = i * unroll + u
                    c = (n_chunks - 1 - c) if reverse else c
                    rows.append(pl.ds(pl.multiple_of(c * CH, CH), CH))
                rhs = jnp.concatenate(
                    [jnp.concatenate([g_ref[r, lanes] for r in rows], axis=1)
                     for g_ref in (g1_ref, g2_ref, g3_ref)], axis=0)
                emat = ematb_ref[...] if reverse else ematf_ref[...]
                e_all = jnp.exp2(_dot(emat, rhs))
                for u in range(unroll):
                    work.append(dict(reverse=reverse, rows=rows[u], u=u,
                                     e=e_all[:, u * PAIR_K:(u + 1) * PAIR_K]))
            work.sort(key=lambda w: w["u"])

            for w in work:
                st_ref = stb_ref if w["reverse"] else stf_ref
                e = w["e"]
                q = q_ref[0, w["rows"], :].astype(F32) * C_HEAD_K ** -0.5
                k = k_ref[0, w["rows"], :].astype(F32)
                v = v_ref[0, w["rows"], :]
                k0 = jnp.where(head0, k, 0.0)
                k1 = jnp.where(head0, 0.0, k)
                st = st_ref[...]
                e_k = e[ROW_EK:ROW_EK + CH]
                k_dec = jnp.concatenate([k0 * e_k, k1 * e_k], axis=0).astype(BF16)
                v_rows = jnp.concatenate([v[:, :C_HEAD_V], v[:, C_HEAD_V:]], axis=0)
                st_ref[...] = e[ROW_LAST:ROW_LAST + 1] * st + _dot_tn(v_rows, k_dec)
                if want_out:
                    qe = q * e[ROW_EQ:ROW_EQ + CH]
                    q_rows = jnp.concatenate([jnp.where(head0, qe, 0.0), jnp.where(head0, 0.0, qe)], axis=0)
                    o_st = _dot_nt(q_rows.astype(BF16), st.astype(BF16))
                    w.update(q=q, k0=k0, k1=k1, v=v, o=jnp.concatenate([o_st[:CH], o_st[CH:]], axis=1))
            if not want_out:
                return carry

            for w in work:
                masks = masks_ref.at[1 if w["reverse"] else 0]
                q, k0, k1, e = w["q"], w["k0"], w["k1"], w["e"]
                k_heads = jnp.concatenate([k0, k1], axis=0)
                scores = _dot_nt(q.astype(BF16), k_heads.astype(BF16)) * masks[len(LEVELS)]
                for lvl in range(len(LEVELS)):
                    e_l = e[ROW_LVL + lvl * CH:ROW_LVL + (lvl + 1) * CH]
                    kk = jnp.concatenate([k0 * e_l, k1 * e_l], axis=0).astype(BF16)
                    scores = scores + _dot_nt((q * e_l).astype(BF16), kk) * masks[lvl]
                w["scores"] = scores.astype(BF16)

            for w in work:
                v = w["v"]
                v_bd = jnp.concatenate([jnp.concatenate([v[:, :C_HEAD_V], zero_v], axis=1),
                                        jnp.concatenate([zero_v, v[:, C_HEAD_V:]], axis=1)], axis=0)
                (ob_ref if w["reverse"] else of_ref)[w["rows"], :] = w["o"] + _dot(w["scores"], v_bd)
            return carry

        lax.fori_loop(0, n_chunks // unroll, body, 0)

    def finish(z_ref, y_ref, n_rows):
        blk = min(256, n_rows)

        def body(t, carry):
            rows = pl.ds(pl.multiple_of(t * blk, blk), blk)
            o = of_ref[rows, :] + ob_ref[rows, :]
            z = z_ref[0, rows, :].astype(F32)
            gain = gain_ref[...]
            for h in range(2):
                lanes = slice(h * C_HEAD_V, (h + 1) * C_HEAD_V)
                y = _rms(o[:, lanes]) * gain[:, lanes]
                y_ref[0, rows, lanes] = (y * _silu(z[:, lanes])).astype(BF16)
            return carry

        lax.fori_loop(0, n_rows // blk, body, 0)

    stf_ref[...] = jnp.zeros_like(stf_ref)
    stb_ref[...] = jnp.zeros_like(stb_ref)
    gates(rc_ref, t_ctx)
    run(qc_ref, kc_ref, vc_ref, t_ctx // CH, want_ctx)
    if want_ctx:
        finish(zc_ref, yc_ref, t_ctx)
    gates(rl_ref, t_lat)
    run(ql_ref, kl_ref, vl_ref, t_lat // CH, True)
    finish(zl_ref, yl_ref, t_lat)


def _gla(p_lat, r_lat, p_ctx, r_ctx, w2, b2, gain, consts, want_ctx):
    b, t_lat, _ = p_lat.shape
    t_ctx = p_ctx.shape[1]
    n_pairs = C_HEADS // 2

    def side(t, with_z):
        z_rows = t if with_z else 16
        return [pl.BlockSpec((1, t, PAIR_K), lambda i, hp: (i, 0, COL_CQ // PAIR_K + hp)),
                pl.BlockSpec((1, t, PAIR_K), lambda i, hp: (i, 0, COL_CK // PAIR_K + hp)),
                pl.BlockSpec((1, t, PAIR_V), lambda i, hp: (i, 0, COL_CV // PAIR_V + hp)),
                pl.BlockSpec((1, z_rows, PAIR_V), lambda i, hp: (i, 0, COL_CZ // PAIR_V + hp)),
                pl.BlockSpec((1, t, LANES), lambda i, hp: (i, 0, 0))]

    def whole(shape):
        nd = len(shape)
        return pl.BlockSpec(shape, lambda i, hp: (0,) * nd)

    emat_f, emat_b, masks = consts
    specs = side(t_lat, True) + side(t_ctx, want_ctx) + [
        pl.BlockSpec((1, LANES, 2 * PAIR_K), lambda i, hp: (hp, 0, 0)),
        pl.BlockSpec((1, 1, 2 * PAIR_K), lambda i, hp: (hp, 0, 0)),
        pl.BlockSpec((1, 1, PAIR_V), lambda i, hp: (hp, 0, 0)),
        whole(emat_f.shape), whole(emat_b.shape), whole(masks.shape)]
    out_specs = [pl.BlockSpec((1, t_lat, PAIR_V), lambda i, hp: (i, 0, hp))]
    out_shape = [jax.ShapeDtypeStruct((b, t_lat, BR_W), BF16)]
    if want_ctx:
        out_specs.append(pl.BlockSpec((1, t_ctx, PAIR_V), lambda i, hp: (i, 0, hp)))
        out_shape.append(jax.ShapeDtypeStruct((b, t_ctx, BR_W), BF16))

    def kern(*refs):
        refs = list(refs)
        for n in range(10, 13):
            refs[n] = refs[n].at[0]
        _gla_kernel(*refs, t_lat=t_lat, t_ctx=t_ctx, want_ctx=want_ctx)

    res = pl.pallas_call(
        kern,
        grid=(b, n_pairs),
        in_specs=specs,
        out_specs=out_specs,
        out_shape=out_shape,
        scratch_shapes=[pltpu.VMEM((C_HEAD_V, PAIR_K), F32), pltpu.VMEM((C_HEAD_V, PAIR_K), F32),
                        pltpu.VMEM((t_lat, PAIR_V), F32), pltpu.VMEM((t_lat, PAIR_V), F32),
                        pltpu.VMEM((t_lat, 2 * PAIR_K), BF16), pltpu.VMEM((t_lat, 2 * PAIR_K), BF16),
                        pltpu.VMEM((t_lat, 2 * PAIR_K), BF16)],
        compiler_params=_cparams(2),
        name="gla",
    )(p_lat, p_lat, p_lat, p_lat, r_lat, p_ctx, p_ctx, p_ctx, p_ctx, r_ctx,
      w2, b2, gain, emat_f, emat_b, masks)
    return (res[0], res[1]) if want_ctx else (res[0], None)


def _merge_kernel(mg_ref, ya_ref, yb_ref, yc_ref, yd_ref, h_ref, gate_ref, wbr_ref, wout_ref,
                  lng_ref, lnb_ref, o_ref, *, alpha):
    acc = None
    for n, y_ref in enumerate((ya_ref, yb_ref, yc_ref, yd_ref)):
        gate = jax.nn.sigmoid(mg_ref[:, n * D_MODEL:(n + 1) * D_MODEL].astype(F32))
        term = gate * _dot(y_ref[...], wbr_ref[n])
        acc = term if acc is None else acc + term
    y = _dot(acc.astype(BF16), wout_ref[...])
    t = alpha * h_ref[...] + gate_ref[0] * y
    mu = jnp.mean(t, axis=-1, keepdims=True)
    tc = t - mu
    var = jnp.mean(tc * tc, axis=-1, keepdims=True)
    o_ref[...] = tc * lax.rsqrt(var + EPS) * lng_ref[...] + lnb_ref[...]


def _merge(p2, ya, yb, yc, yd, h, gate, rows_per_mod, w_br, w_out, ln_g, ln_b, alpha):
    m = h.shape[0]
    tm = min(256, m)
    assert m % tm == 0 and rows_per_mod % tm == 0
    tiles_per_mod = rows_per_mod // tm
    br_spec = pl.BlockSpec((tm, BR_W), lambda i: (i, 0))
    vec_spec = pl.BlockSpec((1, D_MODEL), lambda i: (0, 0))
    return pl.pallas_call(
        functools.partial(_merge_kernel, alpha=alpha),
        grid=(m // tm,),
        in_specs=[
            pl.BlockSpec((tm, N_BRANCH * D_MODEL), lambda i: (i, COL_MG)),
            br_spec, br_spec, br_spec, br_spec,
            pl.BlockSpec((tm, D_MODEL), lambda i: (i, 0)),
            pl.BlockSpec((1, 1, D_MODEL), lambda i: (i // tiles_per_mod, 0, 0)),
            pl.BlockSpec((N_BRANCH, BR_W, D_MODEL), lambda i: (0, 0, 0)),
            pl.BlockSpec((D_MODEL, D_MODEL), lambda i: (0, 0)),
            vec_spec, vec_spec,
        ],
        out_specs=pl.BlockSpec((tm, D_MODEL), lambda i: (i, 0)),
        out_shape=jax.ShapeDtypeStruct((m, D_MODEL), F32),
        compiler_params=_cparams(1),
        name="merge",
    )(p2, ya, yb, yc, yd, h, gate, w_br, w_out, ln_g.reshape(1, D_MODEL), ln_b.reshape(1, D_MODEL))


def _layer_weights(w_in_l, q_norm_l, k_norm_l, c_w2_l, c_b2_l, c_norm_l):
    cuts = [int(c) for c in np.cumsum(IN_WIDTHS)[:-1]]
    parts = jnp.split(w_in_l, cuts, axis=-1)
    perm = np.concatenate([np.arange(0, HEAD_DIM, 2), np.arange(1, HEAD_DIM, 2)])

    def deinterleave(w, heads):
        return w.reshape(D_MODEL, heads, HEAD_DIM)[:, :, perm].reshape(D_MODEL, heads * HEAD_DIM)

    main = [parts[15], deinterleave(parts[0], A_HEADS), deinterleave(parts[1], A_KV_HEADS), parts[2], parts[3],
            parts[4], parts[5], parts[6], parts[7],
            parts[8], parts[9], parts[10], parts[11],
            parts[13], parts[14]]
    w_main = jnp.concatenate(main, axis=-1).astype(BF16)
    w_r = jnp.pad(parts[12], ((0, 0), (0, LANES - 2 * C_GATE_RANK))).astype(BF16)
    n_pairs = C_HEADS // 2

    def gate_w(i):
        w = jnp.zeros((LANES, C_KEY_W), F32).at[i * C_GATE_RANK:(i + 1) * C_GATE_RANK].set(c_w2_l[i])
        return w.reshape(LANES, n_pairs, PAIR_K).transpose(1, 0, 2)

    w2 = jnp.concatenate([gate_w(0), gate_w(1)], axis=-1).astype(BF16)
    b2 = jnp.concatenate([c_b2_l[0].reshape(n_pairs, 1, PAIR_K), c_b2_l[1].reshape(n_pairs, 1, PAIR_K)], axis=-1)
    return dict(
        w_main=w_main, w_r=w_r,
        q_gain=q_norm_l[perm].reshape(1, HEAD_DIM), k_gain=k_norm_l[perm].reshape(1, HEAD_DIM),
        w2=w2, b2=b2, c_gain=c_norm_l.reshape(n_pairs, 1, PAIR_V))


def _rope_tables(t):
    rows = t // GRID_W
    row = jnp.repeat(jnp.arange(rows), GRID_W).astype(F32)
    col = jnp.tile(jnp.arange(GRID_W), rows).astype(F32)
    inv = ROPE_THETA ** (-jnp.arange(0, AXIS_DIM, 2, dtype=F32) / AXIS_DIM)
    ang = jnp.concatenate([row[:, None] * inv, col[:, None] * inv], -1)
    cos, sin = jnp.cos(ang), jnp.sin(ang)
    return jnp.concatenate([cos, cos], -1), jnp.concatenate([-sin, sin], -1)


def kernel(x, c, ctx, c_ctx, w_mod, b_mod, w_in, q_norm, k_norm, b_conv, c_gate_w2, c_gate_b, c_norm,
           d_conv_w, d_conv_b, d_norm_g, d_norm_b, w_br, w_out, ln_g, ln_b):
    b, t, d = x.shape
    t_ctx = ctx.shape[1]
    depth = w_mod.shape[0]
    assert d == D_MODEL and b < 16
    alpha = (2 * depth) ** 0.25

    cc = jnp.zeros((16, D_MODEL), F32).at[:b].set(c).at[b].set(c_ctx)
    mod = _modulation(cc, w_mod, b_mod)
    cos, sin = _rope_tables(t)
    consts = _gla_constants()

    h_lat = x.reshape(b * t, D_MODEL)
    h_ctx = ctx.reshape(b * t_ctx, D_MODEL)
    for l in range(depth):
        want_ctx = l < depth - 1
        lw = _layer_weights(w_in[l], q_norm[l], k_norm[l], c_gate_w2[l], c_gate_b[l], c_norm[l])
        shift, scale, gate = [mod[l, :, n * D_MODEL:(n + 1) * D_MODEL].reshape(16, 1, D_MODEL) for n in range(3)]
        p_lat, r_lat = _in_projection(h_lat, shift, scale, t, lw["w_main"], lw["w_r"])
        p_ctx, r_ctx = _in_projection(h_ctx, shift[b:], scale[b:], b * t_ctx, lw["w_main"], lw["w_r"])
        p_lat3 = p_lat.reshape(b, t, N_MAIN)
        p_ctx3 = p_ctx.reshape(b, t_ctx, N_MAIN)
        r_lat3 = r_lat.reshape(b, t, LANES)
        r_ctx3 = r_ctx.reshape(b, t_ctx, LANES)
        w_br_l = w_br[l].astype(BF16)
        w_out_l = w_out[l].astype(BF16)

        ya_l = _attention(p_lat3, p_lat3, p_ctx3, lw["q_gain"], lw["k_gain"], cos, sin)
        yb_l, yd_l = _local_branches(p_lat, t, b_conv[l], d_conv_w[l], d_conv_b[l], d_norm_g[l], d_norm_b[l])
        yc_l, yc_c = _gla(p_lat3, r_lat3, p_ctx3, r_ctx3, lw["w2"], lw["b2"], lw["c_gain"], consts, want_ctx)
        h_lat_new = _merge(p_lat, ya_l.reshape(b * t, BR_W), yb_l, yc_l.reshape(b * t, BR_W), yd_l,
                           h_lat, gate, t, w_br_l, w_out_l, ln_g[l], ln_b[l], alpha)
        if want_ctx:
            ya_c = _attention(p_ctx3, None, p_ctx3, lw["q_gain"], lw["k_gain"], None, None)
            yb_c, yd_c = _local_branches(p_ctx, t_ctx, b_conv[l], d_conv_w[l], d_conv_b[l], d_norm_g[l],
                                         d_norm_b[l])
            h_ctx = _merge(p_ctx, ya_c.reshape(b * t_ctx, BR_W), yb_c, yc_c.reshape(b * t_ctx, BR_W), yd_c,
                           h_ctx, gate[b:], b * t_ctx, w_br_l, w_out_l, ln_g[l], ln_b[l], alpha)
        h_lat = h_lat_new
    return h_lat.reshape(b, t, D_MODEL)
```

```python
import functools

import numpy as np
import jax
import jax.numpy as jnp
from jax import lax
from jax.experimental import pallas as pl
from jax.experimental.pallas import tpu as pltpu

F32 = jnp.float32
BF16 = jnp.bfloat16

D_MODEL = 1024
GRID_W = 64
N_BRANCH = 4
BR_W = D_MODEL // 2
HEAD_DIM = 128
A_HEADS = BR_W // HEAD_DIM
A_KV_HEADS = A_HEADS // 2
ROPE_THETA = 10000.0
AXIS_DIM = HEAD_DIM // 2
B_CONV = 3
C_HEADS = 4
C_HEAD_K = BR_W // (2 * C_HEADS)
C_HEAD_V = BR_W // C_HEADS
C_KEY_W = C_HEADS * C_HEAD_K
C_GATE_RANK = 16
C_GATE_TAU = 16.0
C_CHUNK = 64
D_CONV = 31
EPS = 1e-6
IN_WIDTHS = (
    A_HEADS * HEAD_DIM, A_KV_HEADS * HEAD_DIM, A_KV_HEADS * HEAD_DIM, BR_W,
    BR_W, BR_W, BR_W, BR_W,
    C_KEY_W, C_KEY_W, C_HEADS * C_HEAD_V, BR_W, 2 * C_GATE_RANK,
    2 * BR_W, BR_W,
    N_BRANCH * D_MODEL,
)

COL_MG = 0
COL_AQ = 4096
COL_AK = COL_AQ + 512
COL_AV = COL_AK + 256
COL_AZ = COL_AV + 256
COL_BG = COL_AZ + 512
COL_BC = COL_BG + 512
COL_BX = COL_BC + 512
COL_BZ = COL_BX + 512
COL_CQ = COL_BZ + 512
COL_CK = COL_CQ + 256
COL_CV = COL_CK + 256
COL_CZ = COL_CV + 512
COL_DA = COL_CZ + 512
COL_DG = COL_DA + 512
COL_DZ = COL_DG + 512
N_MAIN = COL_DZ + 512
LANES = 128
SUBLANES = 8
PROJ_TN = 512
LN_GROUP = 512
VMEM_LIMIT = 56 * 1024 * 1024
LOG2E = 1.4426950408889634


def _cparams(n_axes):
    return pltpu.CompilerParams(dimension_semantics=("arbitrary",) * n_axes,
                                vmem_limit_bytes=VMEM_LIMIT)


def _sigmoid(x):
    return 0.5 * jnp.tanh(0.5 * x) + 0.5


def _silu(x):
    h = 0.5 * x
    return h + h * jnp.tanh(h)


def _dot(a, b):
    return jnp.dot(a, b, preferred_element_type=F32)


def _dot_nt(a, b):
    return lax.dot_general(a, b, (((1,), (1,)), ((), ())), preferred_element_type=F32)


def _dot_tn(a, b):
    return lax.dot_general(a, b, (((0,), (0,)), ((), ())), preferred_element_type=F32)


def _mod_kernel(c_ref, w_ref, b_ref, o_ref):
    s = _silu(c_ref[...])
    o_ref[0] = _dot(s.astype(BF16), w_ref[0].astype(BF16)) + b_ref[0]


def _modulation(cc, w_mod, b_mod):
    depth = w_mod.shape[0]
    n_rows = cc.shape[0]
    return pl.pallas_call(
        _mod_kernel,
        grid=(depth, 3),
        in_specs=[
            pl.BlockSpec((n_rows, D_MODEL), lambda l, j: (0, 0)),
            pl.BlockSpec((1, D_MODEL, D_MODEL), lambda l, j: (l, 0, j)),
            pl.BlockSpec((1, 1, D_MODEL), lambda l, j: (l, 0, j)),
        ],
        out_specs=pl.BlockSpec((1, n_rows, D_MODEL), lambda l, j: (l, 0, j)),
        out_shape=jax.ShapeDtypeStruct((depth, n_rows, 3 * D_MODEL), F32),
        compiler_params=_cparams(2),
        name="modulation",
    )(cc, w_mod, b_mod.reshape(depth, 1, 3 * D_MODEL))


def _inproj_kernel(x_ref, shift_ref, scale_ref, w_ref, wr_ref, o_ref, or_ref, u_ref, *, tm, sub):
    j = pl.program_id(1)

    @pl.when(j == 0)
    def _():
        one_plus = 1.0 + scale_ref[0]
        shift = shift_ref[0]
        grp = min(LN_GROUP, tm)
        for g0 in range(0, tm, grp):
            for r0 in range(g0, g0 + grp, sub):
                x = x_ref[r0:r0 + sub, :]
                mu = jnp.mean(x, axis=-1, keepdims=True)
                xc = x - mu
                var = jnp.mean(xc * xc, axis=-1, keepdims=True)
                u = xc * lax.rsqrt(var + EPS) * one_plus + shift
                u_ref[r0:r0 + sub, :] = u.astype(BF16)
            u_grp = u_ref[g0:g0 + grp, :]
            or_ref[g0:g0 + grp, :] = _dot(u_grp, wr_ref[...]).astype(BF16)
            o_ref[g0:g0 + grp, :] = _dot(u_grp, w_ref[...]).astype(BF16)

    @pl.when(j != 0)
    def _():
        o_ref[...] = _dot(u_ref[...], w_ref[...]).astype(BF16)


def _in_projection(h, shift, scale, rows_per_mod, w_main, w_r):
    m = h.shape[0]
    tm = min(2048, rows_per_mod)
    assert m % tm == 0 and rows_per_mod % tm == 0
    tiles_per_mod = rows_per_mod // tm
    n_tiles = N_MAIN // PROJ_TN
    kern = functools.partial(_inproj_kernel, tm=tm, sub=min(128, tm))
    return pl.pallas_call(
        kern,
        grid=(m // tm, n_tiles),
        in_specs=[
            pl.BlockSpec((tm, D_MODEL), lambda i, j: (i, 0)),
            pl.BlockSpec((1, 1, D_MODEL), lambda i, j: (i // tiles_per_mod, 0, 0)),
            pl.BlockSpec((1, 1, D_MODEL), lambda i, j: (i // tiles_per_mod, 0, 0)),
            pl.BlockSpec((D_MODEL, PROJ_TN), lambda i, j: (0, j)),
            pl.BlockSpec((D_MODEL, LANES), lambda i, j: (0, 0)),
        ],
        out_specs=[
            pl.BlockSpec((tm, PROJ_TN), lambda i, j: (i, j)),
            pl.BlockSpec((tm, LANES), lambda i, j: (i, 0)),
        ],
        out_shape=[
            jax.ShapeDtypeStruct((m, N_MAIN), BF16),
            jax.ShapeDtypeStruct((m, LANES), BF16),
        ],
        scratch_shapes=[pltpu.VMEM((tm, D_MODEL), BF16)],
        compiler_params=_cparams(2),
        name="in_projection",
    )(h, shift, scale, w_main, w_r)


ATTN_KEY_BLOCKS = (768, 512, 256)


def _rms(x):
    return x * lax.rsqrt(jnp.mean(x * x, axis=-1, keepdims=True) + EPS)


def _rope(x, cos, sin):
    return x * cos + pltpu.roll(x, HEAD_DIM // 2, 1) * sin


def _attn_kernel(*refs, n_lat, n_ctx, tq, kblk, kstep):
    if n_lat:
        (q_ref, z_ref, kl_ref, vl_ref, kc_ref, vc_ref, qg_ref, kg_ref, cos_ref, sin_ref,
         o_ref, kn_ref, vn_ref) = refs
    else:
        q_ref, z_ref, kc_ref, vc_ref, qg_ref, kg_ref, o_ref, kn_ref, vn_ref = refs
    qi = pl.program_id(2)
    kg = kg_ref[...]

    @pl.when(qi == 0)
    def _prep():
        if n_lat:
            def body(t, carry):
                r0 = pl.multiple_of(t * kblk, kblk)
                kn = _rms(kl_ref[0, pl.ds(r0, kblk), :].astype(F32)) * kg
                kn = _rope(kn, cos_ref[pl.ds(r0, kblk), :], sin_ref[pl.ds(r0, kblk), :])
                kn_ref[pl.ds(r0, kblk), :] = kn.astype(BF16)
                vn_ref[pl.ds(r0, kblk), 0:HEAD_DIM] = vl_ref[0, pl.ds(r0, kblk), :]
                return carry

            lax.fori_loop(0, n_lat // kblk, body, 0)
        kn_ref[n_lat:n_lat + n_ctx, :] = (_rms(kc_ref[0].astype(F32)) * kg).astype(BF16)
        vn_ref[n_lat:n_lat + n_ctx, 0:HEAD_DIM] = vc_ref[0]
        vn_ref[:, HEAD_DIM:2 * HEAD_DIM] = jnp.ones((n_lat + n_ctx, HEAD_DIM), BF16)

    qg = qg_ref[...]
    if n_lat:
        q0 = pl.multiple_of(qi * tq, tq)
        cos = cos_ref[pl.ds(q0, tq), :]
        sin = sin_ref[pl.ds(q0, tq), :]
    for g in range(2):
        lanes = slice(g * HEAD_DIM, (g + 1) * HEAD_DIM)
        qn = _rms(q_ref[0, :, lanes].astype(F32)) * qg
        if n_lat:
            qn = _rope(qn, cos, sin)
        qn = (qn * (HEAD_DIM ** -0.5 * LOG2E)).astype(BF16)
        m = o_l = None
        for k0 in range(0, n_lat + n_ctx, kstep):
            s = _dot_nt(qn, kn_ref[k0:k0 + kstep, :])
            m_blk = jnp.max(s, axis=-1, keepdims=True)
            m_new = m_blk if m is None else jnp.maximum(m, m_blk)
            p = jnp.exp2((s - m_new).astype(BF16))
            pv = _dot(p, vn_ref[k0:k0 + kstep, :])
            o_l = pv if o_l is None else o_l * jnp.exp2(m - m_new) + pv
            m = m_new
        o = o_l[:, 0:HEAD_DIM] / o_l[:, HEAD_DIM:2 * HEAD_DIM]
        o_ref[0, :, lanes] = (o * _silu(z_ref[0, :, lanes].astype(F32))).astype(BF16)


def _attention(p_q, p_lat, p_ctx, q_gain, k_gain, cos, sin):
    b, t_q, _ = p_q.shape
    n_ctx = p_ctx.shape[1]
    n_lat = 0 if p_lat is None else p_lat.shape[1]
    tq = min(256, t_q)
    assert t_q % tq == 0
    kblk = min(256, n_lat) if n_lat else 0
    n_keys = n_lat + n_ctx
    kstep = next((c for c in ATTN_KEY_BLOCKS if n_keys % c == 0), n_keys)
    hw = 2 * HEAD_DIM
    q_spec = pl.BlockSpec((1, tq, hw), lambda i, kv, qi: (i, qi, COL_AQ // hw + kv))
    z_spec = pl.BlockSpec((1, tq, hw), lambda i, kv, qi: (i, qi, COL_AZ // hw + kv))

    def kv_spec(n, col):
        return pl.BlockSpec((1, n, HEAD_DIM), lambda i, kv, qi: (i, 0, col // HEAD_DIM + kv))

    gain_spec = pl.BlockSpec((1, HEAD_DIM), lambda i, kv, qi: (0, 0))
    args = [p_q, p_q]
    specs = [q_spec, z_spec]
    if n_lat:
        args += [p_lat, p_lat]
        specs += [kv_spec(n_lat, COL_AK), kv_spec(n_lat, COL_AV)]
    args += [p_ctx, p_ctx, q_gain, k_gain]
    specs += [kv_spec(n_ctx, COL_AK), kv_spec(n_ctx, COL_AV), gain_spec, gain_spec]
    if n_lat:
        tab_spec = pl.BlockSpec((n_lat, HEAD_DIM), lambda i, kv, qi: (0, 0))
        args += [cos, sin]
        specs += [tab_spec, tab_spec]
    kern = functools.partial(_attn_kernel, n_lat=n_lat, n_ctx=n_ctx, tq=tq, kblk=kblk, kstep=kstep)
    return pl.pallas_call(
        kern,
        grid=(b, A_KV_HEADS, t_q // tq),
        in_specs=specs,
        out_specs=pl.BlockSpec((1, tq, hw), lambda i, kv, qi: (i, qi, kv)),
        out_shape=jax.ShapeDtypeStruct((b, t_q, BR_W), BF16),
        scratch_shapes=[pltpu.VMEM((n_lat + n_ctx, HEAD_DIM), BF16),
                        pltpu.VMEM((n_lat + n_ctx, 2 * HEAD_DIM), BF16)],
        compiler_params=_cparams(3),
        name="attention_lat" if n_lat else "attention_ctx",
    )(*args)


D_HALO = 16
B_HALO = 16
CONV_RB = 32
CONV_ACCS = 4
ELEM_RB = 32
COPY_RB = 56


def _local_kernel(bg_ref, bc_ref, bcp_ref, bcn_ref, bx_ref, bxp_ref, bxn_ref, bz_ref,
                  da_ref, dap_ref, dan_ref, dg_ref, dgp_ref, dgn_ref, dz_ref,
                  bw_ref, dw_ref, db_ref, dgain_ref, dbeta_ref,
                  yb_ref, yd_ref, tbuf, gbuf, hbuf, *, tt, tiles_per_seq):
    i = pl.program_id(0)
    pos = i % tiles_per_seq
    keep_prev = (pos != 0).astype(F32)
    keep_next = (pos != tiles_per_seq - 1).astype(F32)

    def ld(ref, r0=0, n=None):
        return ref[r0:r0 + (n or ref.shape[0]), :].astype(F32)

    row_blocks = range(0, tt, ELEM_RB)

    tbuf[0:B_HALO, :] = ld(bcp_ref) * ld(bxp_ref) * keep_prev
    for r0 in row_blocks:
        tbuf[B_HALO + r0:B_HALO + r0 + ELEM_RB, :] = ld(bc_ref, r0, ELEM_RB) * ld(bx_ref, r0, ELEM_RB)
    tbuf[B_HALO + tt:2 * B_HALO + tt, :] = ld(bcn_ref) * ld(bxn_ref) * keep_next
    for r0 in row_blocks:
        lo = B_HALO + r0
        conv = (bw_ref[0:1, :] * tbuf[lo - 1:lo - 1 + ELEM_RB, :] + bw_ref[1:2, :] * tbuf[lo:lo + ELEM_RB, :]
                + bw_ref[2:3, :] * tbuf[lo + 1:lo + 1 + ELEM_RB, :])
        yb_ref[r0:r0 + ELEM_RB, :] = (ld(bg_ref, r0, ELEM_RB) * conv
                                      * _silu(ld(bz_ref, r0, ELEM_RB))).astype(BF16)

    gbuf[0, 0:D_HALO, :] = ld(dap_ref) * _sigmoid(ld(dgp_ref)) * keep_prev
    for r0 in row_blocks:
        gbuf[0, D_HALO + r0:D_HALO + r0 + ELEM_RB, :] = (ld(da_ref, r0, ELEM_RB)
                                                          * _sigmoid(ld(dg_ref, r0, ELEM_RB)))
    gbuf[0, D_HALO + tt:2 * D_HALO + tt, :] = ld(dan_ref) * _sigmoid(ld(dgn_ref)) * keep_next
    span = tt + 2 * D_HALO - SUBLANES
    for r in range(1, SUBLANES):
        for x0 in range(0, span, COPY_RB):
            n = min(COPY_RB, span - x0)
            gbuf[r, x0:x0 + n, :] = gbuf[0, r + x0:r + x0 + n, :]
    base = D_HALO - D_CONV // 2
    for cb in range(BR_W // LANES):
        lanes = slice(cb * LANES, (cb + 1) * LANES)
        for rb in range(tt // CONV_RB):
            accs = [None] * CONV_ACCS
            for k in range(D_CONV):
                off = base + k
                r0 = rb * CONV_RB + off - off % SUBLANES
                rows = gbuf[off % SUBLANES, r0:r0 + CONV_RB, lanes]
                term = rows.reshape(CONV_RB // SUBLANES, SUBLANES, LANES) * dw_ref[k, :, lanes][None]
                a = k % CONV_ACCS
                accs[a] = term if accs[a] is None else accs[a] + term
            while len(accs) > 1:
                accs = [accs[n] + accs[n + 1] for n in range(0, len(accs), 2)]
            hbuf[rb * CONV_RB:(rb + 1) * CONV_RB, lanes] = accs[0].reshape(CONV_RB, LANES)
    for r0 in row_blocks:
        hh = hbuf[r0:r0 + ELEM_RB, :] + db_ref[...]
        mu = jnp.mean(hh, axis=-1, keepdims=True)
        hc = hh - mu
        var = jnp.mean(hc * hc, axis=-1, keepdims=True)
        hn = hc * lax.rsqrt(var + EPS) * dgain_ref[...] + dbeta_ref[...]
        yd_ref[r0:r0 + ELEM_RB, :] = (_silu(hn) * _silu(ld(dz_ref, r0, ELEM_RB))).astype(BF16)


def _local_branches(p2, seq_len, b_w, d_w, d_b, d_g, d_beta):
    m = p2.shape[0]
    tt = min(256, seq_len)
    assert seq_len % tt == 0 and m % seq_len == 0
    tiles_per_seq = seq_len // tt
    n_tiles = m // tt

    def cur(col):
        return pl.BlockSpec((tt, BR_W), lambda i: (i, col // BR_W))

    def prev(col, halo):
        per = tt // halo
        return pl.BlockSpec((halo, BR_W), lambda i: (jnp.maximum(i * per - 1, 0), col // BR_W))

    def nxt(col, halo):
        per = tt // halo
        last = m // halo - 1
        return pl.BlockSpec((halo, BR_W), lambda i: (jnp.minimum((i + 1) * per, last), col // BR_W))

    def small(rows):
        return pl.BlockSpec((rows, BR_W), lambda i: (0, 0))

    specs = [cur(COL_BG),
             cur(COL_BC), prev(COL_BC, B_HALO), nxt(COL_BC, B_HALO),
             cur(COL_BX), prev(COL_BX, B_HALO), nxt(COL_BX, B_HALO),
             cur(COL_BZ),
             cur(COL_DA), prev(COL_DA, D_HALO), nxt(COL_DA, D_HALO),
             cur(COL_DG), prev(COL_DG, D_HALO), nxt(COL_DG, D_HALO),
             cur(COL_DZ),
             small(B_CONV), pl.BlockSpec((D_CONV, SUBLANES, BR_W), lambda i: (0, 0, 0)),
             small(1), small(1), small(1)]
    kern = functools.partial(_local_kernel, tt=tt, tiles_per_seq=tiles_per_seq)
    out_spec = pl.BlockSpec((tt, BR_W), lambda i: (i, 0))
    return pl.pallas_call(
        kern,
        grid=(n_tiles,),
        in_specs=specs,
        out_specs=[out_spec, out_spec],
        out_shape=[jax.ShapeDtypeStruct((m, BR_W), BF16), jax.ShapeDtypeStruct((m, BR_W), BF16)],
        scratch_shapes=[pltpu.VMEM((tt + 2 * B_HALO, BR_W), F32),
                        pltpu.VMEM((SUBLANES, tt + 2 * D_HALO, BR_W), F32),
                        pltpu.VMEM((tt, BR_W), F32)],
        compiler_params=_cparams(1),
        name="local_branches",
    )(*([p2] * 15), b_w, jnp.broadcast_to(d_w[:, None, :], (D_CONV, SUBLANES, BR_W)),
      d_b.reshape(1, BR_W), d_g.reshape(1, BR_W), d_beta.reshape(1, BR_W))


CH = C_CHUNK
PAIR_K = 2 * C_HEAD_K
PAIR_V = 2 * C_HEAD_V
LEVELS = (64, 32, 16, 8, 4, 2)
GLA_UNROLL = 4
ROW_EQ = 0
ROW_EK = CH
ROW_LAST = 2 * CH
ROW_LVL = 2 * CH + 8
N_EROWS = ROW_LVL + len(LEVELS) * CH


def _gla_constants():
    idx = np.arange(CH)
    cols = np.arange(2 * CH) % CH
    emats, masks = [], []
    for reverse in (False, True):
        tri = (idx[None, :] >= idx[:, None]) if reverse else (idx[None, :] <= idx[:, None])
        tri = tri.astype(np.float32)
        edge = tri[0] if reverse else tri[CH - 1]
        blocks = [tri, edge[None, :] - tri, np.tile(edge[None, :], (8, 1))]
        lvl_masks = []
        for grp in LEVELS:
            half = grp // 2
            in_q = (idx % grp < half) if reverse else (idx % grp >= half)
            ref = (idx // grp) * grp + (half if reverse else half - 1)
            d = tri - tri[ref]
            blocks.append(np.where(in_q[:, None], d, -d))
            in_q_col = (cols % grp < half) if reverse else (cols % grp >= half)
            lvl_masks.append((idx[:, None] // grp == cols[None, :] // grp) & in_q[:, None] & ~in_q_col[None, :])
        lvl_masks.append(idx[:, None] == cols[None, :])
        emat = np.concatenate(blocks, axis=0)
        assert emat.shape == (N_EROWS, CH)
        emats.append(np.concatenate([emat] * 3, axis=1))
        masks.append(np.stack(lvl_masks).astype(np.float32))
    return (jnp.asarray(emats[0], BF16), jnp.asarray(emats[1], BF16), jnp.asarray(np.stack(masks), F32))


def _split3(g):
    g1 = g.astype(BF16)
    r1 = g - g1.astype(F32)
    g2 = r1.astype(BF16)
    g3 = (r1 - g2.astype(F32)).astype(BF16)
    return g1, g2, g3


def _log2_sigmoid(x):
    e = jnp.exp2(jnp.abs(x) * -LOG2E)
    return jnp.minimum(x, 0.0) * LOG2E - jnp.log2(1.0 + e)


def _gla_kernel(*refs, t_lat, t_ctx, want_ctx):
    (ql_ref, kl_ref, vl_ref, zl_ref, rl_ref, qc_ref, kc_ref, vc_ref, zc_ref, rc_ref,
     w2_ref, b2_ref, gain_ref, ematf_ref, ematb_ref, masks_ref) = refs[:16]
    if want_ctx:
        yl_ref, yc_ref = refs[16:18]
        scratch = refs[18:]
    else:
        yl_ref = refs[16]
        scratch = refs[17:]
        yc_ref = zc_ref = None
    stf_ref, stb_ref, of_ref, ob_ref, g1_ref, g2_ref, g3_ref = scratch
    unroll = GLA_UNROLL

    def gates(r_ref, n_rows):
        blk = min(256, n_rows)

        def body(t, carry):
            rows = pl.ds(pl.multiple_of(t * blk, blk), blk)
            x = _dot(r_ref[0, rows, :], w2_ref[...]) + b2_ref[...]
            g1, g2, g3 = _split3(_log2_sigmoid(x) * (1.0 / C_GATE_TAU))
            g1_ref[rows, :] = g1
            g2_ref[rows, :] = g2
            g3_ref[rows, :] = g3
            return carry

        lax.fori_loop(0, n_rows // blk, body, 0)

    def run(q_ref, k_ref, v_ref, n_chunks, want_out):
        assert n_chunks % unroll == 0
        lane_k = lax.broadcasted_iota(jnp.int32, (CH, PAIR_K), 1)
        head0 = lane_k < C_HEAD_K
        zero_v = jnp.zeros((CH, C_HEAD_V), BF16)

        def body(i, carry):
            work = []
            for reverse in (False, True):
                lanes = slice(PAIR_K, 2 * PAIR_K) if reverse else slice(0, PAIR_K)
                rows = []
                for u in range(unroll):
                    c = i * unroll + u
                    c = (n_chunks - 1 - c) if reverse else c
                    rows.append(pl.ds(pl.multiple_of(c * CH, CH), CH))
                rhs = jnp.concatenate(
                    [jnp.concatenate([g_ref[r, lanes] for r in rows], axis=1)
                     for g_ref in (g1_ref, g2_ref, g3_ref)], axis=0)
                emat = ematb_ref[...] if reverse else ematf_ref[...]
                e_all = jnp.exp2(_dot(emat, rhs))
                for u in range(unroll):
                    work.append(dict(reverse=reverse, rows=rows[u], u=u,
                                     e=e_all[:, u * PAIR_K:(u + 1) * PAIR_K]))
            work.sort(key=lambda w: w["u"])

            for w in work:
                st_ref = stb_ref if w["reverse"] else stf_ref
                e = w["e"]
                q = q_ref[0, w["rows"], :].astype(F32) * C_HEAD_K ** -0.5
                k = k_ref[0, w["rows"], :].astype(F32)
                v = v_ref[0, w["rows"], :]
                k0 = jnp.where(head0, k, 0.0)
                k1 = jnp.where(head0, 0.0, k)
                st = st_ref[...]
                e_k = e[ROW_EK:ROW_EK + CH]
                k_dec = jnp.concatenate([k0 * e_k, k1 * e_k], axis=0).astype(BF16)
                v_rows = jnp.concatenate([v[:, :C_HEAD_V], v[:, C_HEAD_V:]], axis=0)
                st_ref[...] = e[ROW_LAST:ROW_LAST + 1] * st + _dot_tn(v_rows, k_dec)
                if want_out:
                    qe = q * e[ROW_EQ:ROW_EQ + CH]
                    q_rows = jnp.concatenate([jnp.where(head0, qe, 0.0), jnp.where(head0, 0.0, qe)], axis=0)
                    o_st = _dot_nt(q_rows.astype(BF16), st.astype(BF16))
                    w.update(q=q, k0=k0, k1=k1, v=v, o=jnp.concatenate([o_st[:CH], o_st[CH:]], axis=1))
            if not want_out:
                return carry

            for w in work:
                masks = masks_ref.at[1 if w["reverse"] else 0]
                q, k0, k1, e = w["q"], w["k0"], w["k1"], w["e"]
                k_heads = jnp.concatenate([k0, k1], axis=0)
                scores = _dot_nt(q.astype(BF16), k_heads.astype(BF16)) * masks[len(LEVELS)]
                for lvl in range(len(LEVELS)):
                    e_l = e[ROW_LVL + lvl * CH:ROW_LVL + (lvl + 1) * CH]
                    kk = jnp.concatenate([k0 * e_l, k1 * e_l], axis=0).astype(BF16)
                    scores = scores + _dot_nt((q * e_l).astype(BF16), kk) * masks[lvl]
                w["scores"] = scores.astype(BF16)

            for w in work:
                v = w["v"]
                v_bd = jnp.concatenate([jnp.concatenate([v[:, :C_HEAD_V], zero_v], axis=1),
                                        jnp.concatenate([zero_v, v[:, C_HEAD_V:]], axis=1)], axis=0)
                (ob_ref if w["reverse"] else of_ref)[w["rows"], :] = w["o"] + _dot(w["scores"], v_bd)
            return carry

        lax.fori_loop(0, n_chunks // unroll, body, 0)

    def finish(z_ref, y_ref, n_rows):
        blk = min(256, n_rows)

        def body(t, carry):
            rows = pl.ds(pl.multiple_of(t * blk, blk), blk)
            o = of_ref[rows, :] + ob_ref[rows, :]
            z = z_ref[0, rows, :].astype(F32)
            gain = gain_ref[...]
            for h in range(2):
                lanes = slice(h * C_HEAD_V, (h + 1) * C_HEAD_V)
                y = _rms(o[:, lanes]) * gain[:, lanes]
                y_ref[0, rows, lanes] = (y * _silu(z[:, lanes])).astype(BF16)
            return carry

        lax.fori_loop(0, n_rows // blk, body, 0)

    stf_ref[...] = jnp.zeros_like(stf_ref)
    stb_ref[...] = jnp.zeros_like(stb_ref)
    gates(rc_ref, t_ctx)
    run(qc_ref, kc_ref, vc_ref, t_ctx // CH, want_ctx)
    if want_ctx:
        finish(zc_ref, yc_ref, t_ctx)
    gates(rl_ref, t_lat)
    run(ql_ref, kl_ref, vl_ref, t_lat // CH, True)
    finish(zl_ref, yl_ref, t_lat)


def _gla(p_lat, r_lat, p_ctx, r_ctx, w2, b2, gain, consts, want_ctx):
    b, t_lat, _ = p_lat.shape
    t_ctx = p_ctx.shape[1]
    n_pairs = C_HEADS // 2

    def side(t, with_z):
        z_rows = t if with_z else 16
        return [pl.BlockSpec((1, t, PAIR_K), lambda i, hp: (i, 0, COL_CQ // PAIR_K + hp)),
                pl.BlockSpec((1, t, PAIR_K), lambda i, hp: (i, 0, COL_CK // PAIR_K + hp)),
                pl.BlockSpec((1, t, PAIR_V), lambda i, hp: (i, 0, COL_CV // PAIR_V + hp)),
                pl.BlockSpec((1, z_rows, PAIR_V), lambda i, hp: (i, 0, COL_CZ // PAIR_V + hp)),
                pl.BlockSpec((1, t, LANES), lambda i, hp: (i, 0, 0))]

    def whole(shape):
        nd = len(shape)
        return pl.BlockSpec(shape, lambda i, hp: (0,) * nd)

    emat_f, emat_b, masks = consts
    specs = side(t_lat, True) + side(t_ctx, want_ctx) + [
        pl.BlockSpec((1, LANES, 2 * PAIR_K), lambda i, hp: (hp, 0, 0)),
        pl.BlockSpec((1, 1, 2 * PAIR_K), lambda i, hp: (hp, 0, 0)),
        pl.BlockSpec((1, 1, PAIR_V), lambda i, hp: (hp, 0, 0)),
        whole(emat_f.shape), whole(emat_b.shape), whole(masks.shape)]
    out_specs = [pl.BlockSpec((1, t_lat, PAIR_V), lambda i, hp: (i, 0, hp))]
    out_shape = [jax.ShapeDtypeStruct((b, t_lat, BR_W), BF16)]
    if want_ctx:
        out_specs.append(pl.BlockSpec((1, t_ctx, PAIR_V), lambda i, hp: (i, 0, hp)))
        out_shape.append(jax.ShapeDtypeStruct((b, t_ctx, BR_W), BF16))

    def kern(*refs):
        refs = list(refs)
        for n in range(10, 13):
            refs[n] = refs[n].at[0]
        _gla_kernel(*refs, t_lat=t_lat, t_ctx=t_ctx, want_ctx=want_ctx)

    res = pl.pallas_call(
        kern,
        grid=(b, n_pairs),
        in_specs=specs,
        out_specs=out_specs,
        out_shape=out_shape,
        scratch_shapes=[pltpu.VMEM((C_HEAD_V, PAIR_K), F32), pltpu.VMEM((C_HEAD_V, PAIR_K), F32),
                        pltpu.VMEM((t_lat, PAIR_V), F32), pltpu.VMEM((t_lat, PAIR_V), F32),
                        pltpu.VMEM((t_lat, 2 * PAIR_K), BF16), pltpu.VMEM((t_lat, 2 * PAIR_K), BF16),
                        pltpu.VMEM((t_lat, 2 * PAIR_K), BF16)],
        compiler_params=_cparams(2),
        name="gla",
    )(p_lat, p_lat, p_lat, p_lat, r_lat, p_ctx, p_ctx, p_ctx, p_ctx, r_ctx,
      w2, b2, gain, emat_f, emat_b, masks)
    return (res[0], res[1]) if want_ctx else (res[0], None)


def _merge_kernel(mg_ref, ya_ref, yb_ref, yc_ref, yd_ref, h_ref, gate_ref, wbr_ref, wout_ref,
                  lng_ref, lnb_ref, o_ref, *, alpha):
    acc = None
    for n, y_ref in enumerate((ya_ref, yb_ref, yc_ref, yd_ref)):
        gate = _sigmoid(mg_ref[:, n * D_MODEL:(n + 1) * D_MODEL].astype(F32))
        term = gate * _dot(y_ref[...], wbr_ref[n])
        acc = term if acc is None else acc + term
    y = _dot(acc.astype(BF16), wout_ref[...])
    t = alpha * h_ref[...] + gate_ref[0] * y
    mu = jnp.mean(t, axis=-1, keepdims=True)
    tc = t - mu
    var = jnp.mean(tc * tc, axis=-1, keepdims=True)
    o_ref[...] = tc * lax.rsqrt(var + EPS) * lng_ref[...] + lnb_ref[...]


def _merge(p2, ya, yb, yc, yd, h, gate, rows_per_mod, w_br, w_out, ln_g, ln_b, alpha):
    m = h.shape[0]
    tm = min(512, m)
    assert m % tm == 0 and rows_per_mod % tm == 0
    tiles_per_mod = rows_per_mod // tm
    br_spec = pl.BlockSpec((tm, BR_W), lambda i: (i, 0))
    vec_spec = pl.BlockSpec((1, D_MODEL), lambda i: (0, 0))
    return pl.pallas_call(
        functools.partial(_merge_kernel, alpha=alpha),
        grid=(m // tm,),
        in_specs=[
            pl.BlockSpec((tm, N_BRANCH * D_MODEL), lambda i: (i, COL_MG)),
            br_spec, br_spec, br_spec, br_spec,
            pl.BlockSpec((tm, D_MODEL), lambda i: (i, 0)),
            pl.BlockSpec((1, 1, D_MODEL), lambda i: (i // tiles_per_mod, 0, 0)),
            pl.BlockSpec((N_BRANCH, BR_W, D_MODEL), lambda i: (0, 0, 0)),
            pl.BlockSpec((D_MODEL, D_MODEL), lambda i: (0, 0)),
            vec_spec, vec_spec,
        ],
        out_specs=pl.BlockSpec((tm, D_MODEL), lambda i: (i, 0)),
        out_shape=jax.ShapeDtypeStruct((m, D_MODEL), F32),
        compiler_params=_cparams(1),
        name="merge",
    )(p2, ya, yb, yc, yd, h, gate, w_br, w_out, ln_g.reshape(1, D_MODEL), ln_b.reshape(1, D_MODEL))


def _layer_weights(w_in_l, q_norm_l, k_norm_l, c_w2_l, c_b2_l, c_norm_l):
    cuts = [int(c) for c in np.cumsum(IN_WIDTHS)[:-1]]
    parts = jnp.split(w_in_l, cuts, axis=-1)
    perm = np.concatenate([np.arange(0, HEAD_DIM, 2), np.arange(1, HEAD_DIM, 2)])

    def deinterleave(w, heads):
        return w.reshape(D_MODEL, heads, HEAD_DIM)[:, :, perm].reshape(D_MODEL, heads * HEAD_DIM)

    main = [parts[15], deinterleave(parts[0], A_HEADS), deinterleave(parts[1], A_KV_HEADS), parts[2], parts[3],
            parts[4], parts[5], parts[6], parts[7],
            parts[8], parts[9], parts[10], parts[11],
            parts[13], parts[14]]
    w_main = jnp.concatenate(main, axis=-1).astype(BF16)
    w_r = jnp.pad(parts[12], ((0, 0), (0, LANES - 2 * C_GATE_RANK))).astype(BF16)
    n_pairs = C_HEADS // 2

    def gate_w(i):
        w = jnp.zeros((LANES, C_KEY_W), F32).at[i * C_GATE_RANK:(i + 1) * C_GATE_RANK].set(c_w2_l[i])
        return w.reshape(LANES, n_pairs, PAIR_K).transpose(1, 0, 2)

    w2 = jnp.concatenate([gate_w(0), gate_w(1)], axis=-1).astype(BF16)
    b2 = jnp.concatenate([c_b2_l[0].reshape(n_pairs, 1, PAIR_K), c_b2_l[1].reshape(n_pairs, 1, PAIR_K)], axis=-1)
    return dict(
        w_main=w_main, w_r=w_r,
        q_gain=q_norm_l[perm].reshape(1, HEAD_DIM), k_gain=k_norm_l[perm].reshape(1, HEAD_DIM),
        w2=w2, b2=b2, c_gain=c_norm_l.reshape(n_pairs, 1, PAIR_V))


def _rope_tables(t):
    rows = t // GRID_W
    row = jnp.repeat(jnp.arange(rows), GRID_W).astype(F32)
    col = jnp.tile(jnp.arange(GRID_W), rows).astype(F32)
    inv = ROPE_THETA ** (-jnp.arange(0, AXIS_DIM, 2, dtype=F32) / AXIS_DIM)
    ang = jnp.concatenate([row[:, None] * inv, col[:, None] * inv], -1)
    cos, sin = jnp.cos(ang), jnp.sin(ang)
    return jnp.concatenate([cos, cos], -1), jnp.concatenate([-sin, sin], -1)


def kernel(x, c, ctx, c_ctx, w_mod, b_mod, w_in, q_norm, k_norm, b_conv, c_gate_w2, c_gate_b, c_norm,
           d_conv_w, d_conv_b, d_norm_g, d_norm_b, w_br, w_out, ln_g, ln_b):
    b, t, d = x.shape
    t_ctx = ctx.shape[1]
    depth = w_mod.shape[0]
    assert d == D_MODEL and b < 16
    alpha = (2 * depth) ** 0.25

    cc = jnp.zeros((16, D_MODEL), F32).at[:b].set(c).at[b].set(c_ctx)
    mod = _modulation(cc, w_mod, b_mod)
    cos, sin = _rope_tables(t)
    consts = _gla_constants()

    h_lat = x.reshape(b * t, D_MODEL)
    h_ctx = ctx.reshape(b * t_ctx, D_MODEL)
    for l in range(depth):
        want_ctx = l < depth - 1
        lw = _layer_weights(w_in[l], q_norm[l], k_norm[l], c_gate_w2[l], c_gate_b[l], c_norm[l])
        shift, scale, gate = [mod[l, :, n * D_MODEL:(n + 1) * D_MODEL].reshape(16, 1, D_MODEL) for n in range(3)]
        p_lat, r_lat = _in_projection(h_lat, shift, scale, t, lw["w_main"], lw["w_r"])
        p_ctx, r_ctx = _in_projection(h_ctx, shift[b:], scale[b:], b * t_ctx, lw["w_main"], lw["w_r"])
        p_lat3 = p_lat.reshape(b, t, N_MAIN)
        p_ctx3 = p_ctx.reshape(b, t_ctx, N_MAIN)
        r_lat3 = r_lat.reshape(b, t, LANES)
        r_ctx3 = r_ctx.reshape(b, t_ctx, LANES)
        w_br_l = w_br[l].astype(BF16)
        w_out_l = w_out[l].astype(BF16)

        ya_l = _attention(p_lat3, p_lat3, p_ctx3, lw["q_gain"], lw["k_gain"], cos, sin)
        yb_l, yd_l = _local_branches(p_lat, t, b_conv[l], d_conv_w[l], d_conv_b[l], d_norm_g[l], d_norm_b[l])
        yc_l, yc_c = _gla(p_lat3, r_lat3, p_ctx3, r_ctx3, lw["w2"], lw["b2"], lw["c_gain"], consts, want_ctx)
        h_lat_new = _merge(p_lat, ya_l.reshape(b * t, BR_W), yb_l, yc_l.reshape(b * t, BR_W), yd_l,
                           h_lat, gate, t, w_br_l, w_out_l, ln_g[l], ln_b[l], alpha)
        if want_ctx:
            ya_c = _attention(p_ctx3, None, p_ctx3, lw["q_gain"], lw["k_gain"], None, None)
            yb_c, yd_c = _local_branches(p_ctx, t_ctx, b_conv[l], d_conv_w[l], d_conv_b[l], d_norm_g[l],
                                         d_norm_b[l])
            h_ctx = _merge(p_ctx, ya_c.reshape(b * t_ctx, BR_W), yb_c, yc_c.reshape(b * t_ctx, BR_W), yd_c,
                           h_ctx, gate[b:], b * t_ctx, w_br_l, w_out_l, ln_g[l], ln_b[l], alpha)
        h_lat = h_lat_new
    return h_lat.reshape(b, t, D_MODEL)
```

```python
import functools

import numpy as np
import jax
import jax.numpy as jnp
from jax import lax
from jax.experimental import pallas as pl
from jax.experimental.pallas import tpu as pltpu

F32 = jnp.float32
BF16 = jnp.bfloat16

D_MODEL = 1024
GRID_W = 64
N_BRANCH = 4
BR_W = D_MODEL // 2
HEAD_DIM = 128
A_HEADS = BR_W // HEAD_DIM
A_KV_HEADS = A_HEADS // 2
ROPE_THETA = 10000.0
AXIS_DIM = HEAD_DIM // 2
B_CONV = 3
C_HEADS = 4
C_HEAD_K = BR_W // (2 * C_HEADS)
C_HEAD_V = BR_W // C_HEADS
C_KEY_W = C_HEADS * C_HEAD_K
C_GATE_RANK = 16
C_GATE_TAU = 16.0
C_CHUNK = 64
D_CONV = 31
EPS = 1e-6
IN_WIDTHS = (
    A_HEADS * HEAD_DIM, A_KV_HEADS * HEAD_DIM, A_KV_HEADS * HEAD_DIM, BR_W,
    BR_W, BR_W, BR_W, BR_W,
    C_KEY_W, C_KEY_W, C_HEADS * C_HEAD_V, BR_W, 2 * C_GATE_RANK,
    2 * BR_W, BR_W,
    N_BRANCH * D_MODEL,
)

COL_MG = 0
COL_AQ = 4096
COL_AK = COL_AQ + 512
COL_AV = COL_AK + 256
COL_AZ = COL_AV + 256
COL_BG = COL_AZ + 512
COL_BC = COL_BG + 512
COL_BX = COL_BC + 512
COL_BZ = COL_BX + 512
COL_CQ = COL_BZ + 512
COL_CK = COL_CQ + 256
COL_CV = COL_CK + 256
COL_CZ = COL_CV + 512
COL_DA = COL_CZ + 512
COL_DG = COL_DA + 512
COL_DZ = COL_DG + 512
N_MAIN = COL_DZ + 512
FULL_COLS = dict(ak=COL_AK, av=COL_AV, cq=COL_CQ, ck=COL_CK, cv=COL_CV, cz=COL_CZ)
CTX_KV_TILES = (COL_AK, COL_CQ, COL_CV)
CTX_KV_COLS = dict(ak=0, av=256, cq=512, ck=768, cv=1024, cz=1024)
LANES = 128
SUBLANES = 8
PROJ_TN = 512
LN_GROUP = 512
VMEM_LIMIT = 56 * 1024 * 1024
LOG2E = 1.4426950408889634


def _cparams(n_axes):
    return pltpu.CompilerParams(dimension_semantics=("arbitrary",) * n_axes,
                                vmem_limit_bytes=VMEM_LIMIT)


def _sigmoid(x):
    return 0.5 * jnp.tanh(0.5 * x) + 0.5


def _silu(x):
    h = 0.5 * x
    return h + h * jnp.tanh(h)


def _dot(a, b):
    return jnp.dot(a, b, preferred_element_type=F32)


def _dot_nt(a, b):
    return lax.dot_general(a, b, (((1,), (1,)), ((), ())), preferred_element_type=F32)


def _dot_tn(a, b):
    return lax.dot_general(a, b, (((0,), (0,)), ((), ())), preferred_element_type=F32)


def _mod_kernel(c_ref, w_ref, b_ref, o_ref):
    s = _silu(c_ref[...])
    o_ref[0] = _dot(s.astype(BF16), w_ref[0].astype(BF16)) + b_ref[0]


def _modulation(cc, w_mod, b_mod):
    depth = w_mod.shape[0]
    n_rows = cc.shape[0]
    return pl.pallas_call(
        _mod_kernel,
        grid=(depth, 3),
        in_specs=[
            pl.BlockSpec((n_rows, D_MODEL), lambda l, j: (0, 0)),
            pl.BlockSpec((1, D_MODEL, D_MODEL), lambda l, j: (l, 0, j)),
            pl.BlockSpec((1, 1, D_MODEL), lambda l, j: (l, 0, j)),
        ],
        out_specs=pl.BlockSpec((1, n_rows, D_MODEL), lambda l, j: (l, 0, j)),
        out_shape=jax.ShapeDtypeStruct((depth, n_rows, 3 * D_MODEL), F32),
        compiler_params=_cparams(2),
        name="modulation",
    )(cc, w_mod, b_mod.reshape(depth, 1, 3 * D_MODEL))


def _inproj_kernel(x_ref, shift_ref, scale_ref, w_ref, wr_ref, o_ref, or_ref, u_ref, *, tm, sub):
    j = pl.program_id(1)

    @pl.when(j == 0)
    def _():
        one_plus = 1.0 + scale_ref[0]
        shift = shift_ref[0]
        grp = min(LN_GROUP, tm)
        for g0 in range(0, tm, grp):
            for r0 in range(g0, g0 + grp, sub):
                x = x_ref[r0:r0 + sub, :]
                mu = jnp.mean(x, axis=-1, keepdims=True)
                xc = x - mu
                var = jnp.mean(xc * xc, axis=-1, keepdims=True)
                u = xc * lax.rsqrt(var + EPS) * one_plus + shift
                u_ref[r0:r0 + sub, :] = u.astype(BF16)
            u_grp = u_ref[g0:g0 + grp, :]
            or_ref[g0:g0 + grp, :] = _dot(u_grp, wr_ref[...]).astype(BF16)
            o_ref[g0:g0 + grp, :] = _dot(u_grp, w_ref[...]).astype(BF16)

    @pl.when(j != 0)
    def _():
        o_ref[...] = _dot(u_ref[...], w_ref[...]).astype(BF16)


def _in_projection(h, shift, scale, rows_per_mod, w_main, w_r):
    m = h.shape[0]
    tm = min(2048, rows_per_mod)
    assert m % tm == 0 and rows_per_mod % tm == 0
    tiles_per_mod = rows_per_mod // tm
    n_cols = w_main.shape[1]
    assert n_cols % PROJ_TN == 0
    n_tiles = n_cols // PROJ_TN
    kern = functools.partial(_inproj_kernel, tm=tm, sub=min(128, tm))
    return pl.pallas_call(
        kern,
        grid=(m // tm, n_tiles),
        in_specs=[
            pl.BlockSpec((tm, D_MODEL), lambda i, j: (i, 0)),
            pl.BlockSpec((1, 1, D_MODEL), lambda i, j: (i // tiles_per_mod, 0, 0)),
            pl.BlockSpec((1, 1, D_MODEL), lambda i, j: (i // tiles_per_mod, 0, 0)),
            pl.BlockSpec((D_MODEL, PROJ_TN), lambda i, j: (0, j)),
            pl.BlockSpec((D_MODEL, LANES), lambda i, j: (0, 0)),
        ],
        out_specs=[
            pl.BlockSpec((tm, PROJ_TN), lambda i, j: (i, j)),
            pl.BlockSpec((tm, LANES), lambda i, j: (i, 0)),
        ],
        out_shape=[
            jax.ShapeDtypeStruct((m, n_cols), BF16),
            jax.ShapeDtypeStruct((m, LANES), BF16),
        ],
        scratch_shapes=[pltpu.VMEM((tm, D_MODEL), BF16)],
        compiler_params=_cparams(2),
        name="in_projection",
    )(h, shift, scale, w_main, w_r)


ATTN_KEY_BLOCKS = (768, 512, 256)
ONES_ROWS = 16
MAX_FOLD = 64
QK_AHEAD = 2


def _rms(x):
    return x * lax.rsqrt(jnp.mean(x * x, axis=-1, keepdims=True) + EPS)


def _rope(x, cos, sin):
    return x * cos + pltpu.roll(x, HEAD_DIM // 2, 1) * sin


def _attn_kernel(*refs, n_lat, n_ctx, tq, kblk, kstep):
    if n_lat:
        (q_ref, z_ref, kl_ref, vl_ref, kc_ref, vc_ref, qg_ref, kg_ref, cos_ref, sin_ref,
         o_ref, kn_ref, vt_ref) = refs
    else:
        q_ref, z_ref, kc_ref, vc_ref, qg_ref, kg_ref, o_ref, kn_ref, vt_ref = refs
    qi = pl.program_id(2)
    kg = kg_ref[...]

    @pl.when(qi == 0)
    def _prep():
        if n_lat:
            def body(t, carry):
                r0 = pl.multiple_of(t * kblk, kblk)
                kn = _rms(kl_ref[0, pl.ds(r0, kblk), :].astype(F32)) * kg
                kn = _rope(kn, cos_ref[pl.ds(r0, kblk), :], sin_ref[pl.ds(r0, kblk), :])
                kn_ref[pl.ds(r0, kblk), :] = kn.astype(BF16)
                vt_ref[0:HEAD_DIM, pl.ds(r0, kblk)] = vl_ref[0, pl.ds(r0, kblk), :].astype(F32).T.astype(BF16)
                return carry

            lax.fori_loop(0, n_lat // kblk, body, 0)
        kn_ref[n_lat:n_lat + n_ctx, :] = (_rms(kc_ref[0].astype(F32)) * kg).astype(BF16)
        vt_ref[0:HEAD_DIM, n_lat:n_lat + n_ctx] = vc_ref[0].astype(F32).T.astype(BF16)
        vt_ref[HEAD_DIM:HEAD_DIM + ONES_ROWS, :] = jnp.ones((ONES_ROWS, n_lat + n_ctx), BF16)

    qg = qg_ref[...]
    if n_lat:
        q0 = pl.multiple_of(qi * tq, tq)
        cos = cos_ref[pl.ds(q0, tq), :]
        sin = sin_ref[pl.ds(q0, tq), :]
    qts = []
    for g in range(2):
        qn = _rms(q_ref[0, :, g * HEAD_DIM:(g + 1) * HEAD_DIM].astype(F32)) * qg
        if n_lat:
            qn = _rope(qn, cos, sin)
        qts.append((qn * (HEAD_DIM ** -0.5 * LOG2E)).T.astype(BF16))

    items = [(g, k0) for k0 in range(0, n_lat + n_ctx, kstep) for g in range(2)]

    def scores(item):
        g, k0 = item
        return _dot(kn_ref[k0:k0 + kstep, :], qts[g])

    m = [None, None]
    acc = [None, None]
    ahead = [scores(item) for item in items[:QK_AHEAD]]
    for n, (g, k0) in enumerate(items):
        s = ahead.pop(0)
        if n + QK_AHEAD < len(items):
            ahead.append(scores(items[n + QK_AHEAD]))
        m_blk = jnp.max(jnp.max(s.reshape(kstep // MAX_FOLD, MAX_FOLD, tq), axis=0), axis=0, keepdims=True)
        m_new = m_blk if m[g] is None else jnp.maximum(m[g], m_blk)
        p = jnp.exp2((s - m_new).astype(BF16))
        pv = _dot(vt_ref[:, k0:k0 + kstep], p)
        acc[g] = pv if acc[g] is None else acc[g] * jnp.exp2(m[g] - m_new) + pv
        m[g] = m_new
    for g in range(2):
        lanes = slice(g * HEAD_DIM, (g + 1) * HEAD_DIM)
        o = (acc[g][0:HEAD_DIM] / acc[g][HEAD_DIM:HEAD_DIM + 1]).T
        o_ref[0, :, lanes] = (o * _silu(z_ref[0, :, lanes].astype(F32))).astype(BF16)


def _attention(p_q, p_lat, p_ctx, q_gain, k_gain, cos, sin, ctx_cols=None):
    b, t_q, _ = p_q.shape
    n_ctx = p_ctx.shape[1]
    n_lat = 0 if p_lat is None else p_lat.shape[1]
    tq = min(256, t_q)
    assert t_q % tq == 0
    kblk = min(256, n_lat) if n_lat else 0
    n_keys = n_lat + n_ctx
    kstep = next((c for c in ATTN_KEY_BLOCKS if n_keys % c == 0), n_keys)
    hw = 2 * HEAD_DIM
    q_spec = pl.BlockSpec((1, tq, hw), lambda i, kv, qi: (i, qi, COL_AQ // hw + kv))
    z_spec = pl.BlockSpec((1, tq, hw), lambda i, kv, qi: (i, qi, COL_AZ // hw + kv))

    def kv_spec(n, col):
        return pl.BlockSpec((1, n, HEAD_DIM), lambda i, kv, qi: (i, 0, col // HEAD_DIM + kv))

    gain_spec = pl.BlockSpec((1, HEAD_DIM), lambda i, kv, qi: (0, 0))
    args = [p_q, p_q]
    specs = [q_spec, z_spec]
    if n_lat:
        args += [p_lat, p_lat]
        specs += [kv_spec(n_lat, COL_AK), kv_spec(n_lat, COL_AV)]
    ctx_cols = ctx_cols or FULL_COLS
    args += [p_ctx, p_ctx, q_gain, k_gain]
    specs += [kv_spec(n_ctx, ctx_cols["ak"]), kv_spec(n_ctx, ctx_cols["av"]), gain_spec, gain_spec]
    if n_lat:
        tab_spec = pl.BlockSpec((n_lat, HEAD_DIM), lambda i, kv, qi: (0, 0))
        args += [cos, sin]
        specs += [tab_spec, tab_spec]
    kern = functools.partial(_attn_kernel, n_lat=n_lat, n_ctx=n_ctx, tq=tq, kblk=kblk, kstep=kstep)
    return pl.pallas_call(
        kern,
        grid=(b, A_KV_HEADS, t_q // tq),
        in_specs=specs,
        out_specs=pl.BlockSpec((1, tq, hw), lambda i, kv, qi: (i, qi, kv)),
        out_shape=jax.ShapeDtypeStruct((b, t_q, BR_W), BF16),
        scratch_shapes=[pltpu.VMEM((n_lat + n_ctx, HEAD_DIM), BF16),
                        pltpu.VMEM((HEAD_DIM + ONES_ROWS, n_lat + n_ctx), BF16)],
        compiler_params=_cparams(3),
        name="attention_lat" if n_lat else "attention_ctx",
    )(*args)


D_HALO = 16
B_HALO = 16
CONV_RB = 32
CONV_ACCS = 4
ELEM_RB = 32
COPY_RB = 56


def _local_kernel(bg_ref, bc_ref, bcp_ref, bcn_ref, bx_ref, bxp_ref, bxn_ref, bz_ref,
                  da_ref, dap_ref, dan_ref, dg_ref, dgp_ref, dgn_ref, dz_ref,
                  bw_ref, dw_ref, db_ref, dgain_ref, dbeta_ref,
                  yb_ref, yd_ref, tbuf, gbuf, hbuf, *, tt, tiles_per_seq):
    i = pl.program_id(0)
    pos = i % tiles_per_seq
    keep_prev = (pos != 0).astype(F32)
    keep_next = (pos != tiles_per_seq - 1).astype(F32)

    def ld(ref, r0=0, n=None):
        return ref[r0:r0 + (n or ref.shape[0]), :].astype(F32)

    row_blocks = range(0, tt, ELEM_RB)

    tbuf[0:B_HALO, :] = ld(bcp_ref) * ld(bxp_ref) * keep_prev
    for r0 in row_blocks:
        tbuf[B_HALO + r0:B_HALO + r0 + ELEM_RB, :] = ld(bc_ref, r0, ELEM_RB) * ld(bx_ref, r0, ELEM_RB)
    tbuf[B_HALO + tt:2 * B_HALO + tt, :] = ld(bcn_ref) * ld(bxn_ref) * keep_next
    for r0 in row_blocks:
        lo = B_HALO + r0
        conv = (bw_ref[0:1, :] * tbuf[lo - 1:lo - 1 + ELEM_RB, :] + bw_ref[1:2, :] * tbuf[lo:lo + ELEM_RB, :]
                + bw_ref[2:3, :] * tbuf[lo + 1:lo + 1 + ELEM_RB, :])
        yb_ref[r0:r0 + ELEM_RB, :] = (ld(bg_ref, r0, ELEM_RB) * conv
                                      * _silu(ld(bz_ref, r0, ELEM_RB))).astype(BF16)

    gbuf[0, 0:D_HALO, :] = ld(dap_ref) * _sigmoid(ld(dgp_ref)) * keep_prev
    for r0 in row_blocks:
        gbuf[0, D_HALO + r0:D_HALO + r0 + ELEM_RB, :] = (ld(da_ref, r0, ELEM_RB)
                                                          * _sigmoid(ld(dg_ref, r0, ELEM_RB)))
    gbuf[0, D_HALO + tt:2 * D_HALO + tt, :] = ld(dan_ref) * _sigmoid(ld(dgn_ref)) * keep_next
    span = tt + 2 * D_HALO - SUBLANES
    for r in range(1, SUBLANES):
        for x0 in range(0, span, COPY_RB):
            n = min(COPY_RB, span - x0)
            gbuf[r, x0:x0 + n, :] = gbuf[0, r + x0:r + x0 + n, :]
    base = D_HALO - D_CONV // 2
    for cb in range(BR_W // LANES):
        lanes = slice(cb * LANES, (cb + 1) * LANES)
        for rb in range(tt // CONV_RB):
            accs = [None] * CONV_ACCS
            for k in range(D_CONV):
                off = base + k
                r0 = rb * CONV_RB + off - off % SUBLANES
                rows = gbuf[off % SUBLANES, r0:r0 + CONV_RB, lanes]
                term = rows.reshape(CONV_RB // SUBLANES, SUBLANES, LANES) * dw_ref[k, :, lanes][None]
                a = k % CONV_ACCS
                accs[a] = term if accs[a] is None else accs[a] + term
            while len(accs) > 1:
                accs = [accs[n] + accs[n + 1] for n in range(0, len(accs), 2)]
            hbuf[rb * CONV_RB:(rb + 1) * CONV_RB, lanes] = accs[0].reshape(CONV_RB, LANES)
    for r0 in row_blocks:
        hh = hbuf[r0:r0 + ELEM_RB, :] + db_ref[...]
        mu = jnp.mean(hh, axis=-1, keepdims=True)
        hc = hh - mu
        var = jnp.mean(hc * hc, axis=-1, keepdims=True)
        hn = hc * lax.rsqrt(var + EPS) * dgain_ref[...] + dbeta_ref[...]
        yd_ref[r0:r0 + ELEM_RB, :] = (_silu(hn) * _silu(ld(dz_ref, r0, ELEM_RB))).astype(BF16)


def _local_branches(p2, seq_len, b_w, d_w, d_b, d_g, d_beta):
    m = p2.shape[0]
    tt = min(256, seq_len)
    assert seq_len % tt == 0 and m % seq_len == 0
    tiles_per_seq = seq_len // tt
    n_tiles = m // tt

    def cur(col):
        return pl.BlockSpec((tt, BR_W), lambda i: (i, col // BR_W))

    def prev(col, halo):
        per = tt // halo
        return pl.BlockSpec((halo, BR_W), lambda i: (jnp.maximum(i * per - 1, 0), col // BR_W))

    def nxt(col, halo):
        per = tt // halo
        last = m // halo - 1
        return pl.BlockSpec((halo, BR_W), lambda i: (jnp.minimum((i + 1) * per, last), col // BR_W))

    def small(rows):
        return pl.BlockSpec((rows, BR_W), lambda i: (0, 0))

    specs = [cur(COL_BG),
             cur(COL_BC), prev(COL_BC, B_HALO), nxt(COL_BC, B_HALO),
             cur(COL_BX), prev(COL_BX, B_HALO), nxt(COL_BX, B_HALO),
             cur(COL_BZ),
             cur(COL_DA), prev(COL_DA, D_HALO), nxt(COL_DA, D_HALO),
             cur(COL_DG), prev(COL_DG, D_HALO), nxt(COL_DG, D_HALO),
             cur(COL_DZ),
             small(B_CONV), pl.BlockSpec((D_CONV, SUBLANES, BR_W), lambda i: (0, 0, 0)),
             small(1), small(1), small(1)]
    kern = functools.partial(_local_kernel, tt=tt, tiles_per_seq=tiles_per_seq)
    out_spec = pl.BlockSpec((tt, BR_W), lambda i: (i, 0))
    return pl.pallas_call(
        kern,
        grid=(n_tiles,),
        in_specs=specs,
        out_specs=[out_spec, out_spec],
        out_shape=[jax.ShapeDtypeStruct((m, BR_W), BF16), jax.ShapeDtypeStruct((m, BR_W), BF16)],
        scratch_shapes=[pltpu.VMEM((tt + 2 * B_HALO, BR_W), F32),
                        pltpu.VMEM((SUBLANES, tt + 2 * D_HALO, BR_W), F32),
                        pltpu.VMEM((tt, BR_W), F32)],
        compiler_params=_cparams(1),
        name="local_branches",
    )(*([p2] * 15), b_w, jnp.broadcast_to(d_w[:, None, :], (D_CONV, SUBLANES, BR_W)),
      d_b.reshape(1, BR_W), d_g.reshape(1, BR_W), d_beta.reshape(1, BR_W))


CH = C_CHUNK
PAIR_K = 2 * C_HEAD_K
PAIR_V = 2 * C_HEAD_V
LEVELS = (64, 32, 16, 8, 4, 2)
GLA_UNROLL = 4
ROW_EQ = 0
ROW_EK = CH
ROW_LAST = 2 * CH
ROW_LVL = 2 * CH + 8
N_EROWS = ROW_LVL + len(LEVELS) * CH


def _gla_constants():
    idx = np.arange(CH)
    cols = np.arange(2 * CH) % CH
    emats, masks = [], []
    for reverse in (False, True):
        tri = (idx[None, :] >= idx[:, None]) if reverse else (idx[None, :] <= idx[:, None])
        tri = tri.astype(np.float32)
        edge = tri[0] if reverse else tri[CH - 1]
        blocks = [tri, edge[None, :] - tri, np.tile(edge[None, :], (8, 1))]
        lvl_masks = []
        for grp in LEVELS:
            half = grp // 2
            in_q = (idx % grp < half) if reverse else (idx % grp >= half)
            ref = (idx // grp) * grp + (half if reverse else half - 1)
            d = tri - tri[ref]
            blocks.append(np.where(in_q[:, None], d, -d))
            in_q_col = (cols % grp < half) if reverse else (cols % grp >= half)
            lvl_masks.append((idx[:, None] // grp == cols[None, :] // grp) & in_q[:, None] & ~in_q_col[None, :])
        lvl_masks.append(idx[:, None] == cols[None, :])
        emat = np.concatenate(blocks, axis=0)
        assert emat.shape == (N_EROWS, CH)
        emats.append(np.concatenate([emat] * 3, axis=1))
        masks.append(np.stack(lvl_masks).astype(np.float32))
    return (jnp.asarray(emats[0], BF16), jnp.asarray(emats[1], BF16), jnp.asarray(np.stack(masks), F32))


def _split3(g):
    g1 = g.astype(BF16)
    r1 = g - g1.astype(F32)
    g2 = r1.astype(BF16)
    g3 = (r1 - g2.astype(F32)).astype(BF16)
    return g1, g2, g3


def _log2_sigmoid(x):
    e = jnp.exp2(jnp.abs(x) * -LOG2E)
    return jnp.minimum(x, 0.0) * LOG2E - jnp.log2(1.0 + e)


def _gla_kernel(*refs, t_lat, t_ctx, want_ctx):
    (ql_ref, kl_ref, vl_ref, zl_ref, rl_ref, qc_ref, kc_ref, vc_ref, zc_ref, rc_ref,
     w2_ref, b2_ref, gain_ref, ematf_ref, ematb_ref, masks_ref) = refs[:16]
    if want_ctx:
        yl_ref, yc_ref = refs[16:18]
        scratch = refs[18:]
    else:
        yl_ref = refs[16]
        scratch = refs[17:]
        yc_ref = zc_ref = None
    stf_ref, stb_ref, of_ref, ob_ref, g1_ref, g2_ref, g3_ref = scratch
    unroll = GLA_UNROLL

    def gates(r_ref, n_rows):
        blk = min(256, n_rows)

        def body(t, carry):
            rows = pl.ds(pl.multiple_of(t * blk, blk), blk)
            x = _dot(r_ref[0, rows, :], w2_ref[...]) + b2_ref[...]
            g1, g2, g3 = _split3(_log2_sigmoid(x) * (1.0 / C_GATE_TAU))
            g1_ref[rows, :] = g1
            g2_ref[rows, :] = g2
            g3_ref[rows, :] = g3
            return carry

        lax.fori_loop(0, n_rows // blk, body, 0)

    def run(q_ref, k_ref, v_ref, n_chunks, want_out):
        assert n_chunks % unroll == 0
        lane_k = lax.broadcasted_iota(jnp.int32, (CH, PAIR_K), 1)
        head0 = lane_k < C_HEAD_K
        zero_v = jnp.zeros((CH, C_HEAD_V), BF16)

        def body(i, carry):
            work = []
            for reverse in (False, True):
                lanes = slice(PAIR_K, 2 * PAIR_K) if reverse else slice(0, PAIR_K)
                rows = []
                for u in range(unroll):
                    c = i * unroll + u
                    c = (n_chunks - 1 - c) if reverse else c
                    rows.append(pl.ds(pl.multiple_of(c * CH, CH), CH))
                rhs = jnp.concatenate(
                    [jnp.concatenate([g_ref[r, lanes] for r in rows], axis=1)
                     for g_ref in (g1_ref, g2_ref, g3_ref)], axis=0)
                emat = ematb_ref[...] if reverse else ematf_ref[...]
                e_all = jnp.exp2(_dot(emat, rhs))
                for u in range(unroll):
                    work.append(dict(reverse=reverse, rows=rows[u], u=u,
                                     e=e_all[:, u * PAIR_K:(u + 1) * PAIR_K]))
            work.sort(key=lambda w: w["u"])

            for w in work:
                st_ref = stb_ref if w["reverse"] else stf_ref
                e = w["e"]
                q = q_ref[0, w["rows"], :].astype(F32) * C_HEAD_K ** -0.5
                k = k_ref[0, w["rows"], :].astype(F32)
                v = v_ref[0, w["rows"], :]
                k0 = jnp.where(head0, k, 0.0)
                k1 = jnp.where(head0, 0.0, k)
                st = st_ref[...]
                e_k = e[ROW_EK:ROW_EK + CH]
                k_dec = jnp.concatenate([k0 * e_k, k1 * e_k], axis=0).astype(BF16)
                v_rows = jnp.concatenate([v[:, :C_HEAD_V], v[:, C_HEAD_V:]], axis=0)
                st_ref[...] = e[ROW_LAST:ROW_LAST + 1] * st + _dot_tn(v_rows, k_dec)
                if want_out:
                    qe = q * e[ROW_EQ:ROW_EQ + CH]
                    q_rows = jnp.concatenate([jnp.where(head0, qe, 0.0), jnp.where(head0, 0.0, qe)], axis=0)
                    o_st = _dot(q_rows.astype(BF16), st.T.astype(BF16))
                    w.update(q=q, k0=k0, k1=k1, v=v, o=jnp.concatenate([o_st[:CH], o_st[CH:]], axis=1))
            if not want_out:
                return carry

            for w in work:
                masks = masks_ref.at[1 if w["reverse"] else 0]
                q, k0, k1, e = w["q"], w["k0"], w["k1"], w["e"]
                k_heads = jnp.concatenate([k0, k1], axis=0)
                scores = _dot(q.astype(BF16), k_heads.T.astype(BF16)) * masks[len(LEVELS)]
                for lvl in range(len(LEVELS)):
                    e_l = e[ROW_LVL + lvl * CH:ROW_LVL + (lvl + 1) * CH]
                    kk_t = jnp.concatenate([k0 * e_l, k1 * e_l], axis=0).T.astype(BF16)
                    scores = scores + _dot((q * e_l).astype(BF16), kk_t) * masks[lvl]
                w["scores"] = scores.astype(BF16)

            for w in work:
                v = w["v"]
                v_bd = jnp.concatenate([jnp.concatenate([v[:, :C_HEAD_V], zero_v], axis=1),
                                        jnp.concatenate([zero_v, v[:, C_HEAD_V:]], axis=1)], axis=0)
                (ob_ref if w["reverse"] else of_ref)[w["rows"], :] = w["o"] + _dot(w["scores"], v_bd)
            return carry

        lax.fori_loop(0, n_chunks // unroll, body, 0)

    def finish(z_ref, y_ref, n_rows):
        blk = min(256, n_rows)

        def body(t, carry):
            rows = pl.ds(pl.multiple_of(t * blk, blk), blk)
            o = of_ref[rows, :] + ob_ref[rows, :]
            z = z_ref[0, rows, :].astype(F32)
            gain = gain_ref[...]
            for h in range(2):
                lanes = slice(h * C_HEAD_V, (h + 1) * C_HEAD_V)
                y = _rms(o[:, lanes]) * gain[:, lanes]
                y_ref[0, rows, lanes] = (y * _silu(z[:, lanes])).astype(BF16)
            return carry

        lax.fori_loop(0, n_rows // blk, body, 0)

    stf_ref[...] = jnp.zeros_like(stf_ref)
    stb_ref[...] = jnp.zeros_like(stb_ref)
    gates(rc_ref, t_ctx)
    run(qc_ref, kc_ref, vc_ref, t_ctx // CH, want_ctx)
    if want_ctx:
        finish(zc_ref, yc_ref, t_ctx)
    gates(rl_ref, t_lat)
    run(ql_ref, kl_ref, vl_ref, t_lat // CH, True)
    finish(zl_ref, yl_ref, t_lat)


def _gla(p_lat, r_lat, p_ctx, r_ctx, w2, b2, gain, consts, want_ctx, ctx_cols=None):
    b, t_lat, _ = p_lat.shape
    t_ctx = p_ctx.shape[1]
    n_pairs = C_HEADS // 2

    def side(t, with_z, cols):
        z_rows = t if with_z else 16
        return [pl.BlockSpec((1, t, PAIR_K), lambda i, hp: (i, 0, cols["cq"] // PAIR_K + hp)),
                pl.BlockSpec((1, t, PAIR_K), lambda i, hp: (i, 0, cols["ck"] // PAIR_K + hp)),
                pl.BlockSpec((1, t, PAIR_V), lambda i, hp: (i, 0, cols["cv"] // PAIR_V + hp)),
                pl.BlockSpec((1, z_rows, PAIR_V), lambda i, hp: (i, 0, cols["cz"] // PAIR_V + hp)),
                pl.BlockSpec((1, t, LANES), lambda i, hp: (i, 0, 0))]

    def whole(shape):
        nd = len(shape)
        return pl.BlockSpec(shape, lambda i, hp: (0,) * nd)

    emat_f, emat_b, masks = consts
    specs = side(t_lat, True, FULL_COLS) + side(t_ctx, want_ctx, ctx_cols or FULL_COLS) + [
        pl.BlockSpec((1, LANES, 2 * PAIR_K), lambda i, hp: (hp, 0, 0)),
        pl.BlockSpec((1, 1, 2 * PAIR_K), lambda i, hp: (hp, 0, 0)),
        pl.BlockSpec((1, 1, PAIR_V), lambda i, hp: (hp, 0, 0)),
        whole(emat_f.shape), whole(emat_b.shape), whole(masks.shape)]
    out_specs = [pl.BlockSpec((1, t_lat, PAIR_V), lambda i, hp: (i, 0, hp))]
    out_shape = [jax.ShapeDtypeStruct((b, t_lat, BR_W), BF16)]
    if want_ctx:
        out_specs.append(pl.BlockSpec((1, t_ctx, PAIR_V), lambda i, hp: (i, 0, hp)))
        out_shape.append(jax.ShapeDtypeStruct((b, t_ctx, BR_W), BF16))

    def kern(*refs):
        refs = list(refs)
        for n in range(10, 13):
            refs[n] = refs[n].at[0]
        _gla_kernel(*refs, t_lat=t_lat, t_ctx=t_ctx, want_ctx=want_ctx)

    res = pl.pallas_call(
        kern,
        grid=(b, n_pairs),
        in_specs=specs,
        out_specs=out_specs,
        out_shape=out_shape,
        scratch_shapes=[pltpu.VMEM((C_HEAD_V, PAIR_K), F32), pltpu.VMEM((C_HEAD_V, PAIR_K), F32),
                        pltpu.VMEM((t_lat, PAIR_V), F32), pltpu.VMEM((t_lat, PAIR_V), F32),
                        pltpu.VMEM((t_lat, 2 * PAIR_K), BF16), pltpu.VMEM((t_lat, 2 * PAIR_K), BF16),
                        pltpu.VMEM((t_lat, 2 * PAIR_K), BF16)],
        compiler_params=_cparams(2),
        name="gla",
    )(p_lat, p_lat, p_lat, p_lat, r_lat, p_ctx, p_ctx, p_ctx, p_ctx, r_ctx,
      w2, b2, gain, emat_f, emat_b, masks)
    return (res[0], res[1]) if want_ctx else (res[0], None)


def _merge_kernel(mg_ref, ya_ref, yb_ref, yc_ref, yd_ref, h_ref, gate_ref, wbr_ref, wout_ref,
                  lng_ref, lnb_ref, o_ref, *, alpha):
    acc = None
    for n, y_ref in enumerate((ya_ref, yb_ref, yc_ref, yd_ref)):
        gate = _sigmoid(mg_ref[:, n * D_MODEL:(n + 1) * D_MODEL].astype(F32))
        term = gate * _dot(y_ref[...], wbr_ref[n])
        acc = term if acc is None else acc + term
    y = _dot(acc.astype(BF16), wout_ref[...])
    t = alpha * h_ref[...] + gate_ref[0] * y
    mu = jnp.mean(t, axis=-1, keepdims=True)
    tc = t - mu
    var = jnp.mean(tc * tc, axis=-1, keepdims=True)
    o_ref[...] = tc * lax.rsqrt(var + EPS) * lng_ref[...] + lnb_ref[...]


def _merge(p2, ya, yb, yc, yd, h, gate, rows_per_mod, w_br, w_out, ln_g, ln_b, alpha):
    m = h.shape[0]
    tm = min(512, m)
    assert m % tm == 0 and rows_per_mod % tm == 0
    tiles_per_mod = rows_per_mod // tm
    br_spec = pl.BlockSpec((tm, BR_W), lambda i: (i, 0))
    vec_spec = pl.BlockSpec((1, D_MODEL), lambda i: (0, 0))
    return pl.pallas_call(
        functools.partial(_merge_kernel, alpha=alpha),
        grid=(m // tm,),
        in_specs=[
            pl.BlockSpec((tm, N_BRANCH * D_MODEL), lambda i: (i, COL_MG)),
            br_spec, br_spec, br_spec, br_spec,
            pl.BlockSpec((tm, D_MODEL), lambda i: (i, 0)),
            pl.BlockSpec((1, 1, D_MODEL), lambda i: (i // tiles_per_mod, 0, 0)),
            pl.BlockSpec((N_BRANCH, BR_W, D_MODEL), lambda i: (0, 0, 0)),
            pl.BlockSpec((D_MODEL, D_MODEL), lambda i: (0, 0)),
            vec_spec, vec_spec,
        ],
        out_specs=pl.BlockSpec((tm, D_MODEL), lambda i: (i, 0)),
        out_shape=jax.ShapeDtypeStruct((m, D_MODEL), F32),
        compiler_params=_cparams(1),
        name="merge",
    )(p2, ya, yb, yc, yd, h, gate, w_br, w_out, ln_g.reshape(1, D_MODEL), ln_b.reshape(1, D_MODEL))


def _layer_weights(w_in_l, q_norm_l, k_norm_l, c_w2_l, c_b2_l, c_norm_l):
    cuts = [int(c) for c in np.cumsum(IN_WIDTHS)[:-1]]
    parts = jnp.split(w_in_l, cuts, axis=-1)
    perm = np.concatenate([np.arange(0, HEAD_DIM, 2), np.arange(1, HEAD_DIM, 2)])

    def deinterleave(w, heads):
        return w.reshape(D_MODEL, heads, HEAD_DIM)[:, :, perm].reshape(D_MODEL, heads * HEAD_DIM)

    main = [parts[15], deinterleave(parts[0], A_HEADS), deinterleave(parts[1], A_KV_HEADS), parts[2], parts[3],
            parts[4], parts[5], parts[6], parts[7],
            parts[8], parts[9], parts[10], parts[11],
            parts[13], parts[14]]
    w_main = jnp.concatenate(main, axis=-1).astype(BF16)
    w_r = jnp.pad(parts[12], ((0, 0), (0, LANES - 2 * C_GATE_RANK))).astype(BF16)
    n_pairs = C_HEADS // 2

    def gate_w(i):
        w = jnp.zeros((LANES, C_KEY_W), F32).at[i * C_GATE_RANK:(i + 1) * C_GATE_RANK].set(c_w2_l[i])
        return w.reshape(LANES, n_pairs, PAIR_K).transpose(1, 0, 2)

    w2 = jnp.concatenate([gate_w(0), gate_w(1)], axis=-1).astype(BF16)
    b2 = jnp.concatenate([c_b2_l[0].reshape(n_pairs, 1, PAIR_K), c_b2_l[1].reshape(n_pairs, 1, PAIR_K)], axis=-1)
    return dict(
        w_main=w_main, w_r=w_r,
        q_gain=q_norm_l[perm].reshape(1, HEAD_DIM), k_gain=k_norm_l[perm].reshape(1, HEAD_DIM),
        w2=w2, b2=b2, c_gain=c_norm_l.reshape(n_pairs, 1, PAIR_V))


def _rope_tables(t):
    rows = t // GRID_W
    row = jnp.repeat(jnp.arange(rows), GRID_W).astype(F32)
    col = jnp.tile(jnp.arange(GRID_W), rows).astype(F32)
    inv = ROPE_THETA ** (-jnp.arange(0, AXIS_DIM, 2, dtype=F32) / AXIS_DIM)
    ang = jnp.concatenate([row[:, None] * inv, col[:, None] * inv], -1)
    cos, sin = jnp.cos(ang), jnp.sin(ang)
    return jnp.concatenate([cos, cos], -1), jnp.concatenate([-sin, sin], -1)


def kernel(x, c, ctx, c_ctx, w_mod, b_mod, w_in, q_norm, k_norm, b_conv, c_gate_w2, c_gate_b, c_norm,
           d_conv_w, d_conv_b, d_norm_g, d_norm_b, w_br, w_out, ln_g, ln_b):
    b, t, d = x.shape
    t_ctx = ctx.shape[1]
    depth = w_mod.shape[0]
    assert d == D_MODEL and b < 16
    alpha = (2 * depth) ** 0.25

    cc = jnp.zeros((16, D_MODEL), F32).at[:b].set(c).at[b].set(c_ctx)
    mod = _modulation(cc, w_mod, b_mod)
    cos, sin = _rope_tables(t)
    consts = _gla_constants()

    h_lat = x.reshape(b * t, D_MODEL)
    h_ctx = ctx.reshape(b * t_ctx, D_MODEL)
    for l in range(depth):
        want_ctx = l < depth - 1
        lw = _layer_weights(w_in[l], q_norm[l], k_norm[l], c_gate_w2[l], c_gate_b[l], c_norm[l])
        shift, scale, gate = [mod[l, :, n * D_MODEL:(n + 1) * D_MODEL].reshape(16, 1, D_MODEL) for n in range(3)]
        p_lat, r_lat = _in_projection(h_lat, shift, scale, t, lw["w_main"], lw["w_r"])
        if want_ctx:
            w_ctx, ctx_cols = lw["w_main"], FULL_COLS
        else:
            w_ctx = jnp.concatenate([lw["w_main"][:, c:c + PROJ_TN] for c in CTX_KV_TILES], axis=1)
            ctx_cols = CTX_KV_COLS
        p_ctx, r_ctx = _in_projection(h_ctx, shift[b:], scale[b:], b * t_ctx, w_ctx, lw["w_r"])
        p_lat3 = p_lat.reshape(b, t, N_MAIN)
        p_ctx3 = p_ctx.reshape(b, t_ctx, p_ctx.shape[1])
        r_lat3 = r_lat.reshape(b, t, LANES)
        r_ctx3 = r_ctx.reshape(b, t_ctx, LANES)
        w_br_l = w_br[l].astype(BF16)
        w_out_l = w_out[l].astype(BF16)

        ya_l = _attention(p_lat3, p_lat3, p_ctx3, lw["q_gain"], lw["k_gain"], cos, sin, ctx_cols)
        yb_l, yd_l = _local_branches(p_lat, t, b_conv[l], d_conv_w[l], d_conv_b[l], d_norm_g[l], d_norm_b[l])
        yc_l, yc_c = _gla(p_lat3, r_lat3, p_ctx3, r_ctx3, lw["w2"], lw["b2"], lw["c_gain"], consts, want_ctx,
                          ctx_cols)
        h_lat_new = _merge(p_lat, ya_l.reshape(b * t, BR_W), yb_l, yc_l.reshape(b * t, BR_W), yd_l,
                           h_lat, gate, t, w_br_l, w_out_l, ln_g[l], ln_b[l], alpha)
        if want_ctx:
            ya_c = _attention(p_ctx3, None, p_ctx3, lw["q_gain"], lw["k_gain"], None, None)
            yb_c, yd_c = _local_branches(p_ctx, t_ctx, b_conv[l], d_conv_w[l], d_conv_b[l], d_norm_g[l],
                                         d_norm_b[l])
            h_ctx = _merge(p_ctx, ya_c.reshape(b * t_ctx, BR_W), yb_c, yc_c.reshape(b * t_ctx, BR_W), yd_c,
                           h_ctx, gate[b:], b * t_ctx, w_br_l, w_out_l, ln_g[l], ln_b[l], alpha)
        h_lat = h_lat_new
    return h_lat.reshape(b, t, D_MODEL)
```

```python
import functools

import numpy as np
import jax
import jax.numpy as jnp
from jax import lax
from jax.experimental import pallas as pl
from jax.experimental.pallas import tpu as pltpu

F32 = jnp.float32
BF16 = jnp.bfloat16

D_MODEL = 1024
GRID_W = 64
N_BRANCH = 4
BR_W = D_MODEL // 2
HEAD_DIM = 128
A_HEADS = BR_W // HEAD_DIM
A_KV_HEADS = A_HEADS // 2
ROPE_THETA = 10000.0
AXIS_DIM = HEAD_DIM // 2
B_CONV = 3
C_HEADS = 4
C_HEAD_K = BR_W // (2 * C_HEADS)
C_HEAD_V = BR_W // C_HEADS
C_KEY_W = C_HEADS * C_HEAD_K
C_GATE_RANK = 16
C_GATE_TAU = 16.0
C_CHUNK = 64
D_CONV = 31
EPS = 1e-6
IN_WIDTHS = (
    A_HEADS * HEAD_DIM, A_KV_HEADS * HEAD_DIM, A_KV_HEADS * HEAD_DIM, BR_W,
    BR_W, BR_W, BR_W, BR_W,
    C_KEY_W, C_KEY_W, C_HEADS * C_HEAD_V, BR_W, 2 * C_GATE_RANK,
    2 * BR_W, BR_W,
    N_BRANCH * D_MODEL,
)

COL_MG = 0
COL_AQ = 4096
COL_AK = COL_AQ + 512
COL_AV = COL_AK + 256
COL_AZ = COL_AV + 256
COL_BG = COL_AZ + 512
COL_BC = COL_BG + 512
COL_BX = COL_BC + 512
COL_BZ = COL_BX + 512
COL_CQ = COL_BZ + 512
COL_CK = COL_CQ + 256
COL_CV = COL_CK + 256
COL_CZ = COL_CV + 512
COL_DA = COL_CZ + 512
COL_DG = COL_DA + 512
COL_DZ = COL_DG + 512
N_MAIN = COL_DZ + 512
FULL_COLS = dict(ak=COL_AK, av=COL_AV, cq=COL_CQ, ck=COL_CK, cv=COL_CV, cz=COL_CZ)
CTX_KV_TILES = (COL_AK, COL_CQ, COL_CV)
CTX_KV_COLS = dict(ak=0, av=256, cq=512, ck=768, cv=1024, cz=1024)
LANES = 128
SUBLANES = 8
PROJ_TN = 512
LN_GROUP = 512
VMEM_LIMIT = 56 * 1024 * 1024
LOG2E = 1.4426950408889634


def _cparams(n_axes):
    return pltpu.CompilerParams(dimension_semantics=("arbitrary",) * n_axes,
                                vmem_limit_bytes=VMEM_LIMIT)


def _sigmoid(x):
    return 0.5 * jnp.tanh(0.5 * x) + 0.5


def _silu(x):
    h = 0.5 * x
    return h + h * jnp.tanh(h)


def _dot(a, b):
    return jnp.dot(a, b, preferred_element_type=F32)


def _dot_nt(a, b):
    return lax.dot_general(a, b, (((1,), (1,)), ((), ())), preferred_element_type=F32)


def _dot_tn(a, b):
    return lax.dot_general(a, b, (((0,), (0,)), ((), ())), preferred_element_type=F32)


def _mod_kernel(c_ref, w_ref, b_ref, o_ref):
    s = _silu(c_ref[...])
    o_ref[0] = _dot(s.astype(BF16), w_ref[0].astype(BF16)) + b_ref[0]


def _modulation(cc, w_mod, b_mod):
    depth = w_mod.shape[0]
    n_rows = cc.shape[0]
    return pl.pallas_call(
        _mod_kernel,
        grid=(depth, 3),
        in_specs=[
            pl.BlockSpec((n_rows, D_MODEL), lambda l, j: (0, 0)),
            pl.BlockSpec((1, D_MODEL, D_MODEL), lambda l, j: (l, 0, j)),
            pl.BlockSpec((1, 1, D_MODEL), lambda l, j: (l, 0, j)),
        ],
        out_specs=pl.BlockSpec((1, n_rows, D_MODEL), lambda l, j: (l, 0, j)),
        out_shape=jax.ShapeDtypeStruct((depth, n_rows, 3 * D_MODEL), F32),
        compiler_params=_cparams(2),
        name="modulation",
    )(cc, w_mod, b_mod.reshape(depth, 1, 3 * D_MODEL))


def _inproj_kernel(x_ref, shift_ref, scale_ref, w_ref, wr_ref, o_ref, or_ref, u_ref, *, tm, sub):
    j = pl.program_id(1)

    @pl.when(j == 0)
    def _():
        one_plus = 1.0 + scale_ref[0]
        shift = shift_ref[0]
        grp = min(LN_GROUP, tm)
        for g0 in range(0, tm, grp):
            for r0 in range(g0, g0 + grp, sub):
                x = x_ref[r0:r0 + sub, :]
                mu = jnp.mean(x, axis=-1, keepdims=True)
                xc = x - mu
                var = jnp.mean(xc * xc, axis=-1, keepdims=True)
                u = xc * lax.rsqrt(var + EPS) * one_plus + shift
                u_ref[r0:r0 + sub, :] = u.astype(BF16)
            u_grp = u_ref[g0:g0 + grp, :]
            or_ref[g0:g0 + grp, :] = _dot(u_grp, wr_ref[...]).astype(BF16)
            o_ref[g0:g0 + grp, :] = _dot(u_grp, w_ref[...]).astype(BF16)

    @pl.when(j != 0)
    def _():
        o_ref[...] = _dot(u_ref[...], w_ref[...]).astype(BF16)


def _in_projection(h, shift, scale, rows_per_mod, w_main, w_r, tiles=None):
    m = h.shape[0]
    tm = min(2048, rows_per_mod)
    assert m % tm == 0 and rows_per_mod % tm == 0
    tiles_per_mod = rows_per_mod // tm
    assert w_main.shape[1] % PROJ_TN == 0
    n_tiles = len(tiles) if tiles else w_main.shape[1] // PROJ_TN
    n_cols = n_tiles * PROJ_TN

    def w_tile(j):
        if not tiles:
            return j
        idx = tiles[0]
        for n in range(1, len(tiles)):
            idx = idx + (tiles[n] - tiles[n - 1]) * jnp.minimum(jnp.maximum(j - n + 1, 0), 1)
        return idx

    kern = functools.partial(_inproj_kernel, tm=tm, sub=min(128, tm))
    return pl.pallas_call(
        kern,
        grid=(m // tm, n_tiles),
        in_specs=[
            pl.BlockSpec((tm, D_MODEL), lambda i, j: (i, 0)),
            pl.BlockSpec((1, 1, D_MODEL), lambda i, j: (i // tiles_per_mod, 0, 0)),
            pl.BlockSpec((1, 1, D_MODEL), lambda i, j: (i // tiles_per_mod, 0, 0)),
            pl.BlockSpec((D_MODEL, PROJ_TN), lambda i, j: (0, w_tile(j))),
            pl.BlockSpec((D_MODEL, LANES), lambda i, j: (0, 0)),
        ],
        out_specs=[
            pl.BlockSpec((tm, PROJ_TN), lambda i, j: (i, j)),
            pl.BlockSpec((tm, LANES), lambda i, j: (i, 0)),
        ],
        out_shape=[
            jax.ShapeDtypeStruct((m, n_cols), BF16),
            jax.ShapeDtypeStruct((m, LANES), BF16),
        ],
        scratch_shapes=[pltpu.VMEM((tm, D_MODEL), BF16)],
        compiler_params=_cparams(2),
        name="in_projection",
    )(h, shift, scale, w_main, w_r)


ATTN_KEY_BLOCKS = (768, 512, 256)
QK_AHEAD = 2


def _rms(x):
    return x * lax.rsqrt(jnp.mean(x * x, axis=-1, keepdims=True) + EPS)


def _rope(x, cos, sin):
    return x * cos + pltpu.roll(x, HEAD_DIM // 2, 1) * sin


def _attn_kernel(*refs, n_lat, n_ctx, tq, kblk, kstep):
    if n_lat:
        (q_ref, z_ref, kl_ref, vl_ref, kc_ref, vc_ref, qg_ref, kg_ref, cos_ref, sin_ref,
         o_ref, kn_ref, vn_ref) = refs
    else:
        q_ref, z_ref, kc_ref, vc_ref, qg_ref, kg_ref, o_ref, kn_ref, vn_ref = refs
    qi = pl.program_id(2)
    kg = kg_ref[...]

    @pl.when(qi == 0)
    def _prep():
        if n_lat:
            def body(t, carry):
                r0 = pl.multiple_of(t * kblk, kblk)
                kn = _rms(kl_ref[0, pl.ds(r0, kblk), :].astype(F32)) * kg
                kn = _rope(kn, cos_ref[pl.ds(r0, kblk), :], sin_ref[pl.ds(r0, kblk), :])
                kn_ref[pl.ds(r0, kblk), :] = kn.astype(BF16)
                vn_ref[pl.ds(r0, kblk), 0:HEAD_DIM] = vl_ref[0, pl.ds(r0, kblk), :]
                return carry

            lax.fori_loop(0, n_lat // kblk, body, 0)
        kn_ref[n_lat:n_lat + n_ctx, :] = (_rms(kc_ref[0].astype(F32)) * kg).astype(BF16)
        vn_ref[n_lat:n_lat + n_ctx, 0:HEAD_DIM] = vc_ref[0]
        vn_ref[:, HEAD_DIM:2 * HEAD_DIM] = jnp.ones((n_lat + n_ctx, HEAD_DIM), BF16)

    qg = qg_ref[...]
    if n_lat:
        q0 = pl.multiple_of(qi * tq, tq)
        cos = cos_ref[pl.ds(q0, tq), :]
        sin = sin_ref[pl.ds(q0, tq), :]
    qns = []
    for g in range(2):
        qn = _rms(q_ref[0, :, g * HEAD_DIM:(g + 1) * HEAD_DIM].astype(F32)) * qg
        if n_lat:
            qn = _rope(qn, cos, sin)
        qns.append((qn * (HEAD_DIM ** -0.5 * LOG2E)).astype(BF16))

    items = [(g, k0) for k0 in range(0, n_lat + n_ctx, kstep) for g in range(2)]

    def scores(item):
        g, k0 = item
        return _dot_nt(qns[g], kn_ref[k0:k0 + kstep, :])

    m = [None, None]
    acc = [None, None]
    ahead = [scores(item) for item in items[:QK_AHEAD]]
    for n, (g, k0) in enumerate(items):
        s = ahead.pop(0)
        if n + QK_AHEAD < len(items):
            ahead.append(scores(items[n + QK_AHEAD]))
        m_blk = jnp.max(s, axis=-1, keepdims=True)
        m_new = m_blk if m[g] is None else jnp.maximum(m[g], m_blk)
        p = jnp.exp2((s - m_new).astype(BF16))
        pv = _dot(p, vn_ref[k0:k0 + kstep, :])
        acc[g] = pv if acc[g] is None else acc[g] * jnp.exp2(m[g] - m_new) + pv
        m[g] = m_new
    for g in range(2):
        lanes = slice(g * HEAD_DIM, (g + 1) * HEAD_DIM)
        o = acc[g][:, 0:HEAD_DIM] / acc[g][:, HEAD_DIM:2 * HEAD_DIM]
        o_ref[0, :, lanes] = (o * _silu(z_ref[0, :, lanes].astype(F32))).astype(BF16)


def _attention(p_q, p_lat, p_ctx, q_gain, k_gain, cos, sin, ctx_cols=None):
    b, t_q, _ = p_q.shape
    n_ctx = p_ctx.shape[1]
    n_lat = 0 if p_lat is None else p_lat.shape[1]
    tq = min(256, t_q)
    assert t_q % tq == 0
    kblk = min(256, n_lat) if n_lat else 0
    n_keys = n_lat + n_ctx
    kstep = next((c for c in ATTN_KEY_BLOCKS if n_keys % c == 0), n_keys)
    hw = 2 * HEAD_DIM
    q_spec = pl.BlockSpec((1, tq, hw), lambda i, kv, qi: (i, qi, COL_AQ // hw + kv))
    z_spec = pl.BlockSpec((1, tq, hw), lambda i, kv, qi: (i, qi, COL_AZ // hw + kv))

    def kv_spec(n, col):
        return pl.BlockSpec((1, n, HEAD_DIM), lambda i, kv, qi: (i, 0, col // HEAD_DIM + kv))

    gain_spec = pl.BlockSpec((1, HEAD_DIM), lambda i, kv, qi: (0, 0))
    args = [p_q, p_q]
    specs = [q_spec, z_spec]
    if n_lat:
        args += [p_lat, p_lat]
        specs += [kv_spec(n_lat, COL_AK), kv_spec(n_lat, COL_AV)]
    ctx_cols = ctx_cols or FULL_COLS
    args += [p_ctx, p_ctx, q_gain, k_gain]
    specs += [kv_spec(n_ctx, ctx_cols["ak"]), kv_spec(n_ctx, ctx_cols["av"]), gain_spec, gain_spec]
    if n_lat:
        tab_spec = pl.BlockSpec((n_lat, HEAD_DIM), lambda i, kv, qi: (0, 0))
        args += [cos, sin]
        specs += [tab_spec, tab_spec]
    kern = functools.partial(_attn_kernel, n_lat=n_lat, n_ctx=n_ctx, tq=tq, kblk=kblk, kstep=kstep)
    return pl.pallas_call(
        kern,
        grid=(b, A_KV_HEADS, t_q // tq),
        in_specs=specs,
        out_specs=pl.BlockSpec((1, tq, hw), lambda i, kv, qi: (i, qi, kv)),
        out_shape=jax.ShapeDtypeStruct((b, t_q, BR_W), BF16),
        scratch_shapes=[pltpu.VMEM((n_lat + n_ctx, HEAD_DIM), BF16),
                        pltpu.VMEM((n_lat + n_ctx, 2 * HEAD_DIM), BF16)],
        compiler_params=_cparams(3),
        name="attention_lat" if n_lat else "attention_ctx",
    )(*args)


D_HALO = 16
B_HALO = 16
CONV_RB = 32
CONV_ACCS = 4
ELEM_RB = 32
COPY_RB = 56


def _local_kernel(bg_ref, bc_ref, bcp_ref, bcn_ref, bx_ref, bxp_ref, bxn_ref, bz_ref,
                  da_ref, dap_ref, dan_ref, dg_ref, dgp_ref, dgn_ref, dz_ref,
                  bw_ref, dw_ref, db_ref, dgain_ref, dbeta_ref,
                  yb_ref, yd_ref, tbuf, gbuf, hbuf, *, tt, tiles_per_seq):
    i = pl.program_id(0)
    pos = i % tiles_per_seq
    keep_prev = (pos != 0).astype(F32)
    keep_next = (pos != tiles_per_seq - 1).astype(F32)

    def ld(ref, r0=0, n=None):
        return ref[r0:r0 + (n or ref.shape[0]), :].astype(F32)

    row_blocks = range(0, tt, ELEM_RB)

    tbuf[0:B_HALO, :] = ld(bcp_ref) * ld(bxp_ref) * keep_prev
    for r0 in row_blocks:
        tbuf[B_HALO + r0:B_HALO + r0 + ELEM_RB, :] = ld(bc_ref, r0, ELEM_RB) * ld(bx_ref, r0, ELEM_RB)
    tbuf[B_HALO + tt:2 * B_HALO + tt, :] = ld(bcn_ref) * ld(bxn_ref) * keep_next
    for r0 in row_blocks:
        lo = B_HALO + r0
        conv = (bw_ref[0:1, :] * tbuf[lo - 1:lo - 1 + ELEM_RB, :] + bw_ref[1:2, :] * tbuf[lo:lo + ELEM_RB, :]
                + bw_ref[2:3, :] * tbuf[lo + 1:lo + 1 + ELEM_RB, :])
        yb_ref[r0:r0 + ELEM_RB, :] = (ld(bg_ref, r0, ELEM_RB) * conv
                                      * _silu(ld(bz_ref, r0, ELEM_RB))).astype(BF16)

    gbuf[0, 0:D_HALO, :] = ld(dap_ref) * _sigmoid(ld(dgp_ref)) * keep_prev
    for r0 in row_blocks:
        gbuf[0, D_HALO + r0:D_HALO + r0 + ELEM_RB, :] = (ld(da_ref, r0, ELEM_RB)
                                                          * _sigmoid(ld(dg_ref, r0, ELEM_RB)))
    gbuf[0, D_HALO + tt:2 * D_HALO + tt, :] = ld(dan_ref) * _sigmoid(ld(dgn_ref)) * keep_next
    span = tt + 2 * D_HALO - SUBLANES
    for r in range(1, SUBLANES):
        for x0 in range(0, span, COPY_RB):
            n = min(COPY_RB, span - x0)
            gbuf[r, x0:x0 + n, :] = gbuf[0, r + x0:r + x0 + n, :]
    base = D_HALO - D_CONV // 2
    for cb in range(BR_W // LANES):
        lanes = slice(cb * LANES, (cb + 1) * LANES)
        for rb in range(tt // CONV_RB):
            accs = [None] * CONV_ACCS
            for k in range(D_CONV):
                off = base + k
                r0 = rb * CONV_RB + off - off % SUBLANES
                rows = gbuf[off % SUBLANES, r0:r0 + CONV_RB, lanes]
                term = rows.reshape(CONV_RB // SUBLANES, SUBLANES, LANES) * dw_ref[k, :, lanes][None]
                a = k % CONV_ACCS
                accs[a] = term if accs[a] is None else accs[a] + term
            while len(accs) > 1:
                accs = [accs[n] + accs[n + 1] for n in range(0, len(accs), 2)]
            hbuf[rb * CONV_RB:(rb + 1) * CONV_RB, lanes] = accs[0].reshape(CONV_RB, LANES)
    for r0 in row_blocks:
        hh = hbuf[r0:r0 + ELEM_RB, :] + db_ref[...]
        mu = jnp.mean(hh, axis=-1, keepdims=True)
        hc = hh - mu
        var = jnp.mean(hc * hc, axis=-1, keepdims=True)
        hn = hc * lax.rsqrt(var + EPS) * dgain_ref[...] + dbeta_ref[...]
        yd_ref[r0:r0 + ELEM_RB, :] = (_silu(hn) * _silu(ld(dz_ref, r0, ELEM_RB))).astype(BF16)


def _local_branches(p2, seq_len, b_w, d_w, d_b, d_g, d_beta):
    m = p2.shape[0]
    tt = min(256, seq_len)
    assert seq_len % tt == 0 and m % seq_len == 0
    tiles_per_seq = seq_len // tt
    n_tiles = m // tt

    def cur(col):
        return pl.BlockSpec((tt, BR_W), lambda i: (i, col // BR_W))

    def prev(col, halo):
        per = tt // halo
        return pl.BlockSpec((halo, BR_W), lambda i: (jnp.maximum(i * per - 1, 0), col // BR_W))

    def nxt(col, halo):
        per = tt // halo
        last = m // halo - 1
        return pl.BlockSpec((halo, BR_W), lambda i: (jnp.minimum((i + 1) * per, last), col // BR_W))

    def small(rows):
        return pl.BlockSpec((rows, BR_W), lambda i: (0, 0))

    specs = [cur(COL_BG),
             cur(COL_BC), prev(COL_BC, B_HALO), nxt(COL_BC, B_HALO),
             cur(COL_BX), prev(COL_BX, B_HALO), nxt(COL_BX, B_HALO),
             cur(COL_BZ),
             cur(COL_DA), prev(COL_DA, D_HALO), nxt(COL_DA, D_HALO),
             cur(COL_DG), prev(COL_DG, D_HALO), nxt(COL_DG, D_HALO),
             cur(COL_DZ),
             small(B_CONV), pl.BlockSpec((D_CONV, SUBLANES, BR_W), lambda i: (0, 0, 0)),
             small(1), small(1), small(1)]
    kern = functools.partial(_local_kernel, tt=tt, tiles_per_seq=tiles_per_seq)
    out_spec = pl.BlockSpec((tt, BR_W), lambda i: (i, 0))
    return pl.pallas_call(
        kern,
        grid=(n_tiles,),
        in_specs=specs,
        out_specs=[out_spec, out_spec],
        out_shape=[jax.ShapeDtypeStruct((m, BR_W), BF16), jax.ShapeDtypeStruct((m, BR_W), BF16)],
        scratch_shapes=[pltpu.VMEM((tt + 2 * B_HALO, BR_W), F32),
                        pltpu.VMEM((SUBLANES, tt + 2 * D_HALO, BR_W), F32),
                        pltpu.VMEM((tt, BR_W), F32)],
        compiler_params=_cparams(1),
        name="local_branches",
    )(*([p2] * 15), b_w, jnp.broadcast_to(d_w[:, None, :], (D_CONV, SUBLANES, BR_W)),
      d_b.reshape(1, BR_W), d_g.reshape(1, BR_W), d_beta.reshape(1, BR_W))


CH = C_CHUNK
PAIR_K = 2 * C_HEAD_K
PAIR_V = 2 * C_HEAD_V
LEVELS = (64, 32, 16, 8, 4, 2)
GLA_UNROLL = 4
ROW_EQ = 0
ROW_EK = CH
ROW_LAST = 2 * CH
ROW_LVL = 2 * CH + 8
N_EROWS = ROW_LVL + len(LEVELS) * CH


def _gla_constants():
    idx = np.arange(CH)
    cols = np.arange(2 * CH) % CH
    emats, masks = [], []
    for reverse in (False, True):
        tri = (idx[None, :] >= idx[:, None]) if reverse else (idx[None, :] <= idx[:, None])
        tri = tri.astype(np.float32)
        edge = tri[0] if reverse else tri[CH - 1]
        blocks = [tri, edge[None, :] - tri, np.tile(edge[None, :], (8, 1))]
        lvl_masks = []
        for grp in LEVELS:
            half = grp // 2
            in_q = (idx % grp < half) if reverse else (idx % grp >= half)
            ref = (idx // grp) * grp + (half if reverse else half - 1)
            d = tri - tri[ref]
            blocks.append(np.where(in_q[:, None], d, -d))
            in_q_col = (cols % grp < half) if reverse else (cols % grp >= half)
            lvl_masks.append((idx[:, None] // grp == cols[None, :] // grp) & in_q[:, None] & ~in_q_col[None, :])
        lvl_masks.append(idx[:, None] == cols[None, :])
        emat = np.concatenate(blocks, axis=0)
        assert emat.shape == (N_EROWS, CH)
        emats.append(np.concatenate([emat] * 3, axis=1))
        masks.append(np.stack(lvl_masks).astype(np.float32))
    return (jnp.asarray(emats[0], BF16), jnp.asarray(emats[1], BF16), jnp.asarray(np.stack(masks), F32))


def _split3(g):
    g1 = g.astype(BF16)
    r1 = g - g1.astype(F32)
    g2 = r1.astype(BF16)
    g3 = (r1 - g2.astype(F32)).astype(BF16)
    return g1, g2, g3


def _log2_sigmoid(x):
    e = jnp.exp2(jnp.abs(x) * -LOG2E)
    return jnp.minimum(x, 0.0) * LOG2E - jnp.log2(1.0 + e)


def _gla_kernel(*refs, t_lat, t_ctx, want_ctx):
    (ql_ref, kl_ref, vl_ref, zl_ref, rl_ref, qc_ref, kc_ref, vc_ref, zc_ref, rc_ref,
     w2_ref, b2_ref, gain_ref, ematf_ref, ematb_ref, masks_ref) = refs[:16]
    if want_ctx:
        yl_ref, yc_ref = refs[16:18]
        scratch = refs[18:]
    else:
        yl_ref = refs[16]
        scratch = refs[17:]
        yc_ref = zc_ref = None
    stf_ref, stb_ref, of_ref, ob_ref, g1_ref, g2_ref, g3_ref = scratch
    unroll = GLA_UNROLL

    def gates(r_ref, n_rows):
        blk = min(256, n_rows)

        def body(t, carry):
            rows = pl.ds(pl.multiple_of(t * blk, blk), blk)
            x = _dot(r_ref[0, rows, :], w2_ref[...]) + b2_ref[...]
            g1, g2, g3 = _split3(_log2_sigmoid(x) * (1.0 / C_GATE_TAU))
            g1_ref[rows, :] = g1
            g2_ref[rows, :] = g2
            g3_ref[rows, :] = g3
            return carry

        lax.fori_loop(0, n_rows // blk, body, 0)

    def run(q_ref, k_ref, v_ref, n_chunks, want_out):
        assert n_chunks % unroll == 0
        lane_k = lax.broadcasted_iota(jnp.int32, (CH, PAIR_K), 1)
        head0 = lane_k < C_HEAD_K
        zero_v = jnp.zeros((CH, C_HEAD_V), BF16)

        def body(i, carry):
            work = []
            for reverse in (False, True):
                lanes = slice(PAIR_K, 2 * PAIR_K) if reverse else slice(0, PAIR_K)
                rows = []
                for u in range(unroll):
                    c = i * unroll + u
                    c = (n_chunks - 1 - c) if reverse else c
                    rows.append(pl.ds(pl.multiple_of(c * CH, CH), CH))
                rhs = jnp.concatenate(
                    [jnp.concatenate([g_ref[r, lanes] for r in rows], axis=1)
                     for g_ref in (g1_ref, g2_ref, g3_ref)], axis=0)
                emat = ematb_ref[...] if reverse else ematf_ref[...]
                e_all = jnp.exp2(_dot(emat, rhs))
                for u in range(unroll):
                    work.append(dict(reverse=reverse, rows=rows[u], u=u,
                                     e=e_all[:, u * PAIR_K:(u + 1) * PAIR_K]))
            work.sort(key=lambda w: w["u"])

            for w in work:
                st_ref = stb_ref if w["reverse"] else stf_ref
                e = w["e"]
                q = q_ref[0, w["rows"], :].astype(F32) * C_HEAD_K ** -0.5
                k = k_ref[0, w["rows"], :].astype(F32)
                v = v_ref[0, w["rows"], :]
                k0 = jnp.where(head0, k, 0.0)
                k1 = jnp.where(head0, 0.0, k)
                st = st_ref[...]
                e_k = e[ROW_EK:ROW_EK + CH]
                k_dec = jnp.concatenate([k0 * e_k, k1 * e_k], axis=0).astype(BF16)
                v_rows = jnp.concatenate([v[:, :C_HEAD_V], v[:, C_HEAD_V:]], axis=0)
                st_ref[...] = e[ROW_LAST:ROW_LAST + 1] * st + _dot_tn(v_rows, k_dec)
                if want_out:
                    qe = q * e[ROW_EQ:ROW_EQ + CH]
                    q_rows = jnp.concatenate([jnp.where(head0, qe, 0.0), jnp.where(head0, 0.0, qe)], axis=0)
                    o_st = _dot(q_rows.astype(BF16), st.T.astype(BF16))
                    w.update(q=q, k0=k0, k1=k1, v=v, o=jnp.concatenate([o_st[:CH], o_st[CH:]], axis=1))
            if not want_out:
                return carry

            for w in work:
                masks = masks_ref.at[1 if w["reverse"] else 0]
                q, k0, k1, e = w["q"], w["k0"], w["k1"], w["e"]
                k_heads = jnp.concatenate([k0, k1], axis=0)
                scores = _dot(q.astype(BF16), k_heads.T.astype(BF16)) * masks[len(LEVELS)]
                for lvl in range(len(LEVELS)):
                    e_l = e[ROW_LVL + lvl * CH:ROW_LVL + (lvl + 1) * CH]
                    kk_t = jnp.concatenate([k0 * e_l, k1 * e_l], axis=0).T.astype(BF16)
                    scores = scores + _dot((q * e_l).astype(BF16), kk_t) * masks[lvl]
                w["scores"] = scores.astype(BF16)

            for w in work:
                v = w["v"]
                v_bd = jnp.concatenate([jnp.concatenate([v[:, :C_HEAD_V], zero_v], axis=1),
                                        jnp.concatenate([zero_v, v[:, C_HEAD_V:]], axis=1)], axis=0)
                (ob_ref if w["reverse"] else of_ref)[w["rows"], :] = w["o"] + _dot(w["scores"], v_bd)
            return carry

        lax.fori_loop(0, n_chunks // unroll, body, 0)

    def finish(z_ref, y_ref, n_rows):
        blk = min(256, n_rows)

        def body(t, carry):
            rows = pl.ds(pl.multiple_of(t * blk, blk), blk)
            o = of_ref[rows, :] + ob_ref[rows, :]
            z = z_ref[0, rows, :].astype(F32)
            gain = gain_ref[...]
            for h in range(2):
                lanes = slice(h * C_HEAD_V, (h + 1) * C_HEAD_V)
                y = _rms(o[:, lanes]) * gain[:, lanes]
                y_ref[0, rows, lanes] = (y * _silu(z[:, lanes])).astype(BF16)
            return carry

        lax.fori_loop(0, n_rows // blk, body, 0)

    stf_ref[...] = jnp.zeros_like(stf_ref)
    stb_ref[...] = jnp.zeros_like(stb_ref)
    gates(rc_ref, t_ctx)
    run(qc_ref, kc_ref, vc_ref, t_ctx // CH, want_ctx)
    if want_ctx:
        finish(zc_ref, yc_ref, t_ctx)
    gates(rl_ref, t_lat)
    run(ql_ref, kl_ref, vl_ref, t_lat // CH, True)
    finish(zl_ref, yl_ref, t_lat)


def _gla(p_lat, r_lat, p_ctx, r_ctx, w2, b2, gain, consts, want_ctx, ctx_cols=None):
    b, t_lat, _ = p_lat.shape
    t_ctx = p_ctx.shape[1]
    n_pairs = C_HEADS // 2

    def side(t, with_z, cols):
        z_rows = t if with_z else 16
        return [pl.BlockSpec((1, t, PAIR_K), lambda i, hp: (i, 0, cols["cq"] // PAIR_K + hp)),
                pl.BlockSpec((1, t, PAIR_K), lambda i, hp: (i, 0, cols["ck"] // PAIR_K + hp)),
                pl.BlockSpec((1, t, PAIR_V), lambda i, hp: (i, 0, cols["cv"] // PAIR_V + hp)),
                pl.BlockSpec((1, z_rows, PAIR_V), lambda i, hp: (i, 0, cols["cz"] // PAIR_V + hp)),
                pl.BlockSpec((1, t, LANES), lambda i, hp: (i, 0, 0))]

    def whole(shape):
        nd = len(shape)
        return pl.BlockSpec(shape, lambda i, hp: (0,) * nd)

    emat_f, emat_b, masks = consts
    specs = side(t_lat, True, FULL_COLS) + side(t_ctx, want_ctx, ctx_cols or FULL_COLS) + [
        pl.BlockSpec((1, LANES, 2 * PAIR_K), lambda i, hp: (hp, 0, 0)),
        pl.BlockSpec((1, 1, 2 * PAIR_K), lambda i, hp: (hp, 0, 0)),
        pl.BlockSpec((1, 1, PAIR_V), lambda i, hp: (hp, 0, 0)),
        whole(emat_f.shape), whole(emat_b.shape), whole(masks.shape)]
    out_specs = [pl.BlockSpec((1, t_lat, PAIR_V), lambda i, hp: (i, 0, hp))]
    out_shape = [jax.ShapeDtypeStruct((b, t_lat, BR_W), BF16)]
    if want_ctx:
        out_specs.append(pl.BlockSpec((1, t_ctx, PAIR_V), lambda i, hp: (i, 0, hp)))
        out_shape.append(jax.ShapeDtypeStruct((b, t_ctx, BR_W), BF16))

    def kern(*refs):
        refs = list(refs)
        for n in range(10, 13):
            refs[n] = refs[n].at[0]
        _gla_kernel(*refs, t_lat=t_lat, t_ctx=t_ctx, want_ctx=want_ctx)

    res = pl.pallas_call(
        kern,
        grid=(b, n_pairs),
        in_specs=specs,
        out_specs=out_specs,
        out_shape=out_shape,
        scratch_shapes=[pltpu.VMEM((C_HEAD_V, PAIR_K), F32), pltpu.VMEM((C_HEAD_V, PAIR_K), F32),
                        pltpu.VMEM((t_lat, PAIR_V), F32), pltpu.VMEM((t_lat, PAIR_V), F32),
                        pltpu.VMEM((t_lat, 2 * PAIR_K), BF16), pltpu.VMEM((t_lat, 2 * PAIR_K), BF16),
                        pltpu.VMEM((t_lat, 2 * PAIR_K), BF16)],
        compiler_params=_cparams(2),
        name="gla",
    )(p_lat, p_lat, p_lat, p_lat, r_lat, p_ctx, p_ctx, p_ctx, p_ctx, r_ctx,
      w2, b2, gain, emat_f, emat_b, masks)
    return (res[0], res[1]) if want_ctx else (res[0], None)


def _merge_kernel(mg_ref, ya_ref, yb_ref, yc_ref, yd_ref, h_ref, gate_ref, wbr_ref, wout_ref,
                  lng_ref, lnb_ref, o_ref, *, alpha):
    acc = None
    for n, y_ref in enumerate((ya_ref, yb_ref, yc_ref, yd_ref)):
        gate = _sigmoid(mg_ref[:, n * D_MODEL:(n + 1) * D_MODEL].astype(F32))
        term = gate * _dot(y_ref[...], wbr_ref[n])
        acc = term if acc is None else acc + term
    y = _dot(acc.astype(BF16), wout_ref[...])
    t = alpha * h_ref[...] + gate_ref[0] * y
    mu = jnp.mean(t, axis=-1, keepdims=True)
    tc = t - mu
    var = jnp.mean(tc * tc, axis=-1, keepdims=True)
    o_ref[...] = tc * lax.rsqrt(var + EPS) * lng_ref[...] + lnb_ref[...]


def _merge(p2, ya, yb, yc, yd, h, gate, rows_per_mod, w_br, w_out, ln_g, ln_b, alpha):
    m = h.shape[0]
    tm = min(512, m)
    assert m % tm == 0 and rows_per_mod % tm == 0
    tiles_per_mod = rows_per_mod // tm
    br_spec = pl.BlockSpec((tm, BR_W), lambda i: (i, 0))
    vec_spec = pl.BlockSpec((1, D_MODEL), lambda i: (0, 0))
    return pl.pallas_call(
        functools.partial(_merge_kernel, alpha=alpha),
        grid=(m // tm,),
        in_specs=[
            pl.BlockSpec((tm, N_BRANCH * D_MODEL), lambda i: (i, COL_MG)),
            br_spec, br_spec, br_spec, br_spec,
            pl.BlockSpec((tm, D_MODEL), lambda i: (i, 0)),
            pl.BlockSpec((1, 1, D_MODEL), lambda i: (i // tiles_per_mod, 0, 0)),
            pl.BlockSpec((N_BRANCH, BR_W, D_MODEL), lambda i: (0, 0, 0)),
            pl.BlockSpec((D_MODEL, D_MODEL), lambda i: (0, 0)),
            vec_spec, vec_spec,
        ],
        out_specs=pl.BlockSpec((tm, D_MODEL), lambda i: (i, 0)),
        out_shape=jax.ShapeDtypeStruct((m, D_MODEL), F32),
        compiler_params=_cparams(1),
        name="merge",
    )(p2, ya, yb, yc, yd, h, gate, w_br, w_out, ln_g.reshape(1, D_MODEL), ln_b.reshape(1, D_MODEL))


def _layer_weights(w_in_l, q_norm_l, k_norm_l, c_w2_l, c_b2_l, c_norm_l):
    cuts = [int(c) for c in np.cumsum(IN_WIDTHS)[:-1]]
    parts = jnp.split(w_in_l, cuts, axis=-1)
    perm = np.concatenate([np.arange(0, HEAD_DIM, 2), np.arange(1, HEAD_DIM, 2)])

    def deinterleave(w, heads):
        w3 = w.reshape(D_MODEL, heads, HEAD_DIM)
        return jnp.concatenate([w3[:, :, 0::2], w3[:, :, 1::2]], axis=-1).reshape(D_MODEL, heads * HEAD_DIM)

    main = [parts[15], deinterleave(parts[0], A_HEADS), deinterleave(parts[1], A_KV_HEADS), parts[2], parts[3],
            parts[4], parts[5], parts[6], parts[7],
            parts[8], parts[9], parts[10], parts[11],
            parts[13], parts[14]]
    w_main = jnp.concatenate(main, axis=-1).astype(BF16)
    w_r = jnp.pad(parts[12], ((0, 0), (0, LANES - 2 * C_GATE_RANK))).astype(BF16)
    n_pairs = C_HEADS // 2

    def gate_w(i):
        w = jnp.pad(c_w2_l[i], ((i * C_GATE_RANK, LANES - (i + 1) * C_GATE_RANK), (0, 0)))
        return w.reshape(LANES, n_pairs, PAIR_K).transpose(1, 0, 2)

    w2 = jnp.concatenate([gate_w(0), gate_w(1)], axis=-1).astype(BF16)
    b2 = jnp.concatenate([c_b2_l[0].reshape(n_pairs, 1, PAIR_K), c_b2_l[1].reshape(n_pairs, 1, PAIR_K)], axis=-1)
    return dict(
        w_main=w_main, w_r=w_r,
        q_gain=q_norm_l[perm].reshape(1, HEAD_DIM), k_gain=k_norm_l[perm].reshape(1, HEAD_DIM),
        w2=w2, b2=b2, c_gain=c_norm_l.reshape(n_pairs, 1, PAIR_V))


def _rope_tables(t):
    rows = t // GRID_W
    row = np.repeat(np.arange(rows), GRID_W).astype(np.float32)
    col = np.tile(np.arange(GRID_W), rows).astype(np.float32)
    inv = (ROPE_THETA ** (-np.arange(0, AXIS_DIM, 2, dtype=np.float32) / AXIS_DIM)).astype(np.float32)
    ang = np.concatenate([row[:, None] * inv, col[:, None] * inv], -1).astype(np.float64)
    cos, sin = np.cos(ang), np.sin(ang)
    return (jnp.asarray(np.concatenate([cos, cos], -1), F32), jnp.asarray(np.concatenate([-sin, sin], -1), F32))


def kernel(x, c, ctx, c_ctx, w_mod, b_mod, w_in, q_norm, k_norm, b_conv, c_gate_w2, c_gate_b, c_norm,
           d_conv_w, d_conv_b, d_norm_g, d_norm_b, w_br, w_out, ln_g, ln_b):
    b, t, d = x.shape
    t_ctx = ctx.shape[1]
    depth = w_mod.shape[0]
    assert d == D_MODEL and b < 16
    alpha = (2 * depth) ** 0.25

    cc = jnp.concatenate([c, c_ctx[None], jnp.zeros((16 - b - 1, D_MODEL), F32)], axis=0)
    mod = _modulation(cc, w_mod, b_mod)
    cos, sin = _rope_tables(t)
    consts = _gla_constants()

    h_lat = x.reshape(b * t, D_MODEL)
    h_ctx = ctx.reshape(b * t_ctx, D_MODEL)
    for l in range(depth):
        want_ctx = l < depth - 1
        lw = _layer_weights(w_in[l], q_norm[l], k_norm[l], c_gate_w2[l], c_gate_b[l], c_norm[l])
        shift, scale, gate = [mod[l, :, n * D_MODEL:(n + 1) * D_MODEL].reshape(16, 1, D_MODEL) for n in range(3)]
        p_lat, r_lat = _in_projection(h_lat, shift, scale, t, lw["w_main"], lw["w_r"])
        ctx_tiles = None if want_ctx else tuple(c // PROJ_TN for c in CTX_KV_TILES)
        ctx_cols = FULL_COLS if want_ctx else CTX_KV_COLS
        p_ctx, r_ctx = _in_projection(h_ctx, shift[b:], scale[b:], b * t_ctx, lw["w_main"], lw["w_r"], ctx_tiles)
        p_lat3 = p_lat.reshape(b, t, N_MAIN)
        p_ctx3 = p_ctx.reshape(b, t_ctx, p_ctx.shape[1])
        r_lat3 = r_lat.reshape(b, t, LANES)
        r_ctx3 = r_ctx.reshape(b, t_ctx, LANES)
        w_br_l = w_br[l].astype(BF16)
        w_out_l = w_out[l].astype(BF16)

        ya_l = _attention(p_lat3, p_lat3, p_ctx3, lw["q_gain"], lw["k_gain"], cos, sin, ctx_cols)
        yb_l, yd_l = _local_branches(p_lat, t, b_conv[l], d_conv_w[l], d_conv_b[l], d_norm_g[l], d_norm_b[l])
        yc_l, yc_c = _gla(p_lat3, r_lat3, p_ctx3, r_ctx3, lw["w2"], lw["b2"], lw["c_gain"], consts, want_ctx,
                          ctx_cols)
        h_lat_new = _merge(p_lat, ya_l.reshape(b * t, BR_W), yb_l, yc_l.reshape(b * t, BR_W), yd_l,
                           h_lat, gate, t, w_br_l, w_out_l, ln_g[l], ln_b[l], alpha)
        if want_ctx:
            ya_c = _attention(p_ctx3, None, p_ctx3, lw["q_gain"], lw["k_gain"], None, None)
            yb_c, yd_c = _local_branches(p_ctx, t_ctx, b_conv[l], d_conv_w[l], d_conv_b[l], d_norm_g[l],
                                         d_norm_b[l])
            h_ctx = _merge(p_ctx, ya_c.reshape(b * t_ctx, BR_W), yb_c, yc_c.reshape(b * t_ctx, BR_W), yd_c,
                           h_ctx, gate[b:], b * t_ctx, w_br_l, w_out_l, ln_g[l], ln_b[l], alpha)
        h_lat = h_lat_new
    return h_lat.reshape(b, t, D_MODEL)
```

```python
import functools

import numpy as np
import jax
import jax.numpy as jnp
from jax import lax
from jax.experimental import pallas as pl
from jax.experimental.pallas import tpu as pltpu

F32 = jnp.float32
BF16 = jnp.bfloat16

D_MODEL = 1024
GRID_W = 64
N_BRANCH = 4
BR_W = D_MODEL // 2
HEAD_DIM = 128
A_HEADS = BR_W // HEAD_DIM
A_KV_HEADS = A_HEADS // 2
ROPE_THETA = 10000.0
AXIS_DIM = HEAD_DIM // 2
B_CONV = 3
C_HEADS = 4
C_HEAD_K = BR_W // (2 * C_HEADS)
C_HEAD_V = BR_W // C_HEADS
C_KEY_W = C_HEADS * C_HEAD_K
C_GATE_RANK = 16
C_GATE_TAU = 16.0
C_CHUNK = 64
D_CONV = 31
EPS = 1e-6
IN_WIDTHS = (
    A_HEADS * HEAD_DIM, A_KV_HEADS * HEAD_DIM, A_KV_HEADS * HEAD_DIM, BR_W,
    BR_W, BR_W, BR_W, BR_W,
    C_KEY_W, C_KEY_W, C_HEADS * C_HEAD_V, BR_W, 2 * C_GATE_RANK,
    2 * BR_W, BR_W,
    N_BRANCH * D_MODEL,
)

COL_AQ = 0
COL_AK = COL_AQ + 512
COL_AV = COL_AK + 256
COL_AZ = COL_AV + 256
COL_BG = COL_AZ + 512
COL_BC = COL_BG + 512
COL_BX = COL_BC + 512
COL_BZ = COL_BX + 512
COL_CQ = COL_BZ + 512
COL_CK = COL_CQ + 256
COL_CV = COL_CK + 256
COL_CZ = COL_CV + 512
COL_DA = COL_CZ + 512
COL_DG = COL_DA + 512
COL_DZ = COL_DG + 512
COL_MG = COL_DZ + 512
MG_BLK = 512
N_MAIN = COL_MG + N_BRANCH * D_MODEL
FULL_COLS = dict(ak=COL_AK, av=COL_AV, cq=COL_CQ, ck=COL_CK, cv=COL_CV, cz=COL_CZ)
CTX_KV_TILES = (COL_AK, COL_CQ, COL_CV)
CTX_KV_COLS = dict(ak=0, av=256, cq=512, ck=768, cv=1024, cz=1024)
LANES = 128
SUBLANES = 8
PROJ_TN = 512
LN_GROUP = 512
VMEM_LIMIT = 56 * 1024 * 1024
LOG2E = 1.4426950408889634


def _cparams(n_axes):
    return pltpu.CompilerParams(dimension_semantics=("arbitrary",) * n_axes,
                                vmem_limit_bytes=VMEM_LIMIT)


def _sigmoid(x):
    return 0.5 * jnp.tanh(0.5 * x) + 0.5


def _silu(x):
    h = 0.5 * x
    return h + h * jnp.tanh(h)


def _dot(a, b):
    return jnp.dot(a, b, preferred_element_type=F32)


def _dot_nt(a, b):
    return lax.dot_general(a, b, (((1,), (1,)), ((), ())), preferred_element_type=F32)


def _dot_tn(a, b):
    return lax.dot_general(a, b, (((0,), (0,)), ((), ())), preferred_element_type=F32)


def _mod_kernel(c_ref, w_ref, b_ref, o_ref):
    s = _silu(c_ref[...])
    o_ref[0] = _dot(s.astype(BF16), w_ref[0].astype(BF16)) + b_ref[0]


def _modulation(cc, w_mod, b_mod):
    depth = w_mod.shape[0]
    n_rows = cc.shape[0]
    return pl.pallas_call(
        _mod_kernel,
        grid=(depth, 3),
        in_specs=[
            pl.BlockSpec((n_rows, D_MODEL), lambda l, j: (0, 0)),
            pl.BlockSpec((1, D_MODEL, D_MODEL), lambda l, j: (l, 0, j)),
            pl.BlockSpec((1, 1, D_MODEL), lambda l, j: (l, 0, j)),
        ],
        out_specs=pl.BlockSpec((1, n_rows, D_MODEL), lambda l, j: (l, 0, j)),
        out_shape=jax.ShapeDtypeStruct((depth, n_rows, 3 * D_MODEL), F32),
        compiler_params=_cparams(2),
        name="modulation",
    )(cc, w_mod, b_mod.reshape(depth, 1, 3 * D_MODEL))


def _inproj_kernel(x_ref, shift_ref, scale_ref, w_ref, wr_ref, o_ref, or_ref, u_ref, *, tm, sub):
    j = pl.program_id(1)

    @pl.when(j == 0)
    def _():
        one_plus = 1.0 + scale_ref[0]
        shift = shift_ref[0]
        grp = min(LN_GROUP, tm)
        for g0 in range(0, tm, grp):
            for r0 in range(g0, g0 + grp, sub):
                x = x_ref[r0:r0 + sub, :]
                mu = jnp.mean(x, axis=-1, keepdims=True)
                xc = x - mu
                var = jnp.mean(xc * xc, axis=-1, keepdims=True)
                u = xc * lax.rsqrt(var + EPS) * one_plus + shift
                u_ref[r0:r0 + sub, :] = u.astype(BF16)
            u_grp = u_ref[g0:g0 + grp, :]
            or_ref[g0:g0 + grp, :] = _dot(u_grp, wr_ref[...]).astype(BF16)
            o_ref[g0:g0 + grp, :] = _dot(u_grp, w_ref[...]).astype(BF16)

    @pl.when(j != 0)
    def _():
        o_ref[...] = _dot(u_ref[...], w_ref[...]).astype(BF16)


def _in_projection(h, shift, scale, rows_per_mod, w_main, w_r, tiles=None):
    m = h.shape[0]
    tm = min(2048, rows_per_mod)
    assert m % tm == 0 and rows_per_mod % tm == 0
    tiles_per_mod = rows_per_mod // tm
    assert w_main.shape[1] % PROJ_TN == 0
    n_tiles = len(tiles) if tiles else w_main.shape[1] // PROJ_TN
    n_cols = n_tiles * PROJ_TN

    def w_tile(j):
        if not tiles:
            return j
        idx = tiles[0]
        for n in range(1, len(tiles)):
            idx = idx + (tiles[n] - tiles[n - 1]) * jnp.minimum(jnp.maximum(j - n + 1, 0), 1)
        return idx

    kern = functools.partial(_inproj_kernel, tm=tm, sub=min(128, tm))
    return pl.pallas_call(
        kern,
        grid=(m // tm, n_tiles),
        in_specs=[
            pl.BlockSpec((tm, D_MODEL), lambda i, j: (i, 0)),
            pl.BlockSpec((1, 1, D_MODEL), lambda i, j: (i // tiles_per_mod, 0, 0)),
            pl.BlockSpec((1, 1, D_MODEL), lambda i, j: (i // tiles_per_mod, 0, 0)),
            pl.BlockSpec((D_MODEL, PROJ_TN), lambda i, j: (0, w_tile(j))),
            pl.BlockSpec((D_MODEL, LANES), lambda i, j: (0, 0)),
        ],
        out_specs=[
            pl.BlockSpec((tm, PROJ_TN), lambda i, j: (i, j)),
            pl.BlockSpec((tm, LANES), lambda i, j: (i, 0)),
        ],
        out_shape=[
            jax.ShapeDtypeStruct((m, n_cols), BF16),
            jax.ShapeDtypeStruct((m, LANES), BF16),
        ],
        scratch_shapes=[pltpu.VMEM((tm, D_MODEL), BF16)],
        compiler_params=_cparams(2),
        name="in_projection",
    )(h, shift, scale, w_main, w_r)


ATTN_KEY_BLOCKS = (768, 512, 256)
QK_AHEAD = 2


def _rms(x):
    return x * lax.rsqrt(jnp.mean(x * x, axis=-1, keepdims=True) + EPS)


def _rope(x, cos, sin):
    lane = lax.broadcasted_iota(jnp.int32, x.shape, 1)
    swapped = jnp.where(lane % 2 == 0, pltpu.roll(x, HEAD_DIM - 1, 1), pltpu.roll(x, 1, 1))
    return x * cos + swapped * sin


def _attn_kernel(*refs, n_lat, n_ctx, tq, kblk, kstep):
    if n_lat:
        (q_ref, z_ref, kl_ref, vl_ref, kc_ref, vc_ref, qg_ref, kg_ref, cos_ref, sin_ref,
         o_ref, kn_ref, vn_ref) = refs
    else:
        q_ref, z_ref, kc_ref, vc_ref, qg_ref, kg_ref, o_ref, kn_ref, vn_ref = refs
    qi = pl.program_id(2)
    kg = kg_ref[...]

    @pl.when(qi == 0)
    def _prep():
        if n_lat:
            def body(t, carry):
                r0 = pl.multiple_of(t * kblk, kblk)
                kn = _rms(kl_ref[0, pl.ds(r0, kblk), :].astype(F32)) * kg
                kn = _rope(kn, cos_ref[pl.ds(r0, kblk), :], sin_ref[pl.ds(r0, kblk), :])
                kn_ref[pl.ds(r0, kblk), :] = kn.astype(BF16)
                vn_ref[pl.ds(r0, kblk), 0:HEAD_DIM] = vl_ref[0, pl.ds(r0, kblk), :]
                return carry

            lax.fori_loop(0, n_lat // kblk, body, 0)
        kn_ref[n_lat:n_lat + n_ctx, :] = (_rms(kc_ref[0].astype(F32)) * kg).astype(BF16)
        vn_ref[n_lat:n_lat + n_ctx, 0:HEAD_DIM] = vc_ref[0]
        vn_ref[:, HEAD_DIM:2 * HEAD_DIM] = jnp.ones((n_lat + n_ctx, HEAD_DIM), BF16)

    qg = qg_ref[...]
    if n_lat:
        q0 = pl.multiple_of(qi * tq, tq)
        cos = cos_ref[pl.ds(q0, tq), :]
        sin = sin_ref[pl.ds(q0, tq), :]
    qns = []
    for g in range(2):
        qn = _rms(q_ref[0, :, g * HEAD_DIM:(g + 1) * HEAD_DIM].astype(F32)) * qg
        if n_lat:
            qn = _rope(qn, cos, sin)
        qns.append((qn * (HEAD_DIM ** -0.5 * LOG2E)).astype(BF16))

    items = [(g, k0) for k0 in range(0, n_lat + n_ctx, kstep) for g in range(2)]

    def scores(item):
        g, k0 = item
        return _dot_nt(qns[g], kn_ref[k0:k0 + kstep, :])

    m = [None, None]
    acc = [None, None]
    ahead = [scores(item) for item in items[:QK_AHEAD]]
    for n, (g, k0) in enumerate(items):
        s = ahead.pop(0)
        if n + QK_AHEAD < len(items):
            ahead.append(scores(items[n + QK_AHEAD]))
        m_blk = jnp.max(s, axis=-1, keepdims=True)
        m_new = m_blk if m[g] is None else jnp.maximum(m[g], m_blk)
        p = jnp.exp2((s - m_new).astype(BF16))
        pv = _dot(p, vn_ref[k0:k0 + kstep, :])
        acc[g] = pv if acc[g] is None else acc[g] * jnp.exp2(m[g] - m_new) + pv
        m[g] = m_new
    for g in range(2):
        lanes = slice(g * HEAD_DIM, (g + 1) * HEAD_DIM)
        o = acc[g][:, 0:HEAD_DIM] / acc[g][:, HEAD_DIM:2 * HEAD_DIM]
        o_ref[0, :, lanes] = (o * _silu(z_ref[0, :, lanes].astype(F32))).astype(BF16)


def _attention(p_q, p_lat, p_ctx, q_gain, k_gain, cos, sin, ctx_cols=None):
    b, t_q, _ = p_q.shape
    n_ctx = p_ctx.shape[1]
    n_lat = 0 if p_lat is None else p_lat.shape[1]
    tq = min(256, t_q)
    assert t_q % tq == 0
    kblk = min(256, n_lat) if n_lat else 0
    n_keys = n_lat + n_ctx
    kstep = next((c for c in ATTN_KEY_BLOCKS if n_keys % c == 0), n_keys)
    hw = 2 * HEAD_DIM
    q_spec = pl.BlockSpec((1, tq, hw), lambda i, kv, qi: (i, qi, COL_AQ // hw + kv))
    z_spec = pl.BlockSpec((1, tq, hw), lambda i, kv, qi: (i, qi, COL_AZ // hw + kv))

    def kv_spec(n, col):
        return pl.BlockSpec((1, n, HEAD_DIM), lambda i, kv, qi: (i, 0, col // HEAD_DIM + kv))

    gain_spec = pl.BlockSpec((1, HEAD_DIM), lambda i, kv, qi: (0, 0))
    args = [p_q, p_q]
    specs = [q_spec, z_spec]
    if n_lat:
        args += [p_lat, p_lat]
        specs += [kv_spec(n_lat, COL_AK), kv_spec(n_lat, COL_AV)]
    ctx_cols = ctx_cols or FULL_COLS
    args += [p_ctx, p_ctx, q_gain, k_gain]
    specs += [kv_spec(n_ctx, ctx_cols["ak"]), kv_spec(n_ctx, ctx_cols["av"]), gain_spec, gain_spec]
    if n_lat:
        tab_spec = pl.BlockSpec((n_lat, HEAD_DIM), lambda i, kv, qi: (0, 0))
        args += [cos, sin]
        specs += [tab_spec, tab_spec]
    kern = functools.partial(_attn_kernel, n_lat=n_lat, n_ctx=n_ctx, tq=tq, kblk=kblk, kstep=kstep)
    return pl.pallas_call(
        kern,
        grid=(b, A_KV_HEADS, t_q // tq),
        in_specs=specs,
        out_specs=pl.BlockSpec((1, tq, hw), lambda i, kv, qi: (i, qi, kv)),
        out_shape=jax.ShapeDtypeStruct((b, t_q, BR_W), BF16),
        scratch_shapes=[pltpu.VMEM((n_lat + n_ctx, HEAD_DIM), BF16),
                        pltpu.VMEM((n_lat + n_ctx, 2 * HEAD_DIM), BF16)],
        compiler_params=_cparams(3),
        name="attention_lat" if n_lat else "attention_ctx",
    )(*args)


D_HALO = 16
B_HALO = 16
CONV_RB = 32
CONV_ACCS = 4
ELEM_RB = 32
COPY_RB = 56


def _local_kernel(bg_ref, bc_ref, bcp_ref, bcn_ref, bx_ref, bxp_ref, bxn_ref, bz_ref,
                  da_ref, dap_ref, dan_ref, dg_ref, dgp_ref, dgn_ref, dz_ref,
                  bw_ref, dw_ref, db_ref, dgain_ref, dbeta_ref,
                  yb_ref, yd_ref, tbuf, gbuf, hbuf, *, tt, tiles_per_seq):
    i = pl.program_id(0)
    pos = i % tiles_per_seq
    keep_prev = (pos != 0).astype(F32)
    keep_next = (pos != tiles_per_seq - 1).astype(F32)

    def ld(ref, r0=0, n=None):
        return ref[r0:r0 + (n or ref.shape[0]), :].astype(F32)

    row_blocks = range(0, tt, ELEM_RB)

    tbuf[0:B_HALO, :] = ld(bcp_ref) * ld(bxp_ref) * keep_prev
    for r0 in row_blocks:
        tbuf[B_HALO + r0:B_HALO + r0 + ELEM_RB, :] = ld(bc_ref, r0, ELEM_RB) * ld(bx_ref, r0, ELEM_RB)
    tbuf[B_HALO + tt:2 * B_HALO + tt, :] = ld(bcn_ref) * ld(bxn_ref) * keep_next
    for r0 in row_blocks:
        lo = B_HALO + r0
        conv = (bw_ref[0:1, :] * tbuf[lo - 1:lo - 1 + ELEM_RB, :] + bw_ref[1:2, :] * tbuf[lo:lo + ELEM_RB, :]
                + bw_ref[2:3, :] * tbuf[lo + 1:lo + 1 + ELEM_RB, :])
        yb_ref[r0:r0 + ELEM_RB, :] = (ld(bg_ref, r0, ELEM_RB) * conv
                                      * _silu(ld(bz_ref, r0, ELEM_RB))).astype(BF16)

    gbuf[0, 0:D_HALO, :] = ld(dap_ref) * _sigmoid(ld(dgp_ref)) * keep_prev
    for r0 in row_blocks:
        gbuf[0, D_HALO + r0:D_HALO + r0 + ELEM_RB, :] = (ld(da_ref, r0, ELEM_RB)
                                                          * _sigmoid(ld(dg_ref, r0, ELEM_RB)))
    gbuf[0, D_HALO + tt:2 * D_HALO + tt, :] = ld(dan_ref) * _sigmoid(ld(dgn_ref)) * keep_next
    span = tt + 2 * D_HALO - SUBLANES
    for r in range(1, SUBLANES):
        for x0 in range(0, span, COPY_RB):
            n = min(COPY_RB, span - x0)
            gbuf[r, x0:x0 + n, :] = gbuf[0, r + x0:r + x0 + n, :]
    base = D_HALO - D_CONV // 2
    for cb in range(BR_W // LANES):
        lanes = slice(cb * LANES, (cb + 1) * LANES)
        for rb in range(tt // CONV_RB):
            accs = [None] * CONV_ACCS
            for k in range(D_CONV):
                off = base + k
                r0 = rb * CONV_RB + off - off % SUBLANES
                rows = gbuf[off % SUBLANES, r0:r0 + CONV_RB, lanes]
                term = rows.reshape(CONV_RB // SUBLANES, SUBLANES, LANES) * dw_ref[k, :, lanes][None]
                a = k % CONV_ACCS
                accs[a] = term if accs[a] is None else accs[a] + term
            while len(accs) > 1:
                accs = [accs[n] + accs[n + 1] for n in range(0, len(accs), 2)]
            hbuf[rb * CONV_RB:(rb + 1) * CONV_RB, lanes] = accs[0].reshape(CONV_RB, LANES)
    for r0 in row_blocks:
        hh = hbuf[r0:r0 + ELEM_RB, :] + db_ref[...]
        mu = jnp.mean(hh, axis=-1, keepdims=True)
        hc = hh - mu
        var = jnp.mean(hc * hc, axis=-1, keepdims=True)
        hn = hc * lax.rsqrt(var + EPS) * dgain_ref[...] + dbeta_ref[...]
        yd_ref[r0:r0 + ELEM_RB, :] = (_silu(hn) * _silu(ld(dz_ref, r0, ELEM_RB))).astype(BF16)


def _local_branches(p2, seq_len, b_w, d_w, d_b, d_g, d_beta):
    m = p2.shape[0]
    tt = min(256, seq_len)
    assert seq_len % tt == 0 and m % seq_len == 0
    tiles_per_seq = seq_len // tt
    n_tiles = m // tt

    def cur(col):
        return pl.BlockSpec((tt, BR_W), lambda i: (i, col // BR_W))

    def prev(col, halo):
        per = tt // halo
        return pl.BlockSpec((halo, BR_W), lambda i: (jnp.maximum(i * per - 1, 0), col // BR_W))

    def nxt(col, halo):
        per = tt // halo
        last = m // halo - 1
        return pl.BlockSpec((halo, BR_W), lambda i: (jnp.minimum((i + 1) * per, last), col // BR_W))

    def small(rows):
        return pl.BlockSpec((rows, BR_W), lambda i: (0, 0))

    specs = [cur(COL_BG),
             cur(COL_BC), prev(COL_BC, B_HALO), nxt(COL_BC, B_HALO),
             cur(COL_BX), prev(COL_BX, B_HALO), nxt(COL_BX, B_HALO),
             cur(COL_BZ),
             cur(COL_DA), prev(COL_DA, D_HALO), nxt(COL_DA, D_HALO),
             cur(COL_DG), prev(COL_DG, D_HALO), nxt(COL_DG, D_HALO),
             cur(COL_DZ),
             small(B_CONV), pl.BlockSpec((D_CONV, SUBLANES, BR_W), lambda i: (0, 0, 0)),
             small(1), small(1), small(1)]
    kern = functools.partial(_local_kernel, tt=tt, tiles_per_seq=tiles_per_seq)
    out_spec = pl.BlockSpec((tt, BR_W), lambda i: (i, 0))
    return pl.pallas_call(
        kern,
        grid=(n_tiles,),
        in_specs=specs,
        out_specs=[out_spec, out_spec],
        out_shape=[jax.ShapeDtypeStruct((m, BR_W), BF16), jax.ShapeDtypeStruct((m, BR_W), BF16)],
        scratch_shapes=[pltpu.VMEM((tt + 2 * B_HALO, BR_W), F32),
                        pltpu.VMEM((SUBLANES, tt + 2 * D_HALO, BR_W), F32),
                        pltpu.VMEM((tt, BR_W), F32)],
        compiler_params=_cparams(1),
        name="local_branches",
    )(*([p2] * 15), b_w, jnp.broadcast_to(d_w[:, None, :], (D_CONV, SUBLANES, BR_W)),
      d_b.reshape(1, BR_W), d_g.reshape(1, BR_W), d_beta.reshape(1, BR_W))


CH = C_CHUNK
PAIR_K = 2 * C_HEAD_K
PAIR_V = 2 * C_HEAD_V
LEVELS = (8, 4, 2)
BLK = SUBLANES
N_BLK = CH // BLK
GLA_UNROLL = 4
ROW_EQ = 0
ROW_EK = CH
ROW_LAST = 2 * CH
ROW_FK = 2 * CH + 8
ROW_LVL = ROW_FK + CH
ROW_FQ = ROW_LVL + len(LEVELS) * CH
N_FQ = BLK * (N_BLK * (N_BLK - 1) // 2)
N_EROWS = ROW_FQ + N_FQ


def _slabs(reverse):
    if reverse:
        return [(jb, 0, BLK * jb) for jb in range(1, N_BLK)]
    return [(jb, BLK * (jb + 1), CH - BLK * (jb + 1)) for jb in range(N_BLK - 1)]


def _gla_constants():
    idx = np.arange(CH)
    cols = np.arange(2 * CH) % CH
    emats, masks = [], []
    for reverse in (False, True):
        tri = (idx[None, :] >= idx[:, None]) if reverse else (idx[None, :] <= idx[:, None])
        tri = tri.astype(np.float32)
        edge = tri[0] if reverse else tri[CH - 1]
        blk_edge = (idx // BLK) * BLK + (0 if reverse else BLK - 1)
        blocks = [tri, edge[None, :] - tri, np.tile(edge[None, :], (8, 1)), tri[blk_edge] - tri]
        lvl_masks = []
        for grp in LEVELS:
            half = grp // 2
            in_q = (idx % grp < half) if reverse else (idx % grp >= half)
            ref = (idx // grp) * grp + (half if reverse else half - 1)
            d = tri - tri[ref]
            blocks.append(np.where(in_q[:, None], d, -d))
            in_q_col = (cols % grp < half) if reverse else (cols % grp >= half)
            lvl_masks.append((idx[:, None] // grp == cols[None, :] // grp) & in_q[:, None] & ~in_q_col[None, :])
        lvl_masks.append(idx[:, None] == cols[None, :])
        for jb, q0, nq in _slabs(reverse):
            edge_row = BLK * jb + (0 if reverse else BLK - 1)
            blocks.append(tri[q0:q0 + nq] - tri[edge_row][None, :])
        emat = np.concatenate(blocks, axis=0)
        assert emat.shape == (N_EROWS, CH) and emat.min() >= -1 and emat.max() <= 1
        emats.append(np.concatenate([emat] * 3, axis=1))
        masks.append(np.stack(lvl_masks).astype(np.float32))
    return (jnp.asarray(emats[0], BF16), jnp.asarray(emats[1], BF16), jnp.asarray(np.stack(masks), F32))


def _split3(g):
    g1 = g.astype(BF16)
    r1 = g - g1.astype(F32)
    g2 = r1.astype(BF16)
    g3 = (r1 - g2.astype(F32)).astype(BF16)
    return g1, g2, g3


def _log2_sigmoid(x):
    e = jnp.exp2(jnp.abs(x) * -LOG2E)
    return jnp.minimum(x, 0.0) * LOG2E - jnp.log2(1.0 + e)


def _gla_kernel(*refs, t_lat, t_ctx, want_ctx):
    (ql_ref, kl_ref, vl_ref, zl_ref, rl_ref, qc_ref, kc_ref, vc_ref, zc_ref, rc_ref,
     w2_ref, b2_ref, gain_ref, ematf_ref, ematb_ref, masks_ref) = refs[:16]
    if want_ctx:
        yl_ref, yc_ref = refs[16:18]
        scratch = refs[18:]
    else:
        yl_ref = refs[16]
        scratch = refs[17:]
        yc_ref = zc_ref = None
    stf_ref, stb_ref, of_ref, ob_ref, g1_ref, g2_ref, g3_ref = scratch
    unroll = GLA_UNROLL

    def gates(r_ref, n_rows):
        blk = min(256, n_rows)

        def body(t, carry):
            rows = pl.ds(pl.multiple_of(t * blk, blk), blk)
            x = _dot(r_ref[0, rows, :], w2_ref[...]) + b2_ref[...]
            g1, g2, g3 = _split3(_log2_sigmoid(x) * (1.0 / C_GATE_TAU))
            g1_ref[rows, :] = g1
            g2_ref[rows, :] = g2
            g3_ref[rows, :] = g3
            return carry

        lax.fori_loop(0, n_rows // blk, body, 0)

    def run(q_ref, k_ref, v_ref, n_chunks, want_out):
        assert n_chunks % unroll == 0
        lane_k = lax.broadcasted_iota(jnp.int32, (CH, PAIR_K), 1)
        head0 = lane_k < C_HEAD_K
        zero_v = jnp.zeros((CH, C_HEAD_V), BF16)
        key_blk = (lax.broadcasted_iota(jnp.int32, (BLK, 2 * CH), 1) % CH) // BLK

        def body(i, carry):
            work = []
            for reverse in (False, True):
                lanes = slice(PAIR_K, 2 * PAIR_K) if reverse else slice(0, PAIR_K)
                rows = []
                for u in range(unroll):
                    c = i * unroll + u
                    c = (n_chunks - 1 - c) if reverse else c
                    rows.append(pl.ds(pl.multiple_of(c * CH, CH), CH))
                rhs = jnp.concatenate(
                    [jnp.concatenate([g_ref[r, lanes] for r in rows], axis=1)
                     for g_ref in (g1_ref, g2_ref, g3_ref)], axis=0)
                emat = ematb_ref[...] if reverse else ematf_ref[...]
                e_all = jnp.exp2(_dot(emat, rhs))
                for u in range(unroll):
                    work.append(dict(reverse=reverse, rows=rows[u], u=u,
                                     e=e_all[:, u * PAIR_K:(u + 1) * PAIR_K]))
            work.sort(key=lambda w: w["u"])

            for w in work:
                st_ref = stb_ref if w["reverse"] else stf_ref
                e = w["e"]
                q = q_ref[0, w["rows"], :].astype(F32) * C_HEAD_K ** -0.5
                k = k_ref[0, w["rows"], :].astype(F32)
                v = v_ref[0, w["rows"], :]
                k0 = jnp.where(head0, k, 0.0)
                k1 = jnp.where(head0, 0.0, k)
                st = st_ref[...]
                e_k = e[ROW_EK:ROW_EK + CH]
                k_dec = jnp.concatenate([k0 * e_k, k1 * e_k], axis=0).astype(BF16)
                v_rows = jnp.concatenate([v[:, :C_HEAD_V], v[:, C_HEAD_V:]], axis=0)
                st_ref[...] = e[ROW_LAST:ROW_LAST + 1] * st + _dot_tn(v_rows, k_dec)
                if want_out:
                    qe = q * e[ROW_EQ:ROW_EQ + CH]
                    q_rows = jnp.concatenate([jnp.where(head0, qe, 0.0), jnp.where(head0, 0.0, qe)], axis=0)
                    o_st = _dot(q_rows.astype(BF16), st.T.astype(BF16))
                    w.update(q=q, k0=k0, k1=k1, v=v, o=jnp.concatenate([o_st[:CH], o_st[CH:]], axis=1))
            if not want_out:
                return carry

            for w in work:
                masks = masks_ref.at[1 if w["reverse"] else 0]
                q, k0, k1, e = w["q"], w["k0"], w["k1"], w["e"]
                fk = e[ROW_FK:ROW_FK + CH]
                kk_t = jnp.concatenate([k0 * fk, k1 * fk], axis=0).T.astype(BF16)
                slabs = _slabs(w["reverse"])
                lhs, off = [], ROW_FQ
                for jb, q0, nq in slabs:
                    lhs.append(q[q0:q0 + nq] * e[off:off + nq])
                    off += nq
                r = _dot(jnp.concatenate(lhs, axis=0).astype(BF16), kk_t)
                rows, off = [jnp.zeros((BLK, 2 * CH), F32)] * N_BLK, 0
                for jb, q0, nq in slabs:
                    for ib in range(q0 // BLK, (q0 + nq) // BLK):
                        piece = r[off + ib * BLK - q0:off + (ib + 1) * BLK - q0]
                        rows[ib] = jnp.where(key_blk == jb, piece, rows[ib])
                    off += nq
                scores = jnp.concatenate(rows, axis=0)
                k_heads = jnp.concatenate([k0, k1], axis=0)
                scores = scores + _dot(q.astype(BF16), k_heads.T.astype(BF16)) * masks[len(LEVELS)]
                for lvl in range(len(LEVELS)):
                    e_l = e[ROW_LVL + lvl * CH:ROW_LVL + (lvl + 1) * CH]
                    kk_t = jnp.concatenate([k0 * e_l, k1 * e_l], axis=0).T.astype(BF16)
                    scores = scores + _dot((q * e_l).astype(BF16), kk_t) * masks[lvl]
                w["scores"] = scores.astype(BF16)

            for w in work:
                v = w["v"]
                v_bd = jnp.concatenate([jnp.concatenate([v[:, :C_HEAD_V], zero_v], axis=1),
                                        jnp.concatenate([zero_v, v[:, C_HEAD_V:]], axis=1)], axis=0)
                (ob_ref if w["reverse"] else of_ref)[w["rows"], :] = w["o"] + _dot(w["scores"], v_bd)
            return carry

        lax.fori_loop(0, n_chunks // unroll, body, 0)

    def finish(z_ref, y_ref, n_rows):
        blk = min(256, n_rows)

        def body(t, carry):
            rows = pl.ds(pl.multiple_of(t * blk, blk), blk)
            o = of_ref[rows, :] + ob_ref[rows, :]
            z = z_ref[0, rows, :].astype(F32)
            gain = gain_ref[...]
            for h in range(2):
                lanes = slice(h * C_HEAD_V, (h + 1) * C_HEAD_V)
                y = _rms(o[:, lanes]) * gain[:, lanes]
                y_ref[0, rows, lanes] = (y * _silu(z[:, lanes])).astype(BF16)
            return carry

        lax.fori_loop(0, n_rows // blk, body, 0)

    stf_ref[...] = jnp.zeros_like(stf_ref)
    stb_ref[...] = jnp.zeros_like(stb_ref)
    gates(rc_ref, t_ctx)
    run(qc_ref, kc_ref, vc_ref, t_ctx // CH, want_ctx)
    if want_ctx:
        finish(zc_ref, yc_ref, t_ctx)
    gates(rl_ref, t_lat)
    run(ql_ref, kl_ref, vl_ref, t_lat // CH, True)
    finish(zl_ref, yl_ref, t_lat)


def _gla(p_lat, r_lat, p_ctx, r_ctx, w2, b2, gain, consts, want_ctx, ctx_cols=None):
    b, t_lat, _ = p_lat.shape
    t_ctx = p_ctx.shape[1]
    n_pairs = C_HEADS // 2

    def side(t, with_z, cols):
        z_rows = t if with_z else 16
        return [pl.BlockSpec((1, t, PAIR_K), lambda i, hp: (i, 0, cols["cq"] // PAIR_K + hp)),
                pl.BlockSpec((1, t, PAIR_K), lambda i, hp: (i, 0, cols["ck"] // PAIR_K + hp)),
                pl.BlockSpec((1, t, PAIR_V), lambda i, hp: (i, 0, cols["cv"] // PAIR_V + hp)),
                pl.BlockSpec((1, z_rows, PAIR_V), lambda i, hp: (i, 0, cols["cz"] // PAIR_V + hp)),
                pl.BlockSpec((1, t, LANES), lambda i, hp: (i, 0, 0))]

    def whole(shape):
        nd = len(shape)
        return pl.BlockSpec(shape, lambda i, hp: (0,) * nd)

    emat_f, emat_b, masks = consts
    specs = side(t_lat, True, FULL_COLS) + side(t_ctx, want_ctx, ctx_cols or FULL_COLS) + [
        pl.BlockSpec((1, LANES, 2 * PAIR_K), lambda i, hp: (hp, 0, 0)),
        pl.BlockSpec((1, 1, 2 * PAIR_K), lambda i, hp: (hp, 0, 0)),
        pl.BlockSpec((1, 1, PAIR_V), lambda i, hp: (hp, 0, 0)),
        whole(emat_f.shape), whole(emat_b.shape), whole(masks.shape)]
    out_specs = [pl.BlockSpec((1, t_lat, PAIR_V), lambda i, hp: (i, 0, hp))]
    out_shape = [jax.ShapeDtypeStruct((b, t_lat, BR_W), BF16)]
    if want_ctx:
        out_specs.append(pl.BlockSpec((1, t_ctx, PAIR_V), lambda i, hp: (i, 0, hp)))
        out_shape.append(jax.ShapeDtypeStruct((b, t_ctx, BR_W), BF16))

    def kern(*refs):
        refs = list(refs)
        for n in range(10, 13):
            refs[n] = refs[n].at[0]
        _gla_kernel(*refs, t_lat=t_lat, t_ctx=t_ctx, want_ctx=want_ctx)

    res = pl.pallas_call(
        kern,
        grid=(b, n_pairs),
        in_specs=specs,
        out_specs=out_specs,
        out_shape=out_shape,
        scratch_shapes=[pltpu.VMEM((C_HEAD_V, PAIR_K), F32), pltpu.VMEM((C_HEAD_V, PAIR_K), F32),
                        pltpu.VMEM((t_lat, PAIR_V), F32), pltpu.VMEM((t_lat, PAIR_V), F32),
                        pltpu.VMEM((t_lat, 2 * PAIR_K), BF16), pltpu.VMEM((t_lat, 2 * PAIR_K), BF16),
                        pltpu.VMEM((t_lat, 2 * PAIR_K), BF16)],
        compiler_params=_cparams(2),
        name="gla",
    )(p_lat, p_lat, p_lat, p_lat, r_lat, p_ctx, p_ctx, p_ctx, p_ctx, r_ctx,
      w2, b2, gain, emat_f, emat_b, masks)
    return (res[0], res[1]) if want_ctx else (res[0], None)


def _merge_kernel(*refs, alpha):
    n_mg = N_BRANCH * D_MODEL // MG_BLK
    mg_refs = refs[:n_mg]
    ya_ref, yb_ref, yc_ref, yd_ref, h_ref, gate_ref, wbr_ref, wout_ref, lng_ref, lnb_ref, o_ref = refs[n_mg:]
    per = D_MODEL // MG_BLK
    acc = None
    for n, y_ref in enumerate((ya_ref, yb_ref, yc_ref, yd_ref)):
        mg = jnp.concatenate([r[...] for r in mg_refs[n * per:(n + 1) * per]], axis=1)
        gate = _sigmoid(mg.astype(F32))
        term = gate * _dot(y_ref[...], wbr_ref[n])
        acc = term if acc is None else acc + term
    y = _dot(acc.astype(BF16), wout_ref[...])
    t = alpha * h_ref[...] + gate_ref[0] * y
    mu = jnp.mean(t, axis=-1, keepdims=True)
    tc = t - mu
    var = jnp.mean(tc * tc, axis=-1, keepdims=True)
    o_ref[...] = tc * lax.rsqrt(var + EPS) * lng_ref[...] + lnb_ref[...]


def _merge(p2, ya, yb, yc, yd, h, gate, rows_per_mod, w_br, w_out, ln_g, ln_b, alpha):
    m = h.shape[0]
    tm = min(512, m)
    assert m % tm == 0 and rows_per_mod % tm == 0
    tiles_per_mod = rows_per_mod // tm
    n_mg = N_BRANCH * D_MODEL // MG_BLK
    br_spec = pl.BlockSpec((tm, BR_W), lambda i: (i, 0))
    vec_spec = pl.BlockSpec((1, D_MODEL), lambda i: (0, 0))
    return pl.pallas_call(
        functools.partial(_merge_kernel, alpha=alpha),
        grid=(m // tm,),
        in_specs=[pl.BlockSpec((tm, MG_BLK), functools.partial(lambda n, i: (i, COL_MG // MG_BLK + n), n))
                  for n in range(n_mg)] + [
            br_spec, br_spec, br_spec, br_spec,
            pl.BlockSpec((tm, D_MODEL), lambda i: (i, 0)),
            pl.BlockSpec((1, 1, D_MODEL), lambda i: (i // tiles_per_mod, 0, 0)),
            pl.BlockSpec((N_BRANCH, BR_W, D_MODEL), lambda i: (0, 0, 0)),
            pl.BlockSpec((D_MODEL, D_MODEL), lambda i: (0, 0)),
            vec_spec, vec_spec,
        ],
        out_specs=pl.BlockSpec((tm, D_MODEL), lambda i: (i, 0)),
        out_shape=jax.ShapeDtypeStruct((m, D_MODEL), F32),
        compiler_params=_cparams(1),
        name="merge",
    )(*([p2] * n_mg), ya, yb, yc, yd, h, gate, w_br, w_out, ln_g.reshape(1, D_MODEL), ln_b.reshape(1, D_MODEL))


def _layer_weights(w_in_l, q_norm_l, k_norm_l, c_w2_l, c_b2_l, c_norm_l):
    r0 = int(sum(IN_WIDTHS[:12]))
    r1 = r0 + 2 * C_GATE_RANK
    w_main = jnp.concatenate([w_in_l[:, :r0], w_in_l[:, r1:]], axis=-1).astype(BF16)
    w_r = jnp.pad(w_in_l[:, r0:r1], ((0, 0), (0, LANES - 2 * C_GATE_RANK))).astype(BF16)
    n_pairs = C_HEADS // 2

    def gate_w(i):
        w = jnp.pad(c_w2_l[i], ((i * C_GATE_RANK, LANES - (i + 1) * C_GATE_RANK), (0, 0)))
        return w.reshape(LANES, n_pairs, PAIR_K).transpose(1, 0, 2)

    w2 = jnp.concatenate([gate_w(0), gate_w(1)], axis=-1).astype(BF16)
    b2 = jnp.concatenate([c_b2_l[0].reshape(n_pairs, 1, PAIR_K), c_b2_l[1].reshape(n_pairs, 1, PAIR_K)], axis=-1)
    return dict(
        w_main=w_main, w_r=w_r,
        q_gain=q_norm_l.reshape(1, HEAD_DIM), k_gain=k_norm_l.reshape(1, HEAD_DIM),
        w2=w2, b2=b2, c_gain=c_norm_l.reshape(n_pairs, 1, PAIR_V))


def _rope_tables(t):
    rows = t // GRID_W
    row = np.repeat(np.arange(rows), GRID_W).astype(np.float32)
    col = np.tile(np.arange(GRID_W), rows).astype(np.float32)
    inv = (ROPE_THETA ** (-np.arange(0, AXIS_DIM, 2, dtype=np.float32) / AXIS_DIM)).astype(np.float32)
    ang = np.concatenate([row[:, None] * inv, col[:, None] * inv], -1).astype(np.float64)
    cos, sin = np.repeat(np.cos(ang), 2, axis=-1), np.repeat(np.sin(ang), 2, axis=-1)
    sin[:, 0::2] *= -1.0
    return jnp.asarray(cos, F32), jnp.asarray(sin, F32)


def kernel(x, c, ctx, c_ctx, w_mod, b_mod, w_in, q_norm, k_norm, b_conv, c_gate_w2, c_gate_b, c_norm,
           d_conv_w, d_conv_b, d_norm_g, d_norm_b, w_br, w_out, ln_g, ln_b):
    b, t, d = x.shape
    t_ctx = ctx.shape[1]
    depth = w_mod.shape[0]
    assert d == D_MODEL and b < 16
    alpha = (2 * depth) ** 0.25

    cc = jnp.concatenate([c, c_ctx[None], jnp.zeros((16 - b - 1, D_MODEL), F32)], axis=0)
    mod = _modulation(cc, w_mod, b_mod)
    cos, sin = _rope_tables(t)
    consts = _gla_constants()

    h_lat = x.reshape(b * t, D_MODEL)
    h_ctx = ctx.reshape(b * t_ctx, D_MODEL)
    for l in range(depth):
        want_ctx = l < depth - 1
        lw = _layer_weights(w_in[l], q_norm[l], k_norm[l], c_gate_w2[l], c_gate_b[l], c_norm[l])
        shift, scale, gate = [mod[l, :, n * D_MODEL:(n + 1) * D_MODEL].reshape(16, 1, D_MODEL) for n in range(3)]
        p_lat, r_lat = _in_projection(h_lat, shift, scale, t, lw["w_main"], lw["w_r"])
        ctx_tiles = None if want_ctx else tuple(c // PROJ_TN for c in CTX_KV_TILES)
        ctx_cols = FULL_COLS if want_ctx else CTX_KV_COLS
        p_ctx, r_ctx = _in_projection(h_ctx, shift[b:], scale[b:], b * t_ctx, lw["w_main"], lw["w_r"], ctx_tiles)
        p_lat3 = p_lat.reshape(b, t, N_MAIN)
        p_ctx3 = p_ctx.reshape(b, t_ctx, p_ctx.shape[1])
        r_lat3 = r_lat.reshape(b, t, LANES)
        r_ctx3 = r_ctx.reshape(b, t_ctx, LANES)
        w_br_l = w_br[l].astype(BF16)
        w_out_l = w_out[l].astype(BF16)

        ya_l = _attention(p_lat3, p_lat3, p_ctx3, lw["q_gain"], lw["k_gain"], cos, sin, ctx_cols)
        yb_l, yd_l = _local_branches(p_lat, t, b_conv[l], d_conv_w[l], d_conv_b[l], d_norm_g[l], d_norm_b[l])
        yc_l, yc_c = _gla(p_lat3, r_lat3, p_ctx3, r_ctx3, lw["w2"], lw["b2"], lw["c_gain"], consts, want_ctx,
                          ctx_cols)
        h_lat_new = _merge(p_lat, ya_l.reshape(b * t, BR_W), yb_l, yc_l.reshape(b * t, BR_W), yd_l,
                           h_lat, gate, t, w_br_l, w_out_l, ln_g[l], ln_b[l], alpha)
        if want_ctx:
            ya_c = _attention(p_ctx3, None, p_ctx3, lw["q_gain"], lw["k_gain"], None, None)
            yb_c, yd_c = _local_branches(p_ctx, t_ctx, b_conv[l], d_conv_w[l], d_conv_b[l], d_norm_g[l],
                                         d_norm_b[l])
            h_ctx = _merge(p_ctx, ya_c.reshape(b * t_ctx, BR_W), yb_c, yc_c.reshape(b * t_ctx, BR_W), yd_c,
                           h_ctx, gate[b:], b * t_ctx, w_br_l, w_out_l, ln_g[l], ln_b[l], alpha)
        h_lat = h_lat_new
    return h_lat.reshape(b, t, D_MODEL)
```

```python
import functools

import numpy as np
import jax
import jax.numpy as jnp
from jax import lax
from jax.experimental import pallas as pl
from jax.experimental.pallas import tpu as pltpu

F32 = jnp.float32
BF16 = jnp.bfloat16

D_MODEL = 1024
GRID_W = 64
N_BRANCH = 4
BR_W = D_MODEL // 2
HEAD_DIM = 128
A_HEADS = BR_W // HEAD_DIM
A_KV_HEADS = A_HEADS // 2
ROPE_THETA = 10000.0
AXIS_DIM = HEAD_DIM // 2
B_CONV = 3
C_HEADS = 4
C_HEAD_K = BR_W // (2 * C_HEADS)
C_HEAD_V = BR_W // C_HEADS
C_KEY_W = C_HEADS * C_HEAD_K
C_GATE_RANK = 16
C_GATE_TAU = 16.0
C_CHUNK = 64
D_CONV = 31
EPS = 1e-6
IN_WIDTHS = (
    A_HEADS * HEAD_DIM, A_KV_HEADS * HEAD_DIM, A_KV_HEADS * HEAD_DIM, BR_W,
    BR_W, BR_W, BR_W, BR_W,
    C_KEY_W, C_KEY_W, C_HEADS * C_HEAD_V, BR_W, 2 * C_GATE_RANK,
    2 * BR_W, BR_W,
    N_BRANCH * D_MODEL,
)

COL_AQ = 0
COL_AK = COL_AQ + 512
COL_AV = COL_AK + 256
COL_AZ = COL_AV + 256
COL_BG = COL_AZ + 512
COL_BC = COL_BG + 512
COL_BX = COL_BC + 512
COL_BZ = COL_BX + 512
COL_CQ = COL_BZ + 512
COL_CK = COL_CQ + 256
COL_CV = COL_CK + 256
COL_CZ = COL_CV + 512
COL_DA = COL_CZ + 512
COL_DG = COL_DA + 512
COL_DZ = COL_DG + 512
COL_MG = COL_DZ + 512
MG_BLK = 512
N_HEAD_TILES = COL_DA // 512
N_MAIN = COL_MG + N_BRANCH * D_MODEL
FULL_COLS = dict(ak=COL_AK, av=COL_AV, cq=COL_CQ, ck=COL_CK, cv=COL_CV, cz=COL_CZ)
CTX_KV_TILES = (COL_AK, COL_CQ, COL_CV)
CTX_KV_COLS = dict(ak=0, av=256, cq=512, ck=768, cv=1024, cz=1024)
LANES = 128
SUBLANES = 8
PROJ_TN = 512
LN_GROUP = 512
VMEM_LIMIT = 56 * 1024 * 1024
LOG2E = 1.4426950408889634


def _cparams(n_axes):
    return pltpu.CompilerParams(dimension_semantics=("arbitrary",) * n_axes,
                                vmem_limit_bytes=VMEM_LIMIT)


def _sigmoid(x):
    return 0.5 * jnp.tanh(0.5 * x) + 0.5


def _silu(x):
    h = 0.5 * x
    return h + h * jnp.tanh(h)


def _dot(a, b):
    return jnp.dot(a, b, preferred_element_type=F32)


def _dot_nt(a, b):
    return lax.dot_general(a, b, (((1,), (1,)), ((), ())), preferred_element_type=F32)


def _dot_tn(a, b):
    return lax.dot_general(a, b, (((0,), (0,)), ((), ())), preferred_element_type=F32)


def _mod_kernel(c_ref, w_ref, b_ref, o_ref):
    s = _silu(c_ref[...])
    o_ref[0] = _dot(s.astype(BF16), w_ref[0].astype(BF16)) + b_ref[0]


def _modulation(cc, w_mod, b_mod):
    depth = w_mod.shape[0]
    n_rows = cc.shape[0]
    return pl.pallas_call(
        _mod_kernel,
        grid=(depth, 3),
        in_specs=[
            pl.BlockSpec((n_rows, D_MODEL), lambda l, j: (0, 0)),
            pl.BlockSpec((1, D_MODEL, D_MODEL), lambda l, j: (l, 0, j)),
            pl.BlockSpec((1, 1, D_MODEL), lambda l, j: (l, 0, j)),
        ],
        out_specs=pl.BlockSpec((1, n_rows, D_MODEL), lambda l, j: (l, 0, j)),
        out_shape=jax.ShapeDtypeStruct((depth, n_rows, 3 * D_MODEL), F32),
        compiler_params=_cparams(2),
        name="modulation",
    )(cc, w_mod, b_mod.reshape(depth, 1, 3 * D_MODEL))


def _inproj_kernel(x_ref, shift_ref, scale_ref, wa_ref, wb_ref, wr_ref, o_ref, or_ref, u_ref, *,
                   tm, sub, n_a_steps, n_steps):
    j = pl.program_id(1)

    @pl.when(j == 0)
    def _():
        one_plus = 1.0 + scale_ref[0]
        shift = shift_ref[0]
        w = wa_ref[...].astype(BF16)
        grp = min(LN_GROUP, tm)
        for g0 in range(0, tm, grp):
            for r0 in range(g0, g0 + grp, sub):
                x = x_ref[r0:r0 + sub, :]
                mu = jnp.mean(x, axis=-1, keepdims=True)
                xc = x - mu
                var = jnp.mean(xc * xc, axis=-1, keepdims=True)
                u = xc * lax.rsqrt(var + EPS) * one_plus + shift
                u_ref[r0:r0 + sub, :] = u.astype(BF16)
            u_grp = u_ref[g0:g0 + grp, :]
            or_ref[g0:g0 + grp, :] = _dot(u_grp, wr_ref[...]).astype(BF16)
            o_ref[g0:g0 + grp, :] = _dot(u_grp, w).astype(BF16)

    @pl.when((j != 0) & (j < n_a_steps))
    def _():
        o_ref[...] = _dot(u_ref[...], wa_ref[...].astype(BF16)).astype(BF16)

    if n_a_steps < n_steps:
        @pl.when(j >= n_a_steps)
        def _():
            o_ref[...] = _dot(u_ref[...], wb_ref[...]).astype(BF16)


def _in_projection(h, shift, scale, rows_per_mod, w_full, w_tail, w_r, tiles=None):
    m = h.shape[0]
    tm = min(2048, rows_per_mod)
    assert m % tm == 0 and rows_per_mod % tm == 0
    tiles_per_mod = rows_per_mod // tm
    all_tiles = tuple(tiles) if tiles else tuple(range(N_MAIN // PROJ_TN))
    n_tiles = len(all_tiles)
    n_cols = n_tiles * PROJ_TN
    n_a_steps = sum(1 for t in all_tiles if t < N_HEAD_TILES)
    assert n_a_steps >= 1 and all(t < N_HEAD_TILES for t in all_tiles[:n_a_steps])

    def w_tile(j):
        if not tiles:
            return j
        idx = tiles[0]
        for n in range(1, len(tiles)):
            idx = idx + (tiles[n] - tiles[n - 1]) * jnp.minimum(jnp.maximum(j - n + 1, 0), 1)
        return idx

    kern = functools.partial(_inproj_kernel, tm=tm, sub=min(128, tm), n_a_steps=n_a_steps, n_steps=n_tiles)
    return pl.pallas_call(
        kern,
        grid=(m // tm, n_tiles),
        in_specs=[
            pl.BlockSpec((tm, D_MODEL), lambda i, j: (i, 0)),
            pl.BlockSpec((1, 1, D_MODEL), lambda i, j: (i // tiles_per_mod, 0, 0)),
            pl.BlockSpec((1, 1, D_MODEL), lambda i, j: (i // tiles_per_mod, 0, 0)),
            pl.BlockSpec((D_MODEL, PROJ_TN), lambda i, j: (0, jnp.minimum(w_tile(j), N_HEAD_TILES - 1))),
            pl.BlockSpec((D_MODEL, PROJ_TN), lambda i, j: (0, jnp.maximum(w_tile(j) - N_HEAD_TILES, 0))),
            pl.BlockSpec((D_MODEL, LANES), lambda i, j: (0, 0)),
        ],
        out_specs=[
            pl.BlockSpec((tm, PROJ_TN), lambda i, j: (i, j)),
            pl.BlockSpec((tm, LANES), lambda i, j: (i, 0)),
        ],
        out_shape=[
            jax.ShapeDtypeStruct((m, n_cols), BF16),
            jax.ShapeDtypeStruct((m, LANES), BF16),
        ],
        scratch_shapes=[pltpu.VMEM((tm, D_MODEL), BF16)],
        compiler_params=_cparams(2),
        name="in_projection",
    )(h, shift, scale, w_full, w_tail, w_r)


ATTN_KEY_BLOCKS = (768, 512, 256)
QK_AHEAD = 2


def _rms(x):
    return x * lax.rsqrt(jnp.mean(x * x, axis=-1, keepdims=True) + EPS)


def _rope(x, cos, sin):
    lane = lax.broadcasted_iota(jnp.int32, x.shape, 1)
    swapped = jnp.where(lane % 2 == 0, pltpu.roll(x, HEAD_DIM - 1, 1), pltpu.roll(x, 1, 1))
    return x * cos + swapped * sin


def _attn_kernel(*refs, n_lat, n_ctx, tq, kblk, kstep):
    if n_lat:
        (q_ref, z_ref, kl_ref, vl_ref, kc_ref, vc_ref, qg_ref, kg_ref, cos_ref, sin_ref,
         o_ref, kn_ref, vn_ref) = refs
    else:
        q_ref, z_ref, kc_ref, vc_ref, qg_ref, kg_ref, o_ref, kn_ref, vn_ref = refs
    qi = pl.program_id(2)
    kg = kg_ref[...]

    @pl.when(qi == 0)
    def _prep():
        if n_lat:
            def body(t, carry):
                r0 = pl.multiple_of(t * kblk, kblk)
                kn = _rms(kl_ref[0, pl.ds(r0, kblk), :].astype(F32)) * kg
                kn = _rope(kn, cos_ref[pl.ds(r0, kblk), :], sin_ref[pl.ds(r0, kblk), :])
                kn_ref[pl.ds(r0, kblk), :] = kn.astype(BF16)
                vn_ref[pl.ds(r0, kblk), 0:HEAD_DIM] = vl_ref[0, pl.ds(r0, kblk), :]
                return carry

            lax.fori_loop(0, n_lat // kblk, body, 0)
        kn_ref[n_lat:n_lat + n_ctx, :] = (_rms(kc_ref[0].astype(F32)) * kg).astype(BF16)
        vn_ref[n_lat:n_lat + n_ctx, 0:HEAD_DIM] = vc_ref[0]
        vn_ref[:, HEAD_DIM:2 * HEAD_DIM] = jnp.ones((n_lat + n_ctx, HEAD_DIM), BF16)

    qg = qg_ref[...]
    if n_lat:
        q0 = pl.multiple_of(qi * tq, tq)
        cos = cos_ref[pl.ds(q0, tq), :]
        sin = sin_ref[pl.ds(q0, tq), :]
    qns = []
    for g in range(2):
        qn = _rms(q_ref[0, :, g * HEAD_DIM:(g + 1) * HEAD_DIM].astype(F32)) * qg
        if n_lat:
            qn = _rope(qn, cos, sin)
        qns.append((qn * (HEAD_DIM ** -0.5 * LOG2E)).astype(BF16))

    items = [(g, k0) for k0 in range(0, n_lat + n_ctx, kstep) for g in range(2)]

    def scores(item):
        g, k0 = item
        return _dot_nt(qns[g], kn_ref[k0:k0 + kstep, :])

    m = [None, None]
    acc = [None, None]
    ahead = [scores(item) for item in items[:QK_AHEAD]]
    for n, (g, k0) in enumerate(items):
        s = ahead.pop(0)
        if n + QK_AHEAD < len(items):
            ahead.append(scores(items[n + QK_AHEAD]))
        m_blk = jnp.max(s, axis=-1, keepdims=True)
        m_new = m_blk if m[g] is None else jnp.maximum(m[g], m_blk)
        p = jnp.exp2((s - m_new).astype(BF16))
        pv = _dot(p, vn_ref[k0:k0 + kstep, :])
        acc[g] = pv if acc[g] is None else acc[g] * jnp.exp2(m[g] - m_new) + pv
        m[g] = m_new
    for g in range(2):
        lanes = slice(g * HEAD_DIM, (g + 1) * HEAD_DIM)
        o = acc[g][:, 0:HEAD_DIM] / acc[g][:, HEAD_DIM:2 * HEAD_DIM]
        o_ref[0, :, lanes] = (o * _silu(z_ref[0, :, lanes].astype(F32))).astype(BF16)


def _attention(p_q, p_lat, p_ctx, q_gain, k_gain, cos, sin, ctx_cols=None):
    b, t_q, _ = p_q.shape
    n_ctx = p_ctx.shape[1]
    n_lat = 0 if p_lat is None else p_lat.shape[1]
    tq = min(256, t_q)
    assert t_q % tq == 0
    kblk = min(256, n_lat) if n_lat else 0
    n_keys = n_lat + n_ctx
    kstep = next((c for c in ATTN_KEY_BLOCKS if n_keys % c == 0), n_keys)
    hw = 2 * HEAD_DIM
    q_spec = pl.BlockSpec((1, tq, hw), lambda i, kv, qi: (i, qi, COL_AQ // hw + kv))
    z_spec = pl.BlockSpec((1, tq, hw), lambda i, kv, qi: (i, qi, COL_AZ // hw + kv))

    def kv_spec(n, col):
        return pl.BlockSpec((1, n, HEAD_DIM), lambda i, kv, qi: (i, 0, col // HEAD_DIM + kv))

    gain_spec = pl.BlockSpec((1, HEAD_DIM), lambda i, kv, qi: (0, 0))
    args = [p_q, p_q]
    specs = [q_spec, z_spec]
    if n_lat:
        args += [p_lat, p_lat]
        specs += [kv_spec(n_lat, COL_AK), kv_spec(n_lat, COL_AV)]
    ctx_cols = ctx_cols or FULL_COLS
    args += [p_ctx, p_ctx, q_gain, k_gain]
    specs += [kv_spec(n_ctx, ctx_cols["ak"]), kv_spec(n_ctx, ctx_cols["av"]), gain_spec, gain_spec]
    if n_lat:
        tab_spec = pl.BlockSpec((n_lat, HEAD_DIM), lambda i, kv, qi: (0, 0))
        args += [cos, sin]
        specs += [tab_spec, tab_spec]
    kern = functools.partial(_attn_kernel, n_lat=n_lat, n_ctx=n_ctx, tq=tq, kblk=kblk, kstep=kstep)
    return pl.pallas_call(
        kern,
        grid=(b, A_KV_HEADS, t_q // tq),
        in_specs=specs,
        out_specs=pl.BlockSpec((1, tq, hw), lambda i, kv, qi: (i, qi, kv)),
        out_shape=jax.ShapeDtypeStruct((b, t_q, BR_W), BF16),
        scratch_shapes=[pltpu.VMEM((n_lat + n_ctx, HEAD_DIM), BF16),
                        pltpu.VMEM((n_lat + n_ctx, 2 * HEAD_DIM), BF16)],
        compiler_params=_cparams(3),
        name="attention_lat" if n_lat else "attention_ctx",
    )(*args)


D_HALO = 16
B_HALO = 16
CONV_RB = 32
CONV_ACCS = 4
ELEM_RB = 32
COPY_RB = 56


def _local_kernel(bg_ref, bc_ref, bcp_ref, bcn_ref, bx_ref, bxp_ref, bxn_ref, bz_ref,
                  da_ref, dap_ref, dan_ref, dg_ref, dgp_ref, dgn_ref, dz_ref,
                  bw_ref, dw_ref, db_ref, dgain_ref, dbeta_ref,
                  yb_ref, yd_ref, tbuf, gbuf, hbuf, *, tt, tiles_per_seq):
    i = pl.program_id(0)
    pos = i % tiles_per_seq
    keep_prev = (pos != 0).astype(F32)
    keep_next = (pos != tiles_per_seq - 1).astype(F32)

    def ld(ref, r0=0, n=None):
        return ref[r0:r0 + (n or ref.shape[0]), :].astype(F32)

    row_blocks = range(0, tt, ELEM_RB)

    tbuf[0:B_HALO, :] = ld(bcp_ref) * ld(bxp_ref) * keep_prev
    for r0 in row_blocks:
        tbuf[B_HALO + r0:B_HALO + r0 + ELEM_RB, :] = ld(bc_ref, r0, ELEM_RB) * ld(bx_ref, r0, ELEM_RB)
    tbuf[B_HALO + tt:2 * B_HALO + tt, :] = ld(bcn_ref) * ld(bxn_ref) * keep_next
    for r0 in row_blocks:
        lo = B_HALO + r0
        conv = (bw_ref[0:1, :] * tbuf[lo - 1:lo - 1 + ELEM_RB, :] + bw_ref[1:2, :] * tbuf[lo:lo + ELEM_RB, :]
                + bw_ref[2:3, :] * tbuf[lo + 1:lo + 1 + ELEM_RB, :])
        yb_ref[r0:r0 + ELEM_RB, :] = (ld(bg_ref, r0, ELEM_RB) * conv
                                      * _silu(ld(bz_ref, r0, ELEM_RB))).astype(BF16)

    gbuf[0, 0:D_HALO, :] = ld(dap_ref) * _sigmoid(ld(dgp_ref)) * keep_prev
    for r0 in row_blocks:
        gbuf[0, D_HALO + r0:D_HALO + r0 + ELEM_RB, :] = (ld(da_ref, r0, ELEM_RB)
                                                          * _sigmoid(ld(dg_ref, r0, ELEM_RB)))
    gbuf[0, D_HALO + tt:2 * D_HALO + tt, :] = ld(dan_ref) * _sigmoid(ld(dgn_ref)) * keep_next
    span = tt + 2 * D_HALO - SUBLANES
    for r in range(1, SUBLANES):
        for x0 in range(0, span, COPY_RB):
            n = min(COPY_RB, span - x0)
            gbuf[r, x0:x0 + n, :] = gbuf[0, r + x0:r + x0 + n, :]
    base = D_HALO - D_CONV // 2
    for cb in range(BR_W // LANES):
        lanes = slice(cb * LANES, (cb + 1) * LANES)
        for rb in range(tt // CONV_RB):
            accs = [None] * CONV_ACCS
            for k in range(D_CONV):
                off = base + k
                r0 = rb * CONV_RB + off - off % SUBLANES
                rows = gbuf[off % SUBLANES, r0:r0 + CONV_RB, lanes]
                term = rows.reshape(CONV_RB // SUBLANES, SUBLANES, LANES) * dw_ref[k, :, lanes][None]
                a = k % CONV_ACCS
                accs[a] = term if accs[a] is None else accs[a] + term
            while len(accs) > 1:
                accs = [accs[n] + accs[n + 1] for n in range(0, len(accs), 2)]
            hbuf[rb * CONV_RB:(rb + 1) * CONV_RB, lanes] = accs[0].reshape(CONV_RB, LANES)
    for r0 in row_blocks:
        hh = hbuf[r0:r0 + ELEM_RB, :] + db_ref[...]
        mu = jnp.mean(hh, axis=-1, keepdims=True)
        hc = hh - mu
        var = jnp.mean(hc * hc, axis=-1, keepdims=True)
        hn = hc * lax.rsqrt(var + EPS) * dgain_ref[...] + dbeta_ref[...]
        yd_ref[r0:r0 + ELEM_RB, :] = (_silu(hn) * _silu(ld(dz_ref, r0, ELEM_RB))).astype(BF16)


def _local_branches(p2, seq_len, b_w, d_w, d_b, d_g, d_beta):
    m = p2.shape[0]
    tt = min(256, seq_len)
    assert seq_len % tt == 0 and m % seq_len == 0
    tiles_per_seq = seq_len // tt
    n_tiles = m // tt

    def cur(col):
        return pl.BlockSpec((tt, BR_W), lambda i: (i, col // BR_W))

    def prev(col, halo):
        per = tt // halo
        return pl.BlockSpec((halo, BR_W), lambda i: (jnp.maximum(i * per - 1, 0), col // BR_W))

    def nxt(col, halo):
        per = tt // halo
        last = m // halo - 1
        return pl.BlockSpec((halo, BR_W), lambda i: (jnp.minimum((i + 1) * per, last), col // BR_W))

    def small(rows):
        return pl.BlockSpec((rows, BR_W), lambda i: (0, 0))

    specs = [cur(COL_BG),
             cur(COL_BC), prev(COL_BC, B_HALO), nxt(COL_BC, B_HALO),
             cur(COL_BX), prev(COL_BX, B_HALO), nxt(COL_BX, B_HALO),
             cur(COL_BZ),
             cur(COL_DA), prev(COL_DA, D_HALO), nxt(COL_DA, D_HALO),
             cur(COL_DG), prev(COL_DG, D_HALO), nxt(COL_DG, D_HALO),
             cur(COL_DZ),
             small(B_CONV), pl.BlockSpec((D_CONV, SUBLANES, BR_W), lambda i: (0, 0, 0)),
             small(1), small(1), small(1)]
    kern = functools.partial(_local_kernel, tt=tt, tiles_per_seq=tiles_per_seq)
    out_spec = pl.BlockSpec((tt, BR_W), lambda i: (i, 0))
    return pl.pallas_call(
        kern,
        grid=(n_tiles,),
        in_specs=specs,
        out_specs=[out_spec, out_spec],
        out_shape=[jax.ShapeDtypeStruct((m, BR_W), BF16), jax.ShapeDtypeStruct((m, BR_W), BF16)],
        scratch_shapes=[pltpu.VMEM((tt + 2 * B_HALO, BR_W), F32),
                        pltpu.VMEM((SUBLANES, tt + 2 * D_HALO, BR_W), F32),
                        pltpu.VMEM((tt, BR_W), F32)],
        compiler_params=_cparams(1),
        name="local_branches",
    )(*([p2] * 15), b_w, jnp.broadcast_to(d_w[:, None, :], (D_CONV, SUBLANES, BR_W)),
      d_b.reshape(1, BR_W), d_g.reshape(1, BR_W), d_beta.reshape(1, BR_W))


CH = C_CHUNK
PAIR_K = 2 * C_HEAD_K
PAIR_V = 2 * C_HEAD_V
LEVELS = (8, 4, 2)
BLK = SUBLANES
N_BLK = CH // BLK
GLA_UNROLL = 4
ROW_EQ = 0
ROW_EK = CH
ROW_LAST = 2 * CH
ROW_FK = 2 * CH + 8
ROW_LVL = ROW_FK + CH
ROW_FQ = ROW_LVL + len(LEVELS) * CH
N_FQ = BLK * (N_BLK * (N_BLK - 1) // 2)
N_EROWS = ROW_FQ + N_FQ


def _slabs(reverse):
    if reverse:
        return [(jb, 0, BLK * jb) for jb in range(1, N_BLK)]
    return [(jb, BLK * (jb + 1), CH - BLK * (jb + 1)) for jb in range(N_BLK - 1)]


def _gla_constants():
    idx = np.arange(CH)
    cols = np.arange(2 * CH) % CH
    emats, masks = [], []
    for reverse in (False, True):
        tri = (idx[None, :] >= idx[:, None]) if reverse else (idx[None, :] <= idx[:, None])
        tri = tri.astype(np.float32)
        edge = tri[0] if reverse else tri[CH - 1]
        blk_edge = (idx // BLK) * BLK + (0 if reverse else BLK - 1)
        blocks = [tri, edge[None, :] - tri, np.tile(edge[None, :], (8, 1)), tri[blk_edge] - tri]
        lvl_masks = []
        for grp in LEVELS:
            half = grp // 2
            in_q = (idx % grp < half) if reverse else (idx % grp >= half)
            ref = (idx // grp) * grp + (half if reverse else half - 1)
            d = tri - tri[ref]
            blocks.append(np.where(in_q[:, None], d, -d))
            in_q_col = (cols % grp < half) if reverse else (cols % grp >= half)
            lvl_masks.append((idx[:, None] // grp == cols[None, :] // grp) & in_q[:, None] & ~in_q_col[None, :])
        lvl_masks.append(idx[:, None] == cols[None, :])
        for jb, q0, nq in _slabs(reverse):
            edge_row = BLK * jb + (0 if reverse else BLK - 1)
            blocks.append(tri[q0:q0 + nq] - tri[edge_row][None, :])
        emat = np.concatenate(blocks, axis=0)
        assert emat.shape == (N_EROWS, CH) and emat.min() >= -1 and emat.max() <= 1
        emats.append(np.concatenate([emat] * 3, axis=1))
        masks.append(np.stack(lvl_masks).astype(np.float32))
    return (jnp.asarray(emats[0], BF16), jnp.asarray(emats[1], BF16), jnp.asarray(np.stack(masks), F32))


def _split3(g):
    g1 = g.astype(BF16)
    r1 = g - g1.astype(F32)
    g2 = r1.astype(BF16)
    g3 = (r1 - g2.astype(F32)).astype(BF16)
    return g1, g2, g3


def _log2_sigmoid(x):
    e = jnp.exp2(jnp.abs(x) * -LOG2E)
    return jnp.minimum(x, 0.0) * LOG2E - jnp.log2(1.0 + e)


def _gla_kernel(*refs, t_lat, t_ctx, want_ctx):
    (ql_ref, kl_ref, vl_ref, zl_ref, rl_ref, qc_ref, kc_ref, vc_ref, zc_ref, rc_ref,
     w2_ref, b2_ref, gain_ref, ematf_ref, ematb_ref, masks_ref) = refs[:16]
    if want_ctx:
        yl_ref, yc_ref = refs[16:18]
        scratch = refs[18:]
    else:
        yl_ref = refs[16]
        scratch = refs[17:]
        yc_ref = zc_ref = None
    stf_ref, stb_ref, of_ref, ob_ref, g1_ref, g2_ref, g3_ref = scratch
    unroll = GLA_UNROLL

    def gates(r_ref, n_rows):
        blk = min(256, n_rows)

        def body(t, carry):
            rows = pl.ds(pl.multiple_of(t * blk, blk), blk)
            x = _dot(r_ref[0, rows, :], w2_ref[...]) + b2_ref[...]
            g1, g2, g3 = _split3(_log2_sigmoid(x) * (1.0 / C_GATE_TAU))
            g1_ref[rows, :] = g1
            g2_ref[rows, :] = g2
            g3_ref[rows, :] = g3
            return carry

        lax.fori_loop(0, n_rows // blk, body, 0)

    def run(q_ref, k_ref, v_ref, n_chunks, want_out):
        assert n_chunks % unroll == 0
        lane_k = lax.broadcasted_iota(jnp.int32, (CH, PAIR_K), 1)
        head0 = lane_k < C_HEAD_K
        zero_v = jnp.zeros((CH, C_HEAD_V), BF16)
        key_blk = (lax.broadcasted_iota(jnp.int32, (BLK, 2 * CH), 1) % CH) // BLK

        def body(i, carry):
            work = []
            for reverse in (False, True):
                lanes = slice(PAIR_K, 2 * PAIR_K) if reverse else slice(0, PAIR_K)
                rows = []
                for u in range(unroll):
                    c = i * unroll + u
                    c = (n_chunks - 1 - c) if reverse else c
                    rows.append(pl.ds(pl.multiple_of(c * CH, CH), CH))
                rhs = jnp.concatenate(
                    [jnp.concatenate([g_ref[r, lanes] for r in rows], axis=1)
                     for g_ref in (g1_ref, g2_ref, g3_ref)], axis=0)
                emat = ematb_ref[...] if reverse else ematf_ref[...]
                e_all = jnp.exp2(_dot(emat, rhs))
                for u in range(unroll):
                    work.append(dict(reverse=reverse, rows=rows[u], u=u,
                                     e=e_all[:, u * PAIR_K:(u + 1) * PAIR_K]))
            work.sort(key=lambda w: w["u"])

            for w in work:
                st_ref = stb_ref if w["reverse"] else stf_ref
                e = w["e"]
                q = q_ref[0, w["rows"], :].astype(F32) * C_HEAD_K ** -0.5
                k = k_ref[0, w["rows"], :].astype(F32)
                v = v_ref[0, w["rows"], :]
                k0 = jnp.where(head0, k, 0.0)
                k1 = jnp.where(head0, 0.0, k)
                st = st_ref[...]
                e_k = e[ROW_EK:ROW_EK + CH]
                k_dec = jnp.concatenate([k0 * e_k, k1 * e_k], axis=0).astype(BF16)
                v_rows = jnp.concatenate([v[:, :C_HEAD_V], v[:, C_HEAD_V:]], axis=0)
                st_ref[...] = e[ROW_LAST:ROW_LAST + 1] * st + _dot_tn(v_rows, k_dec)
                if want_out:
                    qe = q * e[ROW_EQ:ROW_EQ + CH]
                    q_rows = jnp.concatenate([jnp.where(head0, qe, 0.0), jnp.where(head0, 0.0, qe)], axis=0)
                    o_st = _dot(q_rows.astype(BF16), st.T.astype(BF16))
                    w.update(q=q, k0=k0, k1=k1, v=v, o=jnp.concatenate([o_st[:CH], o_st[CH:]], axis=1))
            if not want_out:
                return carry

            for w in work:
                masks = masks_ref.at[1 if w["reverse"] else 0]
                q, k0, k1, e = w["q"], w["k0"], w["k1"], w["e"]
                fk = e[ROW_FK:ROW_FK + CH]
                kk_t = jnp.concatenate([k0 * fk, k1 * fk], axis=0).T.astype(BF16)
                slabs = _slabs(w["reverse"])
                lhs, off = [], ROW_FQ
                for jb, q0, nq in slabs:
                    lhs.append(q[q0:q0 + nq] * e[off:off + nq])
                    off += nq
                r = _dot(jnp.concatenate(lhs, axis=0).astype(BF16), kk_t)
                rows, off = [jnp.zeros((BLK, 2 * CH), F32)] * N_BLK, 0
                for jb, q0, nq in slabs:
                    for ib in range(q0 // BLK, (q0 + nq) // BLK):
                        piece = r[off + ib * BLK - q0:off + (ib + 1) * BLK - q0]
                        rows[ib] = jnp.where(key_blk == jb, piece, rows[ib])
                    off += nq
                scores = jnp.concatenate(rows, axis=0)
                k_heads = jnp.concatenate([k0, k1], axis=0)
                scores = scores + _dot(q.astype(BF16), k_heads.T.astype(BF16)) * masks[len(LEVELS)]
                for lvl in range(len(LEVELS)):
                    e_l = e[ROW_LVL + lvl * CH:ROW_LVL + (lvl + 1) * CH]
                    kk_t = jnp.concatenate([k0 * e_l, k1 * e_l], axis=0).T.astype(BF16)
                    scores = scores + _dot((q * e_l).astype(BF16), kk_t) * masks[lvl]
                w["scores"] = scores.astype(BF16)

            for w in work:
                v = w["v"]
                v_bd = jnp.concatenate([jnp.concatenate([v[:, :C_HEAD_V], zero_v], axis=1),
                                        jnp.concatenate([zero_v, v[:, C_HEAD_V:]], axis=1)], axis=0)
                (ob_ref if w["reverse"] else of_ref)[w["rows"], :] = w["o"] + _dot(w["scores"], v_bd)
            return carry

        lax.fori_loop(0, n_chunks // unroll, body, 0)

    def finish(z_ref, y_ref, n_rows):
        blk = min(256, n_rows)

        def body(t, carry):
            rows = pl.ds(pl.multiple_of(t * blk, blk), blk)
            o = of_ref[rows, :] + ob_ref[rows, :]
            z = z_ref[0, rows, :].astype(F32)
            gain = gain_ref[...]
            for h in range(2):
                lanes = slice(h * C_HEAD_V, (h + 1) * C_HEAD_V)
                y = _rms(o[:, lanes]) * gain[:, lanes]
                y_ref[0, rows, lanes] = (y * _silu(z[:, lanes])).astype(BF16)
            return carry

        lax.fori_loop(0, n_rows // blk, body, 0)

    stf_ref[...] = jnp.zeros_like(stf_ref)
    stb_ref[...] = jnp.zeros_like(stb_ref)
    gates(rc_ref, t_ctx)
    run(qc_ref, kc_ref, vc_ref, t_ctx // CH, want_ctx)
    if want_ctx:
        finish(zc_ref, yc_ref, t_ctx)
    gates(rl_ref, t_lat)
    run(ql_ref, kl_ref, vl_ref, t_lat // CH, True)
    finish(zl_ref, yl_ref, t_lat)


def _gla(p_lat, r_lat, p_ctx, r_ctx, w2, b2, gain, consts, want_ctx, ctx_cols=None):
    b, t_lat, _ = p_lat.shape
    t_ctx = p_ctx.shape[1]
    n_pairs = C_HEADS // 2

    def side(t, with_z, cols):
        z_rows = t if with_z else 16
        return [pl.BlockSpec((1, t, PAIR_K), lambda i, hp: (i, 0, cols["cq"] // PAIR_K + hp)),
                pl.BlockSpec((1, t, PAIR_K), lambda i, hp: (i, 0, cols["ck"] // PAIR_K + hp)),
                pl.BlockSpec((1, t, PAIR_V), lambda i, hp: (i, 0, cols["cv"] // PAIR_V + hp)),
                pl.BlockSpec((1, z_rows, PAIR_V), lambda i, hp: (i, 0, cols["cz"] // PAIR_V + hp)),
                pl.BlockSpec((1, t, LANES), lambda i, hp: (i, 0, 0))]

    def whole(shape):
        nd = len(shape)
        return pl.BlockSpec(shape, lambda i, hp: (0,) * nd)

    emat_f, emat_b, masks = consts
    specs = side(t_lat, True, FULL_COLS) + side(t_ctx, want_ctx, ctx_cols or FULL_COLS) + [
        pl.BlockSpec((1, LANES, 2 * PAIR_K), lambda i, hp: (hp, 0, 0)),
        pl.BlockSpec((1, 1, 2 * PAIR_K), lambda i, hp: (hp, 0, 0)),
        pl.BlockSpec((1, 1, PAIR_V), lambda i, hp: (hp, 0, 0)),
        whole(emat_f.shape), whole(emat_b.shape), whole(masks.shape)]
    out_specs = [pl.BlockSpec((1, t_lat, PAIR_V), lambda i, hp: (i, 0, hp))]
    out_shape = [jax.ShapeDtypeStruct((b, t_lat, BR_W), BF16)]
    if want_ctx:
        out_specs.append(pl.BlockSpec((1, t_ctx, PAIR_V), lambda i, hp: (i, 0, hp)))
        out_shape.append(jax.ShapeDtypeStruct((b, t_ctx, BR_W), BF16))

    def kern(*refs):
        refs = list(refs)
        for n in range(10, 13):
            refs[n] = refs[n].at[0]
        _gla_kernel(*refs, t_lat=t_lat, t_ctx=t_ctx, want_ctx=want_ctx)

    res = pl.pallas_call(
        kern,
        grid=(b, n_pairs),
        in_specs=specs,
        out_specs=out_specs,
        out_shape=out_shape,
        scratch_shapes=[pltpu.VMEM((C_HEAD_V, PAIR_K), F32), pltpu.VMEM((C_HEAD_V, PAIR_K), F32),
                        pltpu.VMEM((t_lat, PAIR_V), F32), pltpu.VMEM((t_lat, PAIR_V), F32),
                        pltpu.VMEM((t_lat, 2 * PAIR_K), BF16), pltpu.VMEM((t_lat, 2 * PAIR_K), BF16),
                        pltpu.VMEM((t_lat, 2 * PAIR_K), BF16)],
        compiler_params=_cparams(2),
        name="gla",
    )(p_lat, p_lat, p_lat, p_lat, r_lat, p_ctx, p_ctx, p_ctx, p_ctx, r_ctx,
      w2, b2, gain, emat_f, emat_b, masks)
    return (res[0], res[1]) if want_ctx else (res[0], None)


def _merge_kernel(*refs, alpha):
    n_mg = N_BRANCH * D_MODEL // MG_BLK
    mg_refs = refs[:n_mg]
    ya_ref, yb_ref, yc_ref, yd_ref, h_ref, gate_ref, wbr_ref, wout_ref, lng_ref, lnb_ref, o_ref = refs[n_mg:]
    per = D_MODEL // MG_BLK
    acc = None
    for n, y_ref in enumerate((ya_ref, yb_ref, yc_ref, yd_ref)):
        mg = jnp.concatenate([r[...] for r in mg_refs[n * per:(n + 1) * per]], axis=1)
        gate = _sigmoid(mg.astype(F32))
        term = gate * _dot(y_ref[...], wbr_ref[n])
        acc = term if acc is None else acc + term
    y = _dot(acc.astype(BF16), wout_ref[...])
    t = alpha * h_ref[...] + gate_ref[0] * y
    mu = jnp.mean(t, axis=-1, keepdims=True)
    tc = t - mu
    var = jnp.mean(tc * tc, axis=-1, keepdims=True)
    o_ref[...] = tc * lax.rsqrt(var + EPS) * lng_ref[...] + lnb_ref[...]


def _merge(p2, ya, yb, yc, yd, h, gate, rows_per_mod, w_br, w_out, ln_g, ln_b, alpha):
    m = h.shape[0]
    tm = min(512, m)
    assert m % tm == 0 and rows_per_mod % tm == 0
    tiles_per_mod = rows_per_mod // tm
    n_mg = N_BRANCH * D_MODEL // MG_BLK
    br_spec = pl.BlockSpec((tm, BR_W), lambda i: (i, 0))
    vec_spec = pl.BlockSpec((1, D_MODEL), lambda i: (0, 0))
    return pl.pallas_call(
        functools.partial(_merge_kernel, alpha=alpha),
        grid=(m // tm,),
        in_specs=[pl.BlockSpec((tm, MG_BLK), functools.partial(lambda n, i: (i, COL_MG // MG_BLK + n), n))
                  for n in range(n_mg)] + [
            br_spec, br_spec, br_spec, br_spec,
            pl.BlockSpec((tm, D_MODEL), lambda i: (i, 0)),
            pl.BlockSpec((1, 1, D_MODEL), lambda i: (i // tiles_per_mod, 0, 0)),
            pl.BlockSpec((N_BRANCH, BR_W, D_MODEL), lambda i: (0, 0, 0)),
            pl.BlockSpec((D_MODEL, D_MODEL), lambda i: (0, 0)),
            vec_spec, vec_spec,
        ],
        out_specs=pl.BlockSpec((tm, D_MODEL), lambda i: (i, 0)),
        out_shape=jax.ShapeDtypeStruct((m, D_MODEL), F32),
        compiler_params=_cparams(1),
        name="merge",
    )(*([p2] * n_mg), ya, yb, yc, yd, h, gate, w_br, w_out, ln_g.reshape(1, D_MODEL), ln_b.reshape(1, D_MODEL))


def _layer_weights(w_in_l, q_norm_l, k_norm_l, c_w2_l, c_b2_l, c_norm_l):
    r0 = int(sum(IN_WIDTHS[:12]))
    r1 = r0 + 2 * C_GATE_RANK
    assert r0 == N_HEAD_TILES * PROJ_TN
    w_tail = w_in_l[:, r1:].astype(BF16)
    w_r = jnp.pad(w_in_l[:, r0:r1], ((0, 0), (0, LANES - 2 * C_GATE_RANK))).astype(BF16)
    n_pairs = C_HEADS // 2

    def gate_w(i):
        w = jnp.pad(c_w2_l[i], ((i * C_GATE_RANK, LANES - (i + 1) * C_GATE_RANK), (0, 0)))
        return w.reshape(LANES, n_pairs, PAIR_K).transpose(1, 0, 2)

    w2 = jnp.concatenate([gate_w(0), gate_w(1)], axis=-1).astype(BF16)
    b2 = jnp.concatenate([c_b2_l[0].reshape(n_pairs, 1, PAIR_K), c_b2_l[1].reshape(n_pairs, 1, PAIR_K)], axis=-1)
    return dict(
        w_full=w_in_l, w_tail=w_tail, w_r=w_r,
        q_gain=q_norm_l.reshape(1, HEAD_DIM), k_gain=k_norm_l.reshape(1, HEAD_DIM),
        w2=w2, b2=b2, c_gain=c_norm_l.reshape(n_pairs, 1, PAIR_V))


def _rope_tables(t):
    rows = t // GRID_W
    row = np.repeat(np.arange(rows), GRID_W).astype(np.float32)
    col = np.tile(np.arange(GRID_W), rows).astype(np.float32)
    inv = (ROPE_THETA ** (-np.arange(0, AXIS_DIM, 2, dtype=np.float32) / AXIS_DIM)).astype(np.float32)
    ang = np.concatenate([row[:, None] * inv, col[:, None] * inv], -1).astype(np.float64)
    cos, sin = np.repeat(np.cos(ang), 2, axis=-1), np.repeat(np.sin(ang), 2, axis=-1)
    sin[:, 0::2] *= -1.0
    return jnp.asarray(cos, F32), jnp.asarray(sin, F32)


def kernel(x, c, ctx, c_ctx, w_mod, b_mod, w_in, q_norm, k_norm, b_conv, c_gate_w2, c_gate_b, c_norm,
           d_conv_w, d_conv_b, d_norm_g, d_norm_b, w_br, w_out, ln_g, ln_b):
    b, t, d = x.shape
    t_ctx = ctx.shape[1]
    depth = w_mod.shape[0]
    assert d == D_MODEL and b < 16
    alpha = (2 * depth) ** 0.25

    cc = jnp.concatenate([c, c_ctx[None], jnp.zeros((16 - b - 1, D_MODEL), F32)], axis=0)
    mod = _modulation(cc, w_mod, b_mod)
    cos, sin = _rope_tables(t)
    consts = _gla_constants()

    h_lat = x.reshape(b * t, D_MODEL)
    h_ctx = ctx.reshape(b * t_ctx, D_MODEL)
    for l in range(depth):
        want_ctx = l < depth - 1
        lw = _layer_weights(w_in[l], q_norm[l], k_norm[l], c_gate_w2[l], c_gate_b[l], c_norm[l])
        shift, scale, gate = [mod[l, :, n * D_MODEL:(n + 1) * D_MODEL].reshape(16, 1, D_MODEL) for n in range(3)]
        p_lat, r_lat = _in_projection(h_lat, shift, scale, t, lw["w_full"], lw["w_tail"], lw["w_r"])
        ctx_tiles = None if want_ctx else tuple(c // PROJ_TN for c in CTX_KV_TILES)
        ctx_cols = FULL_COLS if want_ctx else CTX_KV_COLS
        p_ctx, r_ctx = _in_projection(h_ctx, shift[b:], scale[b:], b * t_ctx, lw["w_full"], lw["w_tail"],
                                      lw["w_r"], ctx_tiles)
        p_lat3 = p_lat.reshape(b, t, N_MAIN)
        p_ctx3 = p_ctx.reshape(b, t_ctx, p_ctx.shape[1])
        r_lat3 = r_lat.reshape(b, t, LANES)
        r_ctx3 = r_ctx.reshape(b, t_ctx, LANES)
        w_br_l = w_br[l].astype(BF16)
        w_out_l = w_out[l].astype(BF16)

        ya_l = _attention(p_lat3, p_lat3, p_ctx3, lw["q_gain"], lw["k_gain"], cos, sin, ctx_cols)
        yb_l, yd_l = _local_branches(p_lat, t, b_conv[l], d_conv_w[l], d_conv_b[l], d_norm_g[l], d_norm_b[l])
        yc_l, yc_c = _gla(p_lat3, r_lat3, p_ctx3, r_ctx3, lw["w2"], lw["b2"], lw["c_gain"], consts, want_ctx,
                          ctx_cols)
        h_lat_new = _merge(p_lat, ya_l.reshape(b * t, BR_W), yb_l, yc_l.reshape(b * t, BR_W), yd_l,
                           h_lat, gate, t, w_br_l, w_out_l, ln_g[l], ln_b[l], alpha)
        if want_ctx:
            ya_c = _attention(p_ctx3, None, p_ctx3, lw["q_gain"], lw["k_gain"], None, None)
            yb_c, yd_c = _local_branches(p_ctx, t_ctx, b_conv[l], d_conv_w[l], d_conv_b[l], d_norm_g[l],
                                         d_norm_b[l])
            h_ctx = _merge(p_ctx, ya_c.reshape(b * t_ctx, BR_W), yb_c, yc_c.reshape(b * t_ctx, BR_W), yd_c,
                           h_ctx, gate[b:], b * t_ctx, w_br_l, w_out_l, ln_g[l], ln_b[l], alpha)
        h_lat = h_lat_new
    return h_lat.reshape(b, t, D_MODEL)
```

```python
import functools

import numpy as np
import jax
import jax.numpy as jnp
from jax import lax
from jax.experimental import pallas as pl
from jax.experimental.pallas import tpu as pltpu

F32 = jnp.float32
BF16 = jnp.bfloat16

D_MODEL = 1024
GRID_W = 64
N_BRANCH = 4
BR_W = D_MODEL // 2
HEAD_DIM = 128
A_HEADS = BR_W // HEAD_DIM
A_KV_HEADS = A_HEADS // 2
ROPE_THETA = 10000.0
AXIS_DIM = HEAD_DIM // 2
B_CONV = 3
C_HEADS = 4
C_HEAD_K = BR_W // (2 * C_HEADS)
C_HEAD_V = BR_W // C_HEADS
C_KEY_W = C_HEADS * C_HEAD_K
C_GATE_RANK = 16
C_GATE_TAU = 16.0
C_CHUNK = 64
D_CONV = 31
EPS = 1e-6
IN_WIDTHS = (
    A_HEADS * HEAD_DIM, A_KV_HEADS * HEAD_DIM, A_KV_HEADS * HEAD_DIM, BR_W,
    BR_W, BR_W, BR_W, BR_W,
    C_KEY_W, C_KEY_W, C_HEADS * C_HEAD_V, BR_W, 2 * C_GATE_RANK,
    2 * BR_W, BR_W,
    N_BRANCH * D_MODEL,
)

COL_AQ = 0
COL_AK = COL_AQ + 512
COL_AV = COL_AK + 256
COL_AZ = COL_AV + 256
COL_BG = COL_AZ + 512
COL_BC = COL_BG + 512
COL_BX = COL_BC + 512
COL_BZ = COL_BX + 512
COL_CQ = COL_BZ + 512
COL_CK = COL_CQ + 256
COL_CV = COL_CK + 256
COL_CZ = COL_CV + 512
COL_DA = COL_CZ + 512
COL_DG = COL_DA + 512
COL_DZ = COL_DG + 512
COL_MG = COL_DZ + 512
MG_BLK = 512
N_MAIN = COL_MG + N_BRANCH * D_MODEL
FULL_COLS = dict(ak=COL_AK, av=COL_AV, cq=COL_CQ, ck=COL_CK, cv=COL_CV, cz=COL_CZ)
CTX_KV_TILES = (COL_AK, COL_CQ, COL_CV)
CTX_KV_COLS = dict(ak=0, av=256, cq=512, ck=768, cv=1024, cz=1024)
LANES = 128
SUBLANES = 8
PROJ_TN = 1536
CTX_KV_TN = 512
LN_GROUP = 512
VMEM_LIMIT = 56 * 1024 * 1024
LOG2E = 1.4426950408889634


def _cparams(n_axes):
    return pltpu.CompilerParams(dimension_semantics=("arbitrary",) * n_axes,
                                vmem_limit_bytes=VMEM_LIMIT)


def _sigmoid(x):
    return 0.5 * jnp.tanh(0.5 * x) + 0.5


def _silu(x):
    h = 0.5 * x
    return h + h * jnp.tanh(h)


def _dot(a, b):
    return jnp.dot(a, b, preferred_element_type=F32)


def _dot_nt(a, b):
    return lax.dot_general(a, b, (((1,), (1,)), ((), ())), preferred_element_type=F32)


def _dot_tn(a, b):
    return lax.dot_general(a, b, (((0,), (0,)), ((), ())), preferred_element_type=F32)


def _mod_kernel(c_ref, w_ref, b_ref, o_ref):
    s = _silu(c_ref[...])
    o_ref[0] = _dot(s.astype(BF16), w_ref[0].astype(BF16)) + b_ref[0]


def _modulation(cc, w_mod, b_mod):
    depth = w_mod.shape[0]
    n_rows = cc.shape[0]
    return pl.pallas_call(
        _mod_kernel,
        grid=(depth, 3),
        in_specs=[
            pl.BlockSpec((n_rows, D_MODEL), lambda l, j: (0, 0)),
            pl.BlockSpec((1, D_MODEL, D_MODEL), lambda l, j: (l, 0, j)),
            pl.BlockSpec((1, 1, D_MODEL), lambda l, j: (l, 0, j)),
        ],
        out_specs=pl.BlockSpec((1, n_rows, D_MODEL), lambda l, j: (l, 0, j)),
        out_shape=jax.ShapeDtypeStruct((depth, n_rows, 3 * D_MODEL), F32),
        compiler_params=_cparams(2),
        name="modulation",
    )(cc, w_mod, b_mod.reshape(depth, 1, 3 * D_MODEL))


def _inproj_kernel(x_ref, shift_ref, scale_ref, w_ref, wr_ref, o_ref, or_ref, u_ref, *, tm, sub):
    j = pl.program_id(1)

    @pl.when(j == 0)
    def _():
        one_plus = 1.0 + scale_ref[0]
        shift = shift_ref[0]
        grp = min(LN_GROUP, tm)
        for g0 in range(0, tm, grp):
            for r0 in range(g0, g0 + grp, sub):
                x = x_ref[r0:r0 + sub, :]
                mu = jnp.mean(x, axis=-1, keepdims=True)
                xc = x - mu
                var = jnp.mean(xc * xc, axis=-1, keepdims=True)
                u = xc * lax.rsqrt(var + EPS) * one_plus + shift
                u_ref[r0:r0 + sub, :] = u.astype(BF16)
            u_grp = u_ref[g0:g0 + grp, :]
            or_ref[g0:g0 + grp, :] = _dot(u_grp, wr_ref[...]).astype(BF16)
            o_ref[g0:g0 + grp, :] = _dot(u_grp, w_ref[...]).astype(BF16)

    @pl.when(j != 0)
    def _():
        o_ref[...] = _dot(u_ref[...], w_ref[...]).astype(BF16)


def _in_projection(h, shift, scale, rows_per_mod, w_main, w_r, layer, tiles=None, tn=PROJ_TN):
    m = h.shape[0]
    tm = min(2048, rows_per_mod)
    assert m % tm == 0 and rows_per_mod % tm == 0
    tiles_per_mod = rows_per_mod // tm
    assert w_main.shape[2] % tn == 0
    n_tiles = len(tiles) if tiles else w_main.shape[2] // tn
    n_cols = n_tiles * tn

    def w_tile(j):
        if not tiles:
            return j
        idx = tiles[0]
        for n in range(1, len(tiles)):
            idx = idx + (tiles[n] - tiles[n - 1]) * jnp.minimum(jnp.maximum(j - n + 1, 0), 1)
        return idx

    kern = functools.partial(_inproj_kernel, tm=tm, sub=min(128, tm))
    return pl.pallas_call(
        kern,
        grid=(m // tm, n_tiles),
        in_specs=[
            pl.BlockSpec((tm, D_MODEL), lambda i, j: (i, 0)),
            pl.BlockSpec((1, 1, D_MODEL), lambda i, j: (i // tiles_per_mod, 0, 0)),
            pl.BlockSpec((1, 1, D_MODEL), lambda i, j: (i // tiles_per_mod, 0, 0)),
            pl.BlockSpec((None, D_MODEL, tn), lambda i, j: (layer, 0, w_tile(j))),
            pl.BlockSpec((None, D_MODEL, LANES), lambda i, j: (layer, 0, 0)),
        ],
        out_specs=[
            pl.BlockSpec((tm, tn), lambda i, j: (i, j)),
            pl.BlockSpec((tm, LANES), lambda i, j: (i, 0)),
        ],
        out_shape=[
            jax.ShapeDtypeStruct((m, n_cols), BF16),
            jax.ShapeDtypeStruct((m, LANES), BF16),
        ],
        scratch_shapes=[pltpu.VMEM((tm, D_MODEL), BF16)],
        compiler_params=_cparams(2),
        name="in_projection",
    )(h, shift, scale, w_main, w_r)


ATTN_KEY_BLOCKS = (768, 512, 256)
QK_AHEAD = 2


def _rms(x):
    return x * lax.rsqrt(jnp.mean(x * x, axis=-1, keepdims=True) + EPS)


def _rope(x, cos, sin):
    lane = lax.broadcasted_iota(jnp.int32, x.shape, 1)
    swapped = jnp.where(lane % 2 == 0, pltpu.roll(x, HEAD_DIM - 1, 1), pltpu.roll(x, 1, 1))
    return x * cos + swapped * sin


def _attn_kernel(*refs, n_lat, n_ctx, tq, kblk, kstep):
    if n_lat:
        (q_ref, z_ref, kl_ref, vl_ref, kc_ref, vc_ref, qg_ref, kg_ref, cos_ref, sin_ref,
         o_ref, kn_ref, vn_ref) = refs
    else:
        q_ref, z_ref, kc_ref, vc_ref, qg_ref, kg_ref, o_ref, kn_ref, vn_ref = refs
    qi = pl.program_id(2)
    kg = kg_ref[...]

    @pl.when(qi == 0)
    def _prep():
        if n_lat:
            def body(t, carry):
                r0 = pl.multiple_of(t * kblk, kblk)
                kn = _rms(kl_ref[0, pl.ds(r0, kblk), :].astype(F32)) * kg
                kn = _rope(kn, cos_ref[pl.ds(r0, kblk), :], sin_ref[pl.ds(r0, kblk), :])
                kn_ref[pl.ds(r0, kblk), :] = kn.astype(BF16)
                vn_ref[pl.ds(r0, kblk), 0:HEAD_DIM] = vl_ref[0, pl.ds(r0, kblk), :]
                return carry

            lax.fori_loop(0, n_lat // kblk, body, 0)
        kn_ref[n_lat:n_lat + n_ctx, :] = (_rms(kc_ref[0].astype(F32)) * kg).astype(BF16)
        vn_ref[n_lat:n_lat + n_ctx, 0:HEAD_DIM] = vc_ref[0]
        vn_ref[:, HEAD_DIM:2 * HEAD_DIM] = jnp.ones((n_lat + n_ctx, HEAD_DIM), BF16)

    qg = qg_ref[...]
    if n_lat:
        q0 = pl.multiple_of(qi * tq, tq)
        cos = cos_ref[pl.ds(q0, tq), :]
        sin = sin_ref[pl.ds(q0, tq), :]
    qns = []
    for g in range(2):
        qn = _rms(q_ref[0, :, g * HEAD_DIM:(g + 1) * HEAD_DIM].astype(F32)) * qg
        if n_lat:
            qn = _rope(qn, cos, sin)
        qns.append((qn * (HEAD_DIM ** -0.5 * LOG2E)).astype(BF16))

    items = [(g, k0) for k0 in range(0, n_lat + n_ctx, kstep) for g in range(2)]

    def scores(item):
        g, k0 = item
        return _dot_nt(qns[g], kn_ref[k0:k0 + kstep, :])

    m = [None, None]
    acc = [None, None]
    ahead = [scores(item) for item in items[:QK_AHEAD]]
    for n, (g, k0) in enumerate(items):
        s = ahead.pop(0)
        if n + QK_AHEAD < len(items):
            ahead.append(scores(items[n + QK_AHEAD]))
        m_blk = jnp.max(s, axis=-1, keepdims=True)
        m_new = m_blk if m[g] is None else jnp.maximum(m[g], m_blk)
        p = jnp.exp2((s - m_new).astype(BF16))
        pv = _dot(p, vn_ref[k0:k0 + kstep, :])
        acc[g] = pv if acc[g] is None else acc[g] * jnp.exp2(m[g] - m_new) + pv
        m[g] = m_new
    for g in range(2):
        lanes = slice(g * HEAD_DIM, (g + 1) * HEAD_DIM)
        o = acc[g][:, 0:HEAD_DIM] / acc[g][:, HEAD_DIM:2 * HEAD_DIM]
        o_ref[0, :, lanes] = (o * _silu(z_ref[0, :, lanes].astype(F32))).astype(BF16)


def _attention(p_q, p_lat, p_ctx, q_gain, k_gain, cos, sin, ctx_cols=None):
    b, t_q, _ = p_q.shape
    n_ctx = p_ctx.shape[1]
    n_lat = 0 if p_lat is None else p_lat.shape[1]
    tq = min(512, t_q)
    assert t_q % tq == 0
    kblk = min(256, n_lat) if n_lat else 0
    n_keys = n_lat + n_ctx
    kstep = next((c for c in ATTN_KEY_BLOCKS if n_keys % c == 0), n_keys)
    hw = 2 * HEAD_DIM
    q_spec = pl.BlockSpec((1, tq, hw), lambda i, kv, qi: (i, qi, COL_AQ // hw + kv))
    z_spec = pl.BlockSpec((1, tq, hw), lambda i, kv, qi: (i, qi, COL_AZ // hw + kv))

    def kv_spec(n, col):
        return pl.BlockSpec((1, n, HEAD_DIM), lambda i, kv, qi: (i, 0, col // HEAD_DIM + kv))

    gain_spec = pl.BlockSpec((1, HEAD_DIM), lambda i, kv, qi: (0, 0))
    args = [p_q, p_q]
    specs = [q_spec, z_spec]
    if n_lat:
        args += [p_lat, p_lat]
        specs += [kv_spec(n_lat, COL_AK), kv_spec(n_lat, COL_AV)]
    ctx_cols = ctx_cols or FULL_COLS
    args += [p_ctx, p_ctx, q_gain, k_gain]
    specs += [kv_spec(n_ctx, ctx_cols["ak"]), kv_spec(n_ctx, ctx_cols["av"]), gain_spec, gain_spec]
    if n_lat:
        tab_spec = pl.BlockSpec((n_lat, HEAD_DIM), lambda i, kv, qi: (0, 0))
        args += [cos, sin]
        specs += [tab_spec, tab_spec]
    kern = functools.partial(_attn_kernel, n_lat=n_lat, n_ctx=n_ctx, tq=tq, kblk=kblk, kstep=kstep)
    return pl.pallas_call(
        kern,
        grid=(b, A_KV_HEADS, t_q // tq),
        in_specs=specs,
        out_specs=pl.BlockSpec((1, tq, hw), lambda i, kv, qi: (i, qi, kv)),
        out_shape=jax.ShapeDtypeStruct((b, t_q, BR_W), BF16),
        scratch_shapes=[pltpu.VMEM((n_lat + n_ctx, HEAD_DIM), BF16),
                        pltpu.VMEM((n_lat + n_ctx, 2 * HEAD_DIM), BF16)],
        compiler_params=_cparams(3),
        name="attention_lat" if n_lat else "attention_ctx",
    )(*args)


D_HALO = 16
B_HALO = 16
CONV_RB = 32
CONV_ACCS = 4
ELEM_RB = 32
COPY_RB = 56


def _local_kernel(bg_ref, bc_ref, bcp_ref, bcn_ref, bx_ref, bxp_ref, bxn_ref, bz_ref,
                  da_ref, dap_ref, dan_ref, dg_ref, dgp_ref, dgn_ref, dz_ref,
                  bw_ref, dw_ref, db_ref, dgain_ref, dbeta_ref,
                  yb_ref, yd_ref, tbuf, gbuf, hbuf, *, tt, tiles_per_seq):
    i = pl.program_id(0)
    pos = i % tiles_per_seq
    keep_prev = (pos != 0).astype(F32)
    keep_next = (pos != tiles_per_seq - 1).astype(F32)

    def ld(ref, r0=0, n=None):
        return ref[r0:r0 + (n or ref.shape[0]), :].astype(F32)

    row_blocks = range(0, tt, ELEM_RB)

    tbuf[0:B_HALO, :] = ld(bcp_ref) * ld(bxp_ref) * keep_prev
    for r0 in row_blocks:
        tbuf[B_HALO + r0:B_HALO + r0 + ELEM_RB, :] = ld(bc_ref, r0, ELEM_RB) * ld(bx_ref, r0, ELEM_RB)
    tbuf[B_HALO + tt:2 * B_HALO + tt, :] = ld(bcn_ref) * ld(bxn_ref) * keep_next
    for r0 in row_blocks:
        lo = B_HALO + r0
        conv = (bw_ref[0:1, :] * tbuf[lo - 1:lo - 1 + ELEM_RB, :] + bw_ref[1:2, :] * tbuf[lo:lo + ELEM_RB, :]
                + bw_ref[2:3, :] * tbuf[lo + 1:lo + 1 + ELEM_RB, :])
        yb_ref[r0:r0 + ELEM_RB, :] = (ld(bg_ref, r0, ELEM_RB) * conv
                                      * _silu(ld(bz_ref, r0, ELEM_RB))).astype(BF16)

    gbuf[0, 0:D_HALO, :] = ld(dap_ref) * _sigmoid(ld(dgp_ref)) * keep_prev
    for r0 in row_blocks:
        gbuf[0, D_HALO + r0:D_HALO + r0 + ELEM_RB, :] = (ld(da_ref, r0, ELEM_RB)
                                                          * _sigmoid(ld(dg_ref, r0, ELEM_RB)))
    gbuf[0, D_HALO + tt:2 * D_HALO + tt, :] = ld(dan_ref) * _sigmoid(ld(dgn_ref)) * keep_next
    span = tt + 2 * D_HALO - SUBLANES
    for r in range(1, SUBLANES):
        for x0 in range(0, span, COPY_RB):
            n = min(COPY_RB, span - x0)
            gbuf[r, x0:x0 + n, :] = gbuf[0, r + x0:r + x0 + n, :]
    base = D_HALO - D_CONV // 2
    for cb in range(BR_W // LANES):
        lanes = slice(cb * LANES, (cb + 1) * LANES)
        for rb in range(tt // CONV_RB):
            accs = [None] * CONV_ACCS
            for k in range(D_CONV):
                off = base + k
                r0 = rb * CONV_RB + off - off % SUBLANES
                rows = gbuf[off % SUBLANES, r0:r0 + CONV_RB, lanes]
                term = rows.reshape(CONV_RB // SUBLANES, SUBLANES, LANES) * dw_ref[k, :, lanes][None]
                a = k % CONV_ACCS
                accs[a] = term if accs[a] is None else accs[a] + term
            while len(accs) > 1:
                accs = [accs[n] + accs[n + 1] for n in range(0, len(accs), 2)]
            hbuf[rb * CONV_RB:(rb + 1) * CONV_RB, lanes] = accs[0].reshape(CONV_RB, LANES)
    for r0 in row_blocks:
        hh = hbuf[r0:r0 + ELEM_RB, :] + db_ref[...]
        mu = jnp.mean(hh, axis=-1, keepdims=True)
        hc = hh - mu
        var = jnp.mean(hc * hc, axis=-1, keepdims=True)
        hn = hc * lax.rsqrt(var + EPS) * dgain_ref[...] + dbeta_ref[...]
        yd_ref[r0:r0 + ELEM_RB, :] = (_silu(hn) * _silu(ld(dz_ref, r0, ELEM_RB))).astype(BF16)


def _local_branches(p2, seq_len, b_w, d_w, d_b, d_g, d_beta):
    m = p2.shape[0]
    tt = min(512, seq_len)
    assert seq_len % tt == 0 and m % seq_len == 0
    tiles_per_seq = seq_len // tt
    n_tiles = m // tt

    def cur(col):
        return pl.BlockSpec((tt, BR_W), lambda i: (i, col // BR_W))

    def prev(col, halo):
        per = tt // halo
        return pl.BlockSpec((halo, BR_W), lambda i: (jnp.maximum(i * per - 1, 0), col // BR_W))

    def nxt(col, halo):
        per = tt // halo
        last = m // halo - 1
        return pl.BlockSpec((halo, BR_W), lambda i: (jnp.minimum((i + 1) * per, last), col // BR_W))

    def small(rows):
        return pl.BlockSpec((rows, BR_W), lambda i: (0, 0))

    specs = [cur(COL_BG),
             cur(COL_BC), prev(COL_BC, B_HALO), nxt(COL_BC, B_HALO),
             cur(COL_BX), prev(COL_BX, B_HALO), nxt(COL_BX, B_HALO),
             cur(COL_BZ),
             cur(COL_DA), prev(COL_DA, D_HALO), nxt(COL_DA, D_HALO),
             cur(COL_DG), prev(COL_DG, D_HALO), nxt(COL_DG, D_HALO),
             cur(COL_DZ),
             small(B_CONV), pl.BlockSpec((D_CONV, SUBLANES, BR_W), lambda i: (0, 0, 0)),
             small(1), small(1), small(1)]
    kern = functools.partial(_local_kernel, tt=tt, tiles_per_seq=tiles_per_seq)
    out_spec = pl.BlockSpec((tt, BR_W), lambda i: (i, 0))
    return pl.pallas_call(
        kern,
        grid=(n_tiles,),
        in_specs=specs,
        out_specs=[out_spec, out_spec],
        out_shape=[jax.ShapeDtypeStruct((m, BR_W), BF16), jax.ShapeDtypeStruct((m, BR_W), BF16)],
        scratch_shapes=[pltpu.VMEM((tt + 2 * B_HALO, BR_W), F32),
                        pltpu.VMEM((SUBLANES, tt + 2 * D_HALO, BR_W), F32),
                        pltpu.VMEM((tt, BR_W), F32)],
        compiler_params=_cparams(1),
        name="local_branches",
    )(*([p2] * 15), b_w, jnp.broadcast_to(d_w[:, None, :], (D_CONV, SUBLANES, BR_W)),
      d_b.reshape(1, BR_W), d_g.reshape(1, BR_W), d_beta.reshape(1, BR_W))


CH = C_CHUNK
PAIR_K = 2 * C_HEAD_K
PAIR_V = 2 * C_HEAD_V
LEVELS = (8, 4, 2)
BLK = SUBLANES
N_BLK = CH // BLK
GLA_UNROLL = 4
ROW_EQ = 0
ROW_EK = CH
ROW_LAST = 2 * CH
ROW_FK = 2 * CH + 8
ROW_LVL = ROW_FK + CH
ROW_FQ = ROW_LVL + len(LEVELS) * CH
N_FQ = BLK * (N_BLK * (N_BLK - 1) // 2)
N_EROWS = ROW_FQ + N_FQ


def _slabs(reverse):
    if reverse:
        return [(jb, 0, BLK * jb) for jb in range(1, N_BLK)]
    return [(jb, BLK * (jb + 1), CH - BLK * (jb + 1)) for jb in range(N_BLK - 1)]


def _gla_constants():
    idx = np.arange(CH)
    cols = np.arange(2 * CH) % CH
    emats, masks = [], []
    for reverse in (False, True):
        tri = (idx[None, :] >= idx[:, None]) if reverse else (idx[None, :] <= idx[:, None])
        tri = tri.astype(np.float32)
        edge = tri[0] if reverse else tri[CH - 1]
        blk_edge = (idx // BLK) * BLK + (0 if reverse else BLK - 1)
        blocks = [tri, edge[None, :] - tri, np.tile(edge[None, :], (8, 1)), tri[blk_edge] - tri]
        lvl_masks = []
        for grp in LEVELS:
            half = grp // 2
            in_q = (idx % grp < half) if reverse else (idx % grp >= half)
            ref = (idx // grp) * grp + (half if reverse else half - 1)
            d = tri - tri[ref]
            blocks.append(np.where(in_q[:, None], d, -d))
            in_q_col = (cols % grp < half) if reverse else (cols % grp >= half)
            lvl_masks.append((idx[:, None] // grp == cols[None, :] // grp) & in_q[:, None] & ~in_q_col[None, :])
        lvl_masks.append(idx[:, None] == cols[None, :])
        for jb, q0, nq in _slabs(reverse):
            edge_row = BLK * jb + (0 if reverse else BLK - 1)
            blocks.append(tri[q0:q0 + nq] - tri[edge_row][None, :])
        emat = np.concatenate(blocks, axis=0)
        assert emat.shape == (N_EROWS, CH) and emat.min() >= -1 and emat.max() <= 1
        emats.append(np.concatenate([emat] * 3, axis=1))
        masks.append(np.stack(lvl_masks).astype(np.float32))
    return (jnp.asarray(emats[0], BF16), jnp.asarray(emats[1], BF16), jnp.asarray(np.stack(masks), F32))


def _split3(g):
    g1 = g.astype(BF16)
    r1 = g - g1.astype(F32)
    g2 = r1.astype(BF16)
    g3 = (r1 - g2.astype(F32)).astype(BF16)
    return g1, g2, g3


def _log2_sigmoid(x):
    e = jnp.exp2(jnp.abs(x) * -LOG2E)
    return jnp.minimum(x, 0.0) * LOG2E - jnp.log2(1.0 + e)


def _gla_kernel(*refs, t_lat, t_ctx, want_ctx):
    (ql_ref, kl_ref, vl_ref, zl_ref, rl_ref, qc_ref, kc_ref, vc_ref, zc_ref, rc_ref,
     w2_ref, b2_ref, gain_ref, ematf_ref, ematb_ref, masks_ref) = refs[:16]
    if want_ctx:
        yl_ref, yc_ref = refs[16:18]
        scratch = refs[18:]
    else:
        yl_ref = refs[16]
        scratch = refs[17:]
        yc_ref = zc_ref = None
    stf_ref, stb_ref, of_ref, ob_ref, g1_ref, g2_ref, g3_ref = scratch
    unroll = GLA_UNROLL

    def gates(r_ref, n_rows):
        blk = min(256, n_rows)

        def body(t, carry):
            rows = pl.ds(pl.multiple_of(t * blk, blk), blk)
            x = _dot(r_ref[0, rows, :], w2_ref[...]) + b2_ref[...]
            g1, g2, g3 = _split3(_log2_sigmoid(x) * (1.0 / C_GATE_TAU))
            g1_ref[rows, :] = g1
            g2_ref[rows, :] = g2
            g3_ref[rows, :] = g3
            return carry

        lax.fori_loop(0, n_rows // blk, body, 0)

    def run(q_ref, k_ref, v_ref, n_chunks, want_out):
        assert n_chunks % unroll == 0
        lane_k = lax.broadcasted_iota(jnp.int32, (CH, PAIR_K), 1)
        head0 = lane_k < C_HEAD_K
        zero_v = jnp.zeros((CH, C_HEAD_V), BF16)
        key_blk = (lax.broadcasted_iota(jnp.int32, (BLK, 2 * CH), 1) % CH) // BLK

        def body(i, carry):
            work = []
            for reverse in (False, True):
                lanes = slice(PAIR_K, 2 * PAIR_K) if reverse else slice(0, PAIR_K)
                rows = []
                for u in range(unroll):
                    c = i * unroll + u
                    c = (n_chunks - 1 - c) if reverse else c
                    rows.append(pl.ds(pl.multiple_of(c * CH, CH), CH))
                rhs = jnp.concatenate(
                    [jnp.concatenate([g_ref[r, lanes] for r in rows], axis=1)
                     for g_ref in (g1_ref, g2_ref, g3_ref)], axis=0)
                emat = ematb_ref[...] if reverse else ematf_ref[...]
                e_all = jnp.exp2(_dot(emat, rhs))
                for u in range(unroll):
                    work.append(dict(reverse=reverse, rows=rows[u], u=u,
                                     e=e_all[:, u * PAIR_K:(u + 1) * PAIR_K]))
            work.sort(key=lambda w: w["u"])

            for w in work:
                st_ref = stb_ref if w["reverse"] else stf_ref
                e = w["e"]
                q = q_ref[0, w["rows"], :].astype(F32) * C_HEAD_K ** -0.5
                k = k_ref[0, w["rows"], :].astype(F32)
                v = v_ref[0, w["rows"], :]
                k0 = jnp.where(head0, k, 0.0)
                k1 = jnp.where(head0, 0.0, k)
                st = st_ref[...]
                e_k = e[ROW_EK:ROW_EK + CH]
                k_dec = jnp.concatenate([k0 * e_k, k1 * e_k], axis=0).astype(BF16)
                v_rows = jnp.concatenate([v[:, :C_HEAD_V], v[:, C_HEAD_V:]], axis=0)
                st_ref[...] = e[ROW_LAST:ROW_LAST + 1] * st + _dot_tn(v_rows, k_dec)
                if want_out:
                    qe = q * e[ROW_EQ:ROW_EQ + CH]
                    q_rows = jnp.concatenate([jnp.where(head0, qe, 0.0), jnp.where(head0, 0.0, qe)], axis=0)
                    o_st = _dot(q_rows.astype(BF16), st.T.astype(BF16))
                    w.update(q=q, k0=k0, k1=k1, v=v, o=jnp.concatenate([o_st[:CH], o_st[CH:]], axis=1))
            if not want_out:
                return carry

            for w in work:
                masks = masks_ref.at[1 if w["reverse"] else 0]
                q, k0, k1, e = w["q"], w["k0"], w["k1"], w["e"]
                fk = e[ROW_FK:ROW_FK + CH]
                kk_t = jnp.concatenate([k0 * fk, k1 * fk], axis=0).T.astype(BF16)
                slabs = _slabs(w["reverse"])
                lhs, off = [], ROW_FQ
                for jb, q0, nq in slabs:
                    lhs.append(q[q0:q0 + nq] * e[off:off + nq])
                    off += nq
                r = _dot(jnp.concatenate(lhs, axis=0).astype(BF16), kk_t)
                rows, off = [jnp.zeros((BLK, 2 * CH), F32)] * N_BLK, 0
                for jb, q0, nq in slabs:
                    for ib in range(q0 // BLK, (q0 + nq) // BLK):
                        piece = r[off + ib * BLK - q0:off + (ib + 1) * BLK - q0]
                        rows[ib] = jnp.where(key_blk == jb, piece, rows[ib])
                    off += nq
                scores = jnp.concatenate(rows, axis=0)
                k_heads = jnp.concatenate([k0, k1], axis=0)
                scores = scores + _dot(q.astype(BF16), k_heads.T.astype(BF16)) * masks[len(LEVELS)]
                for lvl in range(len(LEVELS)):
                    e_l = e[ROW_LVL + lvl * CH:ROW_LVL + (lvl + 1) * CH]
                    kk_t = jnp.concatenate([k0 * e_l, k1 * e_l], axis=0).T.astype(BF16)
                    scores = scores + _dot((q * e_l).astype(BF16), kk_t) * masks[lvl]
                w["scores"] = scores.astype(BF16)

            for w in work:
                v = w["v"]
                v_bd = jnp.concatenate([jnp.concatenate([v[:, :C_HEAD_V], zero_v], axis=1),
                                        jnp.concatenate([zero_v, v[:, C_HEAD_V:]], axis=1)], axis=0)
                (ob_ref if w["reverse"] else of_ref)[w["rows"], :] = w["o"] + _dot(w["scores"], v_bd)
            return carry

        lax.fori_loop(0, n_chunks // unroll, body, 0)

    def finish(z_ref, y_ref, n_rows):
        blk = min(256, n_rows)

        def body(t, carry):
            rows = pl.ds(pl.multiple_of(t * blk, blk), blk)
            o = of_ref[rows, :] + ob_ref[rows, :]
            z = z_ref[0, rows, :].astype(F32)
            gain = gain_ref[...]
            for h in range(2):
                lanes = slice(h * C_HEAD_V, (h + 1) * C_HEAD_V)
                y = _rms(o[:, lanes]) * gain[:, lanes]
                y_ref[0, rows, lanes] = (y * _silu(z[:, lanes])).astype(BF16)
            return carry

        lax.fori_loop(0, n_rows // blk, body, 0)

    stf_ref[...] = jnp.zeros_like(stf_ref)
    stb_ref[...] = jnp.zeros_like(stb_ref)
    gates(rc_ref, t_ctx)
    run(qc_ref, kc_ref, vc_ref, t_ctx // CH, want_ctx)
    if want_ctx:
        finish(zc_ref, yc_ref, t_ctx)
    gates(rl_ref, t_lat)
    run(ql_ref, kl_ref, vl_ref, t_lat // CH, True)
    finish(zl_ref, yl_ref, t_lat)


def _gla(p_lat, r_lat, p_ctx, r_ctx, w2, b2, gain, consts, want_ctx, ctx_cols=None):
    b, t_lat, _ = p_lat.shape
    t_ctx = p_ctx.shape[1]
    n_pairs = C_HEADS // 2

    def side(t, with_z, cols):
        z_rows = t if with_z else 16
        return [pl.BlockSpec((1, t, PAIR_K), lambda i, hp: (i, 0, cols["cq"] // PAIR_K + hp)),
                pl.BlockSpec((1, t, PAIR_K), lambda i, hp: (i, 0, cols["ck"] // PAIR_K + hp)),
                pl.BlockSpec((1, t, PAIR_V), lambda i, hp: (i, 0, cols["cv"] // PAIR_V + hp)),
                pl.BlockSpec((1, z_rows, PAIR_V), lambda i, hp: (i, 0, cols["cz"] // PAIR_V + hp)),
                pl.BlockSpec((1, t, LANES), lambda i, hp: (i, 0, 0))]

    def whole(shape):
        nd = len(shape)
        return pl.BlockSpec(shape, lambda i, hp: (0,) * nd)

    emat_f, emat_b, masks = consts
    specs = side(t_lat, True, FULL_COLS) + side(t_ctx, want_ctx, ctx_cols or FULL_COLS) + [
        pl.BlockSpec((1, LANES, 2 * PAIR_K), lambda i, hp: (hp, 0, 0)),
        pl.BlockSpec((1, 1, 2 * PAIR_K), lambda i, hp: (hp, 0, 0)),
        pl.BlockSpec((1, 1, PAIR_V), lambda i, hp: (hp, 0, 0)),
        whole(emat_f.shape), whole(emat_b.shape), whole(masks.shape)]
    out_specs = [pl.BlockSpec((1, t_lat, PAIR_V), lambda i, hp: (i, 0, hp))]
    out_shape = [jax.ShapeDtypeStruct((b, t_lat, BR_W), BF16)]
    if want_ctx:
        out_specs.append(pl.BlockSpec((1, t_ctx, PAIR_V), lambda i, hp: (i, 0, hp)))
        out_shape.append(jax.ShapeDtypeStruct((b, t_ctx, BR_W), BF16))

    def kern(*refs):
        refs = list(refs)
        for n in range(10, 13):
            refs[n] = refs[n].at[0]
        _gla_kernel(*refs, t_lat=t_lat, t_ctx=t_ctx, want_ctx=want_ctx)

    res = pl.pallas_call(
        kern,
        grid=(b, n_pairs),
        in_specs=specs,
        out_specs=out_specs,
        out_shape=out_shape,
        scratch_shapes=[pltpu.VMEM((C_HEAD_V, PAIR_K), F32), pltpu.VMEM((C_HEAD_V, PAIR_K), F32),
                        pltpu.VMEM((t_lat, PAIR_V), F32), pltpu.VMEM((t_lat, PAIR_V), F32),
                        pltpu.VMEM((t_lat, 2 * PAIR_K), BF16), pltpu.VMEM((t_lat, 2 * PAIR_K), BF16),
                        pltpu.VMEM((t_lat, 2 * PAIR_K), BF16)],
        compiler_params=_cparams(2),
        name="gla",
    )(p_lat, p_lat, p_lat, p_lat, r_lat, p_ctx, p_ctx, p_ctx, p_ctx, r_ctx,
      w2, b2, gain, emat_f, emat_b, masks)
    return (res[0], res[1]) if want_ctx else (res[0], None)


def _merge_kernel(*refs, alpha):
    n_mg = N_BRANCH * D_MODEL // MG_BLK
    mg_refs = refs[:n_mg]
    ya_ref, yb_ref, yc_ref, yd_ref, h_ref, gate_ref, wbr_ref, wout_ref, lng_ref, lnb_ref, o_ref = refs[n_mg:]
    per = D_MODEL // MG_BLK
    acc = None
    for n, y_ref in enumerate((ya_ref, yb_ref, yc_ref, yd_ref)):
        mg = jnp.concatenate([r[...] for r in mg_refs[n * per:(n + 1) * per]], axis=1)
        gate = _sigmoid(mg.astype(F32))
        term = gate * _dot(y_ref[...], wbr_ref[n])
        acc = term if acc is None else acc + term
    y = _dot(acc.astype(BF16), wout_ref[...])
    t = alpha * h_ref[...] + gate_ref[0] * y
    mu = jnp.mean(t, axis=-1, keepdims=True)
    tc = t - mu
    var = jnp.mean(tc * tc, axis=-1, keepdims=True)
    o_ref[...] = tc * lax.rsqrt(var + EPS) * lng_ref[...] + lnb_ref[...]


def _merge(p2, ya, yb, yc, yd, h, gate, rows_per_mod, w_br, w_out, ln_g, ln_b, alpha):
    m = h.shape[0]
    tm = min(512, m)
    assert m % tm == 0 and rows_per_mod % tm == 0
    tiles_per_mod = rows_per_mod // tm
    n_mg = N_BRANCH * D_MODEL // MG_BLK
    br_spec = pl.BlockSpec((tm, BR_W), lambda i: (i, 0))
    vec_spec = pl.BlockSpec((1, D_MODEL), lambda i: (0, 0))
    return pl.pallas_call(
        functools.partial(_merge_kernel, alpha=alpha),
        grid=(m // tm,),
        in_specs=[pl.BlockSpec((tm, MG_BLK), functools.partial(lambda n, i: (i, COL_MG // MG_BLK + n), n))
                  for n in range(n_mg)] + [
            br_spec, br_spec, br_spec, br_spec,
            pl.BlockSpec((tm, D_MODEL), lambda i: (i, 0)),
            pl.BlockSpec((1, 1, D_MODEL), lambda i: (i // tiles_per_mod, 0, 0)),
            pl.BlockSpec((N_BRANCH, BR_W, D_MODEL), lambda i: (0, 0, 0)),
            pl.BlockSpec((D_MODEL, D_MODEL), lambda i: (0, 0)),
            vec_spec, vec_spec,
        ],
        out_specs=pl.BlockSpec((tm, D_MODEL), lambda i: (i, 0)),
        out_shape=jax.ShapeDtypeStruct((m, D_MODEL), F32),
        compiler_params=_cparams(1),
        name="merge",
    )(*([p2] * n_mg), ya, yb, yc, yd, h, gate, w_br, w_out, ln_g.reshape(1, D_MODEL), ln_b.reshape(1, D_MODEL))


def _projection_weights(w_in):
    r0 = int(sum(IN_WIDTHS[:12]))
    r1 = r0 + 2 * C_GATE_RANK
    w_main = jnp.concatenate([w_in[:, :, :r0], w_in[:, :, r1:]], axis=-1).astype(BF16)
    w_r = jnp.pad(w_in[:, :, r0:r1], ((0, 0), (0, 0), (0, LANES - 2 * C_GATE_RANK))).astype(BF16)
    return w_main, w_r


def _gate_weights(c_w2_l, c_b2_l):
    n_pairs = C_HEADS // 2

    def gate_w(i):
        w = jnp.pad(c_w2_l[i], ((i * C_GATE_RANK, LANES - (i + 1) * C_GATE_RANK), (0, 0)))
        return w.reshape(LANES, n_pairs, PAIR_K).transpose(1, 0, 2)

    w2 = jnp.concatenate([gate_w(0), gate_w(1)], axis=-1).astype(BF16)
    b2 = jnp.concatenate([c_b2_l[0].reshape(n_pairs, 1, PAIR_K), c_b2_l[1].reshape(n_pairs, 1, PAIR_K)], axis=-1)
    return w2, b2


def _rope_tables(t):
    rows = t // GRID_W
    row = np.repeat(np.arange(rows), GRID_W).astype(np.float32)
    col = np.tile(np.arange(GRID_W), rows).astype(np.float32)
    inv = (ROPE_THETA ** (-np.arange(0, AXIS_DIM, 2, dtype=np.float32) / AXIS_DIM)).astype(np.float32)
    ang = np.concatenate([row[:, None] * inv, col[:, None] * inv], -1).astype(np.float64)
    cos, sin = np.repeat(np.cos(ang), 2, axis=-1), np.repeat(np.sin(ang), 2, axis=-1)
    sin[:, 0::2] *= -1.0
    return jnp.asarray(cos, F32), jnp.asarray(sin, F32)


def kernel(x, c, ctx, c_ctx, w_mod, b_mod, w_in, q_norm, k_norm, b_conv, c_gate_w2, c_gate_b, c_norm,
           d_conv_w, d_conv_b, d_norm_g, d_norm_b, w_br, w_out, ln_g, ln_b):
    b, t, d = x.shape
    t_ctx = ctx.shape[1]
    depth = w_mod.shape[0]
    assert d == D_MODEL and b < 16
    alpha = (2 * depth) ** 0.25

    cc = jnp.concatenate([c, c_ctx[None], jnp.zeros((16 - b - 1, D_MODEL), F32)], axis=0)
    mod = _modulation(cc, w_mod, b_mod)
    cos, sin = _rope_tables(t)
    consts = _gla_constants()
    w_main, w_r = _projection_weights(w_in)
    w_br_bf, w_out_bf = w_br.astype(BF16), w_out.astype(BF16)
    n_pairs = C_HEADS // 2

    h_lat = x.reshape(b * t, D_MODEL)
    h_ctx = ctx.reshape(b * t_ctx, D_MODEL)
    for l in range(depth):
        want_ctx = l < depth - 1
        w2, b2 = _gate_weights(c_gate_w2[l], c_gate_b[l])
        q_gain, k_gain = q_norm[l].reshape(1, HEAD_DIM), k_norm[l].reshape(1, HEAD_DIM)
        c_gain = c_norm[l].reshape(n_pairs, 1, PAIR_V)
        shift, scale, gate = [mod[l, :, n * D_MODEL:(n + 1) * D_MODEL].reshape(16, 1, D_MODEL) for n in range(3)]
        p_lat, r_lat = _in_projection(h_lat, shift, scale, t, w_main, w_r, l)
        ctx_tiles = None if want_ctx else tuple(c // CTX_KV_TN for c in CTX_KV_TILES)
        ctx_cols = FULL_COLS if want_ctx else CTX_KV_COLS
        p_ctx, r_ctx = _in_projection(h_ctx, shift[b:], scale[b:], b * t_ctx, w_main, w_r, l, ctx_tiles,
                                      PROJ_TN if want_ctx else CTX_KV_TN)
        p_lat3 = p_lat.reshape(b, t, N_MAIN)
        p_ctx3 = p_ctx.reshape(b, t_ctx, p_ctx.shape[1])
        r_lat3 = r_lat.reshape(b, t, LANES)
        r_ctx3 = r_ctx.reshape(b, t_ctx, LANES)
        w_br_l, w_out_l = w_br_bf[l], w_out_bf[l]

        ya_l = _attention(p_lat3, p_lat3, p_ctx3, q_gain, k_gain, cos, sin, ctx_cols)
        yb_l, yd_l = _local_branches(p_lat, t, b_conv[l], d_conv_w[l], d_conv_b[l], d_norm_g[l], d_norm_b[l])
        yc_l, yc_c = _gla(p_lat3, r_lat3, p_ctx3, r_ctx3, w2, b2, c_gain, consts, want_ctx, ctx_cols)
        h_lat_new = _merge(p_lat, ya_l.reshape(b * t, BR_W), yb_l, yc_l.reshape(b * t, BR_W), yd_l,
                           h_lat, gate, t, w_br_l, w_out_l, ln_g[l], ln_b[l], alpha)
        if want_ctx:
            ya_c = _attention(p_ctx3, None, p_ctx3, q_gain, k_gain, None, None)
            yb_c, yd_c = _local_branches(p_ctx, t_ctx, b_conv[l], d_conv_w[l], d_conv_b[l], d_norm_g[l],
                                         d_norm_b[l])
            h_ctx = _merge(p_ctx, ya_c.reshape(b * t_ctx, BR_W), yb_c, yc_c.reshape(b * t_ctx, BR_W), yd_c,
                           h_ctx, gate[b:], b * t_ctx, w_br_l, w_out_l, ln_g[l], ln_b[l], alpha)
        h_lat = h_lat_new
    return h_lat.reshape(b, t, D_MODEL)
```

```python
import functools

import numpy as np
import jax
import jax.numpy as jnp
from jax import lax
from jax.experimental import pallas as pl
from jax.experimental.pallas import tpu as pltpu

F32 = jnp.float32
BF16 = jnp.bfloat16

D_MODEL = 1024
GRID_W = 64
N_BRANCH = 4
BR_W = D_MODEL // 2
HEAD_DIM = 128
A_HEADS = BR_W // HEAD_DIM
A_KV_HEADS = A_HEADS // 2
ROPE_THETA = 10000.0
AXIS_DIM = HEAD_DIM // 2
B_CONV = 3
C_HEADS = 4
C_HEAD_K = BR_W // (2 * C_HEADS)
C_HEAD_V = BR_W // C_HEADS
C_KEY_W = C_HEADS * C_HEAD_K
C_GATE_RANK = 16
C_GATE_TAU = 16.0
C_CHUNK = 64
D_CONV = 31
EPS = 1e-6
IN_WIDTHS = (
    A_HEADS * HEAD_DIM, A_KV_HEADS * HEAD_DIM, A_KV_HEADS * HEAD_DIM, BR_W,
    BR_W, BR_W, BR_W, BR_W,
    C_KEY_W, C_KEY_W, C_HEADS * C_HEAD_V, BR_W, 2 * C_GATE_RANK,
    2 * BR_W, BR_W,
    N_BRANCH * D_MODEL,
)

COL_AQ = 0
COL_AK = COL_AQ + 512
COL_AV = COL_AK + 256
COL_AZ = COL_AV + 256
COL_BG = COL_AZ + 512
COL_BC = COL_BG + 512
COL_BX = COL_BC + 512
COL_BZ = COL_BX + 512
COL_CQ = COL_BZ + 512
COL_CK = COL_CQ + 256
COL_CV = COL_CK + 256
COL_CZ = COL_CV + 512
COL_DA = COL_CZ + 512
COL_DG = COL_DA + 512
COL_DZ = COL_DG + 512
COL_MG = COL_DZ + 512
MG_BLK = 512
N_MAIN = COL_MG + N_BRANCH * D_MODEL
FULL_COLS = dict(ak=COL_AK, av=COL_AV, cq=COL_CQ, ck=COL_CK, cv=COL_CV, cz=COL_CZ)
CTX_KV_TILES = (COL_AK, COL_CQ, COL_CV)
CTX_KV_COLS = dict(ak=0, av=256, cq=512, ck=768, cv=1024, cz=1024)
LANES = 128
SUBLANES = 8
PROJ_TN = 1536
CTX_KV_TN = 512
LN_GROUP = 512
VMEM_LIMIT = 56 * 1024 * 1024
LOG2E = 1.4426950408889634


def _cparams(n_axes):
    return pltpu.CompilerParams(dimension_semantics=("arbitrary",) * n_axes,
                                vmem_limit_bytes=VMEM_LIMIT)


def _sigmoid(x):
    return 0.5 * jnp.tanh(0.5 * x) + 0.5


def _silu(x):
    h = 0.5 * x
    return h + h * jnp.tanh(h)


def _dot(a, b):
    return jnp.dot(a, b, preferred_element_type=F32)


def _dot_nt(a, b):
    return lax.dot_general(a, b, (((1,), (1,)), ((), ())), preferred_element_type=F32)


def _dot_tn(a, b):
    return lax.dot_general(a, b, (((0,), (0,)), ((), ())), preferred_element_type=F32)


def _mod_kernel(c_ref, w_ref, b_ref, o_ref):
    s = _silu(c_ref[...])
    o_ref[0] = _dot(s.astype(BF16), w_ref[0].astype(BF16)) + b_ref[0]


def _modulation(cc, w_mod, b_mod):
    depth = w_mod.shape[0]
    n_rows = cc.shape[0]
    return pl.pallas_call(
        _mod_kernel,
        grid=(depth, 3),
        in_specs=[
            pl.BlockSpec((n_rows, D_MODEL), lambda l, j: (0, 0)),
            pl.BlockSpec((1, D_MODEL, D_MODEL), lambda l, j: (l, 0, j)),
            pl.BlockSpec((1, 1, D_MODEL), lambda l, j: (l, 0, j)),
        ],
        out_specs=pl.BlockSpec((1, n_rows, D_MODEL), lambda l, j: (l, 0, j)),
        out_shape=jax.ShapeDtypeStruct((depth, n_rows, 3 * D_MODEL), F32),
        compiler_params=_cparams(2),
        name="modulation",
    )(cc, w_mod, b_mod.reshape(depth, 1, 3 * D_MODEL))


def _inproj_kernel(x_ref, shift_ref, scale_ref, w_ref, wr_ref, o_ref, or_ref, u_ref, *, tm, sub):
    j = pl.program_id(1)

    @pl.when(j == 0)
    def _():
        one_plus = 1.0 + scale_ref[0]
        shift = shift_ref[0]
        grp = min(LN_GROUP, tm)
        for g0 in range(0, tm, grp):
            for r0 in range(g0, g0 + grp, sub):
                x = x_ref[r0:r0 + sub, :]
                mu = jnp.mean(x, axis=-1, keepdims=True)
                xc = x - mu
                var = jnp.mean(xc * xc, axis=-1, keepdims=True)
                u = xc * lax.rsqrt(var + EPS) * one_plus + shift
                u_ref[r0:r0 + sub, :] = u.astype(BF16)
            u_grp = u_ref[g0:g0 + grp, :]
            or_ref[g0:g0 + grp, :] = _dot(u_grp, wr_ref[...]).astype(BF16)
            o_ref[g0:g0 + grp, :] = _dot(u_grp, w_ref[...]).astype(BF16)

    @pl.when(j != 0)
    def _():
        o_ref[...] = _dot(u_ref[...], w_ref[...]).astype(BF16)


def _in_projection(h, shift, scale, rows_per_mod, w_main, w_r, layer, tiles=None, tn=PROJ_TN):
    m = h.shape[0]
    tm = min(2048, rows_per_mod)
    assert m % tm == 0 and rows_per_mod % tm == 0
    tiles_per_mod = rows_per_mod // tm
    assert w_main.shape[2] % tn == 0
    n_tiles = len(tiles) if tiles else w_main.shape[2] // tn
    n_cols = n_tiles * tn

    def w_tile(j):
        if not tiles:
            return j
        idx = tiles[0]
        for n in range(1, len(tiles)):
            idx = idx + (tiles[n] - tiles[n - 1]) * jnp.minimum(jnp.maximum(j - n + 1, 0), 1)
        return idx

    kern = functools.partial(_inproj_kernel, tm=tm, sub=min(128, tm))
    return pl.pallas_call(
        kern,
        grid=(m // tm, n_tiles),
        in_specs=[
            pl.BlockSpec((tm, D_MODEL), lambda i, j: (i, 0)),
            pl.BlockSpec((1, 1, D_MODEL), lambda i, j: (i // tiles_per_mod, 0, 0)),
            pl.BlockSpec((1, 1, D_MODEL), lambda i, j: (i // tiles_per_mod, 0, 0)),
            pl.BlockSpec((None, D_MODEL, tn), lambda i, j: (layer, 0, w_tile(j))),
            pl.BlockSpec((None, D_MODEL, LANES), lambda i, j: (layer, 0, 0)),
        ],
        out_specs=[
            pl.BlockSpec((tm, tn), lambda i, j: (i, j)),
            pl.BlockSpec((tm, LANES), lambda i, j: (i, 0)),
        ],
        out_shape=[
            jax.ShapeDtypeStruct((m, n_cols), BF16),
            jax.ShapeDtypeStruct((m, LANES), BF16),
        ],
        scratch_shapes=[pltpu.VMEM((tm, D_MODEL), BF16)],
        compiler_params=_cparams(2),
        name="in_projection",
    )(h, shift, scale, w_main, w_r)


ATTN_KEY_BLOCKS = (768, 512, 256)
QK_AHEAD = 2


def _rms(x):
    return x * lax.rsqrt(jnp.mean(x * x, axis=-1, keepdims=True) + EPS)


def _rope(x, cos, sin):
    lane = lax.broadcasted_iota(jnp.int32, x.shape, 1)
    swapped = jnp.where(lane % 2 == 0, pltpu.roll(x, HEAD_DIM - 1, 1), pltpu.roll(x, 1, 1))
    return x * cos + swapped * sin


def _attn_kernel(*refs, n_lat, n_ctx, tq, kblk, kstep):
    if n_lat:
        (q_ref, z_ref, kl_ref, vl_ref, kc_ref, vc_ref, qg_ref, kg_ref, cos_ref, sin_ref,
         o_ref, kn_ref, vn_ref) = refs
    else:
        q_ref, z_ref, kc_ref, vc_ref, qg_ref, kg_ref, o_ref, kn_ref, vn_ref = refs
    qi = pl.program_id(2)
    kg = kg_ref[...]

    @pl.when(qi == 0)
    def _prep():
        if n_lat:
            def body(t, carry):
                r0 = pl.multiple_of(t * kblk, kblk)
                kn = _rms(kl_ref[0, pl.ds(r0, kblk), :].astype(F32)) * kg
                kn = _rope(kn, cos_ref[pl.ds(r0, kblk), :], sin_ref[pl.ds(r0, kblk), :])
                kn_ref[pl.ds(r0, kblk), :] = kn.astype(BF16)
                vn_ref[pl.ds(r0, kblk), 0:HEAD_DIM] = vl_ref[0, pl.ds(r0, kblk), :]
                return carry

            lax.fori_loop(0, n_lat // kblk, body, 0)
        kn_ref[n_lat:n_lat + n_ctx, :] = (_rms(kc_ref[0].astype(F32)) * kg).astype(BF16)
        vn_ref[n_lat:n_lat + n_ctx, 0:HEAD_DIM] = vc_ref[0]
        vn_ref[:, HEAD_DIM:2 * HEAD_DIM] = jnp.ones((n_lat + n_ctx, HEAD_DIM), BF16)

    qg = qg_ref[...]
    if n_lat:
        q0 = pl.multiple_of(qi * tq, tq)
        cos = cos_ref[pl.ds(q0, tq), :]
        sin = sin_ref[pl.ds(q0, tq), :]
    qns = []
    for g in range(2):
        qn = _rms(q_ref[0, :, g * HEAD_DIM:(g + 1) * HEAD_DIM].astype(F32)) * qg
        if n_lat:
            qn = _rope(qn, cos, sin)
        qns.append((qn * (HEAD_DIM ** -0.5 * LOG2E)).astype(BF16))

    items = [(g, k0) for k0 in range(0, n_lat + n_ctx, kstep) for g in range(2)]

    def scores(item):
        g, k0 = item
        return _dot_nt(qns[g], kn_ref[k0:k0 + kstep, :])

    m = [None, None]
    acc = [None, None]
    ahead = [scores(item) for item in items[:QK_AHEAD]]
    for n, (g, k0) in enumerate(items):
        s = ahead.pop(0)
        if n + QK_AHEAD < len(items):
            ahead.append(scores(items[n + QK_AHEAD]))
        m_blk = jnp.max(s, axis=-1, keepdims=True)
        m_new = m_blk if m[g] is None else jnp.maximum(m[g], m_blk)
        p = jnp.exp2((s - m_new).astype(BF16))
        pv = _dot(p, vn_ref[k0:k0 + kstep, :])
        acc[g] = pv if acc[g] is None else acc[g] * jnp.exp2(m[g] - m_new) + pv
        m[g] = m_new
    for g in range(2):
        lanes = slice(g * HEAD_DIM, (g + 1) * HEAD_DIM)
        o = acc[g][:, 0:HEAD_DIM] / acc[g][:, HEAD_DIM:2 * HEAD_DIM]
        o_ref[0, :, lanes] = (o * _silu(z_ref[0, :, lanes].astype(F32))).astype(BF16)


def _attention(p_q, p_lat, p_ctx, q_gain, k_gain, cos, sin, ctx_cols=None):
    b, t_q, _ = p_q.shape
    n_ctx = p_ctx.shape[1]
    n_lat = 0 if p_lat is None else p_lat.shape[1]
    tq = min(512, t_q)
    assert t_q % tq == 0
    kblk = min(1024, n_lat) if n_lat else 0
    n_keys = n_lat + n_ctx
    kstep = next((c for c in ATTN_KEY_BLOCKS if n_keys % c == 0), n_keys)
    hw = 2 * HEAD_DIM
    q_spec = pl.BlockSpec((1, tq, hw), lambda i, kv, qi: (i, qi, COL_AQ // hw + kv))
    z_spec = pl.BlockSpec((1, tq, hw), lambda i, kv, qi: (i, qi, COL_AZ // hw + kv))

    def kv_spec(n, col):
        return pl.BlockSpec((1, n, HEAD_DIM), lambda i, kv, qi: (i, 0, col // HEAD_DIM + kv))

    gain_spec = pl.BlockSpec((1, HEAD_DIM), lambda i, kv, qi: (0, 0))
    args = [p_q, p_q]
    specs = [q_spec, z_spec]
    if n_lat:
        args += [p_lat, p_lat]
        specs += [kv_spec(n_lat, COL_AK), kv_spec(n_lat, COL_AV)]
    ctx_cols = ctx_cols or FULL_COLS
    args += [p_ctx, p_ctx, q_gain, k_gain]
    specs += [kv_spec(n_ctx, ctx_cols["ak"]), kv_spec(n_ctx, ctx_cols["av"]), gain_spec, gain_spec]
    if n_lat:
        tab_spec = pl.BlockSpec((n_lat, HEAD_DIM), lambda i, kv, qi: (0, 0))
        args += [cos, sin]
        specs += [tab_spec, tab_spec]
    kern = functools.partial(_attn_kernel, n_lat=n_lat, n_ctx=n_ctx, tq=tq, kblk=kblk, kstep=kstep)
    return pl.pallas_call(
        kern,
        grid=(b, A_KV_HEADS, t_q // tq),
        in_specs=specs,
        out_specs=pl.BlockSpec((1, tq, hw), lambda i, kv, qi: (i, qi, kv)),
        out_shape=jax.ShapeDtypeStruct((b, t_q, BR_W), BF16),
        scratch_shapes=[pltpu.VMEM((n_lat + n_ctx, HEAD_DIM), BF16),
                        pltpu.VMEM((n_lat + n_ctx, 2 * HEAD_DIM), BF16)],
        compiler_params=_cparams(3),
        name="attention_lat" if n_lat else "attention_ctx",
    )(*args)


D_HALO = 16
B_HALO = 16
CONV_RB = 32
CONV_ACCS = 4
ELEM_RB = 32
COPY_RB = 56


def _local_kernel(bg_ref, bc_ref, bcp_ref, bcn_ref, bx_ref, bxp_ref, bxn_ref, bz_ref,
                  da_ref, dap_ref, dan_ref, dg_ref, dgp_ref, dgn_ref, dz_ref,
                  bw_ref, dw_ref, db_ref, dgain_ref, dbeta_ref,
                  yb_ref, yd_ref, tbuf, gbuf, hbuf, *, tt, tiles_per_seq):
    i = pl.program_id(0)
    pos = i % tiles_per_seq
    keep_prev = (pos != 0).astype(F32)
    keep_next = (pos != tiles_per_seq - 1).astype(F32)

    def ld(ref, r0=0, n=None):
        return ref[r0:r0 + (n or ref.shape[0]), :].astype(F32)

    row_blocks = range(0, tt, ELEM_RB)

    tbuf[0:B_HALO, :] = ld(bcp_ref) * ld(bxp_ref) * keep_prev
    for r0 in row_blocks:
        tbuf[B_HALO + r0:B_HALO + r0 + ELEM_RB, :] = ld(bc_ref, r0, ELEM_RB) * ld(bx_ref, r0, ELEM_RB)
    tbuf[B_HALO + tt:2 * B_HALO + tt, :] = ld(bcn_ref) * ld(bxn_ref) * keep_next
    for r0 in row_blocks:
        lo = B_HALO + r0
        conv = (bw_ref[0:1, :] * tbuf[lo - 1:lo - 1 + ELEM_RB, :] + bw_ref[1:2, :] * tbuf[lo:lo + ELEM_RB, :]
                + bw_ref[2:3, :] * tbuf[lo + 1:lo + 1 + ELEM_RB, :])
        yb_ref[r0:r0 + ELEM_RB, :] = (ld(bg_ref, r0, ELEM_RB) * conv
                                      * _silu(ld(bz_ref, r0, ELEM_RB))).astype(BF16)

    gbuf[0, 0:D_HALO, :] = ld(dap_ref) * _sigmoid(ld(dgp_ref)) * keep_prev
    for r0 in row_blocks:
        gbuf[0, D_HALO + r0:D_HALO + r0 + ELEM_RB, :] = (ld(da_ref, r0, ELEM_RB)
                                                          * _sigmoid(ld(dg_ref, r0, ELEM_RB)))
    gbuf[0, D_HALO + tt:2 * D_HALO + tt, :] = ld(dan_ref) * _sigmoid(ld(dgn_ref)) * keep_next
    span = tt + 2 * D_HALO - SUBLANES
    for r in range(1, SUBLANES):
        for x0 in range(0, span, COPY_RB):
            n = min(COPY_RB, span - x0)
            gbuf[r, x0:x0 + n, :] = gbuf[0, r + x0:r + x0 + n, :]
    base = D_HALO - D_CONV // 2
    for cb in range(BR_W // LANES):
        lanes = slice(cb * LANES, (cb + 1) * LANES)
        for rb in range(tt // CONV_RB):
            accs = [None] * CONV_ACCS
            for k in range(D_CONV):
                off = base + k
                r0 = rb * CONV_RB + off - off % SUBLANES
                rows = gbuf[off % SUBLANES, r0:r0 + CONV_RB, lanes]
                term = rows.reshape(CONV_RB // SUBLANES, SUBLANES, LANES) * dw_ref[k, :, lanes][None]
                a = k % CONV_ACCS
                accs[a] = term if accs[a] is None else accs[a] + term
            while len(accs) > 1:
                accs = [accs[n] + accs[n + 1] for n in range(0, len(accs), 2)]
            hbuf[rb * CONV_RB:(rb + 1) * CONV_RB, lanes] = accs[0].reshape(CONV_RB, LANES)
    for r0 in row_blocks:
        hh = hbuf[r0:r0 + ELEM_RB, :] + db_ref[...]
        mu = jnp.mean(hh, axis=-1, keepdims=True)
        hc = hh - mu
        var = jnp.mean(hc * hc, axis=-1, keepdims=True)
        hn = hc * lax.rsqrt(var + EPS) * dgain_ref[...] + dbeta_ref[...]
        yd_ref[r0:r0 + ELEM_RB, :] = (_silu(hn) * _silu(ld(dz_ref, r0, ELEM_RB))).astype(BF16)


def _local_branches(p2, seq_len, b_w, d_w, d_b, d_g, d_beta):
    m = p2.shape[0]
    tt = min(512, seq_len)
    assert seq_len % tt == 0 and m % seq_len == 0
    tiles_per_seq = seq_len // tt
    n_tiles = m // tt

    def cur(col):
        return pl.BlockSpec((tt, BR_W), lambda i: (i, col // BR_W))

    def prev(col, halo):
        per = tt // halo
        return pl.BlockSpec((halo, BR_W), lambda i: (jnp.maximum(i * per - 1, 0), col // BR_W))

    def nxt(col, halo):
        per = tt // halo
        last = m // halo - 1
        return pl.BlockSpec((halo, BR_W), lambda i: (jnp.minimum((i + 1) * per, last), col // BR_W))

    def small(rows):
        return pl.BlockSpec((rows, BR_W), lambda i: (0, 0))

    specs = [cur(COL_BG),
             cur(COL_BC), prev(COL_BC, B_HALO), nxt(COL_BC, B_HALO),
             cur(COL_BX), prev(COL_BX, B_HALO), nxt(COL_BX, B_HALO),
             cur(COL_BZ),
             cur(COL_DA), prev(COL_DA, D_HALO), nxt(COL_DA, D_HALO),
             cur(COL_DG), prev(COL_DG, D_HALO), nxt(COL_DG, D_HALO),
             cur(COL_DZ),
             small(B_CONV), pl.BlockSpec((D_CONV, SUBLANES, BR_W), lambda i: (0, 0, 0)),
             small(1), small(1), small(1)]
    kern = functools.partial(_local_kernel, tt=tt, tiles_per_seq=tiles_per_seq)
    out_spec = pl.BlockSpec((tt, BR_W), lambda i: (i, 0))
    return pl.pallas_call(
        kern,
        grid=(n_tiles,),
        in_specs=specs,
        out_specs=[out_spec, out_spec],
        out_shape=[jax.ShapeDtypeStruct((m, BR_W), BF16), jax.ShapeDtypeStruct((m, BR_W), BF16)],
        scratch_shapes=[pltpu.VMEM((tt + 2 * B_HALO, BR_W), F32),
                        pltpu.VMEM((SUBLANES, tt + 2 * D_HALO, BR_W), F32),
                        pltpu.VMEM((tt, BR_W), F32)],
        compiler_params=_cparams(1),
        name="local_branches",
    )(*([p2] * 15), b_w, jnp.broadcast_to(d_w[:, None, :], (D_CONV, SUBLANES, BR_W)),
      d_b.reshape(1, BR_W), d_g.reshape(1, BR_W), d_beta.reshape(1, BR_W))


CH = C_CHUNK
PAIR_K = 2 * C_HEAD_K
PAIR_V = 2 * C_HEAD_V
LEVELS = (8, 4, 2)
BLK = SUBLANES
N_BLK = CH // BLK
GLA_UNROLL = 8
ROW_EQ = 0
ROW_EK = CH
ROW_LAST = 2 * CH
ROW_FK = 2 * CH + 8
ROW_LVL = ROW_FK + CH
ROW_FQ = ROW_LVL + len(LEVELS) * CH
N_FQ = BLK * (N_BLK * (N_BLK - 1) // 2)
N_EROWS = ROW_FQ + N_FQ


def _slabs(reverse):
    if reverse:
        return [(jb, 0, BLK * jb) for jb in range(1, N_BLK)]
    return [(jb, BLK * (jb + 1), CH - BLK * (jb + 1)) for jb in range(N_BLK - 1)]


def _gla_constants():
    idx = np.arange(CH)
    cols = np.arange(2 * CH) % CH
    emats, masks = [], []
    for reverse in (False, True):
        tri = (idx[None, :] >= idx[:, None]) if reverse else (idx[None, :] <= idx[:, None])
        tri = tri.astype(np.float32)
        edge = tri[0] if reverse else tri[CH - 1]
        blk_edge = (idx // BLK) * BLK + (0 if reverse else BLK - 1)
        blocks = [tri, edge[None, :] - tri, np.tile(edge[None, :], (8, 1)), tri[blk_edge] - tri]
        lvl_masks = []
        for grp in LEVELS:
            half = grp // 2
            in_q = (idx % grp < half) if reverse else (idx % grp >= half)
            ref = (idx // grp) * grp + (half if reverse else half - 1)
            d = tri - tri[ref]
            blocks.append(np.where(in_q[:, None], d, -d))
            in_q_col = (cols % grp < half) if reverse else (cols % grp >= half)
            lvl_masks.append((idx[:, None] // grp == cols[None, :] // grp) & in_q[:, None] & ~in_q_col[None, :])
        lvl_masks.append(idx[:, None] == cols[None, :])
        for jb, q0, nq in _slabs(reverse):
            edge_row = BLK * jb + (0 if reverse else BLK - 1)
            blocks.append(tri[q0:q0 + nq] - tri[edge_row][None, :])
        emat = np.concatenate(blocks, axis=0)
        assert emat.shape == (N_EROWS, CH) and emat.min() >= -1 and emat.max() <= 1
        emats.append(np.concatenate([emat] * 3, axis=1))
        masks.append(np.stack(lvl_masks).astype(np.float32))
    return (jnp.asarray(emats[0], BF16), jnp.asarray(emats[1], BF16), jnp.asarray(np.stack(masks), F32))


def _split3(g):
    g1 = g.astype(BF16)
    r1 = g - g1.astype(F32)
    g2 = r1.astype(BF16)
    g3 = (r1 - g2.astype(F32)).astype(BF16)
    return g1, g2, g3


def _log2_sigmoid(x):
    e = jnp.exp2(jnp.abs(x) * -LOG2E)
    return jnp.minimum(x, 0.0) * LOG2E - jnp.log2(1.0 + e)


def _gla_kernel(*refs, t_lat, t_ctx, want_ctx):
    (ql_ref, kl_ref, vl_ref, zl_ref, rl_ref, qc_ref, kc_ref, vc_ref, zc_ref, rc_ref,
     w2_ref, b2_ref, gain_ref, ematf_ref, ematb_ref, masks_ref) = refs[:16]
    if want_ctx:
        yl_ref, yc_ref = refs[16:18]
        scratch = refs[18:]
    else:
        yl_ref = refs[16]
        scratch = refs[17:]
        yc_ref = zc_ref = None
    stf_ref, stb_ref, of_ref, ob_ref, g1_ref, g2_ref, g3_ref = scratch

    def gates(r_ref, n_rows):
        blk = min(512, n_rows)

        def body(t, carry):
            rows = pl.ds(pl.multiple_of(t * blk, blk), blk)
            x = _dot(r_ref[0, rows, :], w2_ref[...]) + b2_ref[...]
            g1, g2, g3 = _split3(_log2_sigmoid(x) * (1.0 / C_GATE_TAU))
            g1_ref[rows, :] = g1
            g2_ref[rows, :] = g2
            g3_ref[rows, :] = g3
            return carry

        lax.fori_loop(0, n_rows // blk, body, 0)

    def run(q_ref, k_ref, v_ref, n_chunks, want_out):
        unroll = min(GLA_UNROLL, n_chunks)
        assert n_chunks % unroll == 0
        lane_k = lax.broadcasted_iota(jnp.int32, (CH, PAIR_K), 1)
        head0 = lane_k < C_HEAD_K
        zero_v = jnp.zeros((CH, C_HEAD_V), BF16)
        key_blk = (lax.broadcasted_iota(jnp.int32, (BLK, 2 * CH), 1) % CH) // BLK

        def body(i, carry):
            work = []
            for reverse in (False, True):
                lanes = slice(PAIR_K, 2 * PAIR_K) if reverse else slice(0, PAIR_K)
                rows = []
                for u in range(unroll):
                    c = i * unroll + u
                    c = (n_chunks - 1 - c) if reverse else c
                    rows.append(pl.ds(pl.multiple_of(c * CH, CH), CH))
                rhs = jnp.concatenate(
                    [jnp.concatenate([g_ref[r, lanes] for r in rows], axis=1)
                     for g_ref in (g1_ref, g2_ref, g3_ref)], axis=0)
                emat = ematb_ref[...] if reverse else ematf_ref[...]
                e_all = jnp.exp2(_dot(emat, rhs))
                for u in range(unroll):
                    work.append(dict(reverse=reverse, rows=rows[u], u=u,
                                     e=e_all[:, u * PAIR_K:(u + 1) * PAIR_K]))
            work.sort(key=lambda w: w["u"])

            for w in work:
                st_ref = stb_ref if w["reverse"] else stf_ref
                e = w["e"]
                q = q_ref[0, w["rows"], :].astype(F32) * C_HEAD_K ** -0.5
                k = k_ref[0, w["rows"], :].astype(F32)
                v = v_ref[0, w["rows"], :]
                k0 = jnp.where(head0, k, 0.0)
                k1 = jnp.where(head0, 0.0, k)
                st = st_ref[...]
                e_k = e[ROW_EK:ROW_EK + CH]
                k_dec = jnp.concatenate([k0 * e_k, k1 * e_k], axis=0).astype(BF16)
                v_rows = jnp.concatenate([v[:, :C_HEAD_V], v[:, C_HEAD_V:]], axis=0)
                st_ref[...] = e[ROW_LAST:ROW_LAST + 1] * st + _dot_tn(v_rows, k_dec)
                if want_out:
                    qe = q * e[ROW_EQ:ROW_EQ + CH]
                    q_rows = jnp.concatenate([jnp.where(head0, qe, 0.0), jnp.where(head0, 0.0, qe)], axis=0)
                    o_st = _dot(q_rows.astype(BF16), st.T.astype(BF16))
                    w.update(q=q, k0=k0, k1=k1, v=v, o=jnp.concatenate([o_st[:CH], o_st[CH:]], axis=1))
            if not want_out:
                return carry

            for w in work:
                masks = masks_ref.at[1 if w["reverse"] else 0]
                q, k0, k1, e = w["q"], w["k0"], w["k1"], w["e"]
                fk = e[ROW_FK:ROW_FK + CH]
                kk_t = jnp.concatenate([k0 * fk, k1 * fk], axis=0).T.astype(BF16)
                slabs = _slabs(w["reverse"])
                lhs, off = [], ROW_FQ
                for jb, q0, nq in slabs:
                    lhs.append(q[q0:q0 + nq] * e[off:off + nq])
                    off += nq
                r = _dot(jnp.concatenate(lhs, axis=0).astype(BF16), kk_t)
                rows, off = [jnp.zeros((BLK, 2 * CH), F32)] * N_BLK, 0
                for jb, q0, nq in slabs:
                    for ib in range(q0 // BLK, (q0 + nq) // BLK):
                        piece = r[off + ib * BLK - q0:off + (ib + 1) * BLK - q0]
                        rows[ib] = jnp.where(key_blk == jb, piece, rows[ib])
                    off += nq
                scores = jnp.concatenate(rows, axis=0)
                k_heads = jnp.concatenate([k0, k1], axis=0)
                scores = scores + _dot(q.astype(BF16), k_heads.T.astype(BF16)) * masks[len(LEVELS)]
                for lvl in range(len(LEVELS)):
                    e_l = e[ROW_LVL + lvl * CH:ROW_LVL + (lvl + 1) * CH]
                    kk_t = jnp.concatenate([k0 * e_l, k1 * e_l], axis=0).T.astype(BF16)
                    scores = scores + _dot((q * e_l).astype(BF16), kk_t) * masks[lvl]
                w["scores"] = scores.astype(BF16)

            for w in work:
                v = w["v"]
                v_bd = jnp.concatenate([jnp.concatenate([v[:, :C_HEAD_V], zero_v], axis=1),
                                        jnp.concatenate([zero_v, v[:, C_HEAD_V:]], axis=1)], axis=0)
                (ob_ref if w["reverse"] else of_ref)[w["rows"], :] = w["o"] + _dot(w["scores"], v_bd)
            return carry

        lax.fori_loop(0, n_chunks // unroll, body, 0)

    def finish(z_ref, y_ref, n_rows):
        blk = min(256, n_rows)

        def body(t, carry):
            rows = pl.ds(pl.multiple_of(t * blk, blk), blk)
            o = of_ref[rows, :] + ob_ref[rows, :]
            z = z_ref[0, rows, :].astype(F32)
            gain = gain_ref[...]
            for h in range(2):
                lanes = slice(h * C_HEAD_V, (h + 1) * C_HEAD_V)
                y = _rms(o[:, lanes]) * gain[:, lanes]
                y_ref[0, rows, lanes] = (y * _silu(z[:, lanes])).astype(BF16)
            return carry

        lax.fori_loop(0, n_rows // blk, body, 0)

    stf_ref[...] = jnp.zeros_like(stf_ref)
    stb_ref[...] = jnp.zeros_like(stb_ref)
    gates(rc_ref, t_ctx)
    run(qc_ref, kc_ref, vc_ref, t_ctx // CH, want_ctx)
    if want_ctx:
        finish(zc_ref, yc_ref, t_ctx)
    gates(rl_ref, t_lat)
    run(ql_ref, kl_ref, vl_ref, t_lat // CH, True)
    finish(zl_ref, yl_ref, t_lat)


def _gla(p_lat, r_lat, p_ctx, r_ctx, w2, b2, gain, consts, want_ctx, ctx_cols=None):
    b, t_lat, _ = p_lat.shape
    t_ctx = p_ctx.shape[1]
    n_pairs = C_HEADS // 2

    def side(t, with_z, cols):
        z_rows = t if with_z else 16
        return [pl.BlockSpec((1, t, PAIR_K), lambda i, hp: (i, 0, cols["cq"] // PAIR_K + hp)),
                pl.BlockSpec((1, t, PAIR_K), lambda i, hp: (i, 0, cols["ck"] // PAIR_K + hp)),
                pl.BlockSpec((1, t, PAIR_V), lambda i, hp: (i, 0, cols["cv"] // PAIR_V + hp)),
                pl.BlockSpec((1, z_rows, PAIR_V), lambda i, hp: (i, 0, cols["cz"] // PAIR_V + hp)),
                pl.BlockSpec((1, t, LANES), lambda i, hp: (i, 0, 0))]

    def whole(shape):
        nd = len(shape)
        return pl.BlockSpec(shape, lambda i, hp: (0,) * nd)

    emat_f, emat_b, masks = consts
    specs = side(t_lat, True, FULL_COLS) + side(t_ctx, want_ctx, ctx_cols or FULL_COLS) + [
        pl.BlockSpec((1, LANES, 2 * PAIR_K), lambda i, hp: (hp, 0, 0)),
        pl.BlockSpec((1, 1, 2 * PAIR_K), lambda i, hp: (hp, 0, 0)),
        pl.BlockSpec((1, 1, PAIR_V), lambda i, hp: (hp, 0, 0)),
        whole(emat_f.shape), whole(emat_b.shape), whole(masks.shape)]
    out_specs = [pl.BlockSpec((1, t_lat, PAIR_V), lambda i, hp: (i, 0, hp))]
    out_shape = [jax.ShapeDtypeStruct((b, t_lat, BR_W), BF16)]
    if want_ctx:
        out_specs.append(pl.BlockSpec((1, t_ctx, PAIR_V), lambda i, hp: (i, 0, hp)))
        out_shape.append(jax.ShapeDtypeStruct((b, t_ctx, BR_W), BF16))

    def kern(*refs):
        refs = list(refs)
        for n in range(10, 13):
            refs[n] = refs[n].at[0]
        _gla_kernel(*refs, t_lat=t_lat, t_ctx=t_ctx, want_ctx=want_ctx)

    res = pl.pallas_call(
        kern,
        grid=(b, n_pairs),
        in_specs=specs,
        out_specs=out_specs,
        out_shape=out_shape,
        scratch_shapes=[pltpu.VMEM((C_HEAD_V, PAIR_K), F32), pltpu.VMEM((C_HEAD_V, PAIR_K), F32),
                        pltpu.VMEM((t_lat, PAIR_V), F32), pltpu.VMEM((t_lat, PAIR_V), F32),
                        pltpu.VMEM((t_lat, 2 * PAIR_K), BF16), pltpu.VMEM((t_lat, 2 * PAIR_K), BF16),
                        pltpu.VMEM((t_lat, 2 * PAIR_K), BF16)],
        compiler_params=_cparams(2),
        name="gla",
    )(p_lat, p_lat, p_lat, p_lat, r_lat, p_ctx, p_ctx, p_ctx, p_ctx, r_ctx,
      w2, b2, gain, emat_f, emat_b, masks)
    return (res[0], res[1]) if want_ctx else (res[0], None)


def _merge_kernel(*refs, alpha):
    n_mg = N_BRANCH * D_MODEL // MG_BLK
    mg_refs = refs[:n_mg]
    ya_ref, yb_ref, yc_ref, yd_ref, h_ref, gate_ref, wbr_ref, wout_ref, lng_ref, lnb_ref, o_ref = refs[n_mg:]
    per = D_MODEL // MG_BLK
    acc = None
    for n, y_ref in enumerate((ya_ref, yb_ref, yc_ref, yd_ref)):
        mg = jnp.concatenate([r[...] for r in mg_refs[n * per:(n + 1) * per]], axis=1)
        gate = _sigmoid(mg.astype(F32))
        term = gate * _dot(y_ref[...], wbr_ref[n])
        acc = term if acc is None else acc + term
    y = _dot(acc.astype(BF16), wout_ref[...])
    t = alpha * h_ref[...] + gate_ref[0] * y
    mu = jnp.mean(t, axis=-1, keepdims=True)
    tc = t - mu
    var = jnp.mean(tc * tc, axis=-1, keepdims=True)
    o_ref[...] = tc * lax.rsqrt(var + EPS) * lng_ref[...] + lnb_ref[...]


def _merge(p2, ya, yb, yc, yd, h, gate, rows_per_mod, w_br, w_out, ln_g, ln_b, alpha):
    m = h.shape[0]
    tm = min(512, m)
    assert m % tm == 0 and rows_per_mod % tm == 0
    tiles_per_mod = rows_per_mod // tm
    n_mg = N_BRANCH * D_MODEL // MG_BLK
    br_spec = pl.BlockSpec((tm, BR_W), lambda i: (i, 0))
    vec_spec = pl.BlockSpec((1, D_MODEL), lambda i: (0, 0))
    return pl.pallas_call(
        functools.partial(_merge_kernel, alpha=alpha),
        grid=(m // tm,),
        in_specs=[pl.BlockSpec((tm, MG_BLK), functools.partial(lambda n, i: (i, COL_MG // MG_BLK + n), n))
                  for n in range(n_mg)] + [
            br_spec, br_spec, br_spec, br_spec,
            pl.BlockSpec((tm, D_MODEL), lambda i: (i, 0)),
            pl.BlockSpec((1, 1, D_MODEL), lambda i: (i // tiles_per_mod, 0, 0)),
            pl.BlockSpec((N_BRANCH, BR_W, D_MODEL), lambda i: (0, 0, 0)),
            pl.BlockSpec((D_MODEL, D_MODEL), lambda i: (0, 0)),
            vec_spec, vec_spec,
        ],
        out_specs=pl.BlockSpec((tm, D_MODEL), lambda i: (i, 0)),
        out_shape=jax.ShapeDtypeStruct((m, D_MODEL), F32),
        compiler_params=_cparams(1),
        name="merge",
    )(*([p2] * n_mg), ya, yb, yc, yd, h, gate, w_br, w_out, ln_g.reshape(1, D_MODEL), ln_b.reshape(1, D_MODEL))


def _projection_weights(w_in):
    r0 = int(sum(IN_WIDTHS[:12]))
    r1 = r0 + 2 * C_GATE_RANK
    w_main = jnp.concatenate([w_in[:, :, :r0], w_in[:, :, r1:]], axis=-1).astype(BF16)
    w_r = jnp.pad(w_in[:, :, r0:r1], ((0, 0), (0, 0), (0, LANES - 2 * C_GATE_RANK))).astype(BF16)
    return w_main, w_r


def _gate_weights(c_w2_l, c_b2_l):
    n_pairs = C_HEADS // 2

    def gate_w(i):
        w = jnp.pad(c_w2_l[i], ((i * C_GATE_RANK, LANES - (i + 1) * C_GATE_RANK), (0, 0)))
        return w.reshape(LANES, n_pairs, PAIR_K).transpose(1, 0, 2)

    w2 = jnp.concatenate([gate_w(0), gate_w(1)], axis=-1).astype(BF16)
    b2 = jnp.concatenate([c_b2_l[0].reshape(n_pairs, 1, PAIR_K), c_b2_l[1].reshape(n_pairs, 1, PAIR_K)], axis=-1)
    return w2, b2


def _rope_tables(t):
    rows = t // GRID_W
    row = np.repeat(np.arange(rows), GRID_W).astype(np.float32)
    col = np.tile(np.arange(GRID_W), rows).astype(np.float32)
    inv = (ROPE_THETA ** (-np.arange(0, AXIS_DIM, 2, dtype=np.float32) / AXIS_DIM)).astype(np.float32)
    ang = np.concatenate([row[:, None] * inv, col[:, None] * inv], -1).astype(np.float64)
    cos, sin = np.repeat(np.cos(ang), 2, axis=-1), np.repeat(np.sin(ang), 2, axis=-1)
    sin[:, 0::2] *= -1.0
    return jnp.asarray(cos, F32), jnp.asarray(sin, F32)


def kernel(x, c, ctx, c_ctx, w_mod, b_mod, w_in, q_norm, k_norm, b_conv, c_gate_w2, c_gate_b, c_norm,
           d_conv_w, d_conv_b, d_norm_g, d_norm_b, w_br, w_out, ln_g, ln_b):
    b, t, d = x.shape
    t_ctx = ctx.shape[1]
    depth = w_mod.shape[0]
    assert d == D_MODEL and b < 16
    alpha = (2 * depth) ** 0.25

    cc = jnp.concatenate([c, c_ctx[None], jnp.zeros((16 - b - 1, D_MODEL), F32)], axis=0)
    mod = _modulation(cc, w_mod, b_mod)
    cos, sin = _rope_tables(t)
    consts = _gla_constants()
    w_main, w_r = _projection_weights(w_in)
    w_br_bf, w_out_bf = w_br.astype(BF16), w_out.astype(BF16)
    n_pairs = C_HEADS // 2

    h_lat = x.reshape(b * t, D_MODEL)
    h_ctx = ctx.reshape(b * t_ctx, D_MODEL)
    for l in range(depth):
        want_ctx = l < depth - 1
        w2, b2 = _gate_weights(c_gate_w2[l], c_gate_b[l])
        q_gain, k_gain = q_norm[l].reshape(1, HEAD_DIM), k_norm[l].reshape(1, HEAD_DIM)
        c_gain = c_norm[l].reshape(n_pairs, 1, PAIR_V)
        shift, scale, gate = [mod[l, :, n * D_MODEL:(n + 1) * D_MODEL].reshape(16, 1, D_MODEL) for n in range(3)]
        p_lat, r_lat = _in_projection(h_lat, shift, scale, t, w_main, w_r, l)
        ctx_tiles = None if want_ctx else tuple(c // CTX_KV_TN for c in CTX_KV_TILES)
        ctx_cols = FULL_COLS if want_ctx else CTX_KV_COLS
        p_ctx, r_ctx = _in_projection(h_ctx, shift[b:], scale[b:], b * t_ctx, w_main, w_r, l, ctx_tiles,
                                      PROJ_TN if want_ctx else CTX_KV_TN)
        p_lat3 = p_lat.reshape(b, t, N_MAIN)
        p_ctx3 = p_ctx.reshape(b, t_ctx, p_ctx.shape[1])
        r_lat3 = r_lat.reshape(b, t, LANES)
        r_ctx3 = r_ctx.reshape(b, t_ctx, LANES)
        w_br_l, w_out_l = w_br_bf[l], w_out_bf[l]

        ya_l = _attention(p_lat3, p_lat3, p_ctx3, q_gain, k_gain, cos, sin, ctx_cols)
        yb_l, yd_l = _local_branches(p_lat, t, b_conv[l], d_conv_w[l], d_conv_b[l], d_norm_g[l], d_norm_b[l])
        yc_l, yc_c = _gla(p_lat3, r_lat3, p_ctx3, r_ctx3, w2, b2, c_gain, consts, want_ctx, ctx_cols)
        h_lat_new = _merge(p_lat, ya_l.reshape(b * t, BR_W), yb_l, yc_l.reshape(b * t, BR_W), yd_l,
                           h_lat, gate, t, w_br_l, w_out_l, ln_g[l], ln_b[l], alpha)
        if want_ctx:
            ya_c = _attention(p_ctx3, None, p_ctx3, q_gain, k_gain, None, None)
            yb_c, yd_c = _local_branches(p_ctx, t_ctx, b_conv[l], d_conv_w[l], d_conv_b[l], d_norm_g[l],
                                         d_norm_b[l])
            h_ctx = _merge(p_ctx, ya_c.reshape(b * t_ctx, BR_W), yb_c, yc_c.reshape(b * t_ctx, BR_W), yd_c,
                           h_ctx, gate[b:], b * t_ctx, w_br_l, w_out_l, ln_g[l], ln_b[l], alpha)
        h_lat = h_lat_new
    return h_lat.reshape(b, t, D_MODEL)
```

```python
import functools

import numpy as np
import jax
import jax.numpy as jnp
from jax import lax
from jax.experimental import pallas as pl
from jax.experimental.pallas import tpu as pltpu

F32 = jnp.float32
BF16 = jnp.bfloat16

D_MODEL = 1024
GRID_W = 64
N_BRANCH = 4
BR_W = D_MODEL // 2
HEAD_DIM = 128
A_HEADS = BR_W // HEAD_DIM
A_KV_HEADS = A_HEADS // 2
ROPE_THETA = 10000.0
AXIS_DIM = HEAD_DIM // 2
B_CONV = 3
C_HEADS = 4
C_HEAD_K = BR_W // (2 * C_HEADS)
C_HEAD_V = BR_W // C_HEADS
C_KEY_W = C_HEADS * C_HEAD_K
C_GATE_RANK = 16
C_GATE_TAU = 16.0
C_CHUNK = 64
D_CONV = 31
EPS = 1e-6
IN_WIDTHS = (
    A_HEADS * HEAD_DIM, A_KV_HEADS * HEAD_DIM, A_KV_HEADS * HEAD_DIM, BR_W,
    BR_W, BR_W, BR_W, BR_W,
    C_KEY_W, C_KEY_W, C_HEADS * C_HEAD_V, BR_W, 2 * C_GATE_RANK,
    2 * BR_W, BR_W,
    N_BRANCH * D_MODEL,
)

COL_AQ = 0
COL_AK = COL_AQ + 512
COL_AV = COL_AK + 256
COL_AZ = COL_AV + 256
COL_BG = COL_AZ + 512
COL_BC = COL_BG + 512
COL_BX = COL_BC + 512
COL_BZ = COL_BX + 512
COL_CQ = COL_BZ + 512
COL_CK = COL_CQ + 256
COL_CV = COL_CK + 256
COL_CZ = COL_CV + 512
COL_DA = COL_CZ + 512
COL_DG = COL_DA + 512
COL_DZ = COL_DG + 512
COL_MG = COL_DZ + 512
MG_BLK = 512
MERGE_SPLIT = 1
N_MAIN = COL_MG + N_BRANCH * D_MODEL
FULL_COLS = dict(ak=COL_AK, av=COL_AV, cq=COL_CQ, ck=COL_CK, cv=COL_CV, cz=COL_CZ)
CTX_KV_TILES = (COL_AK, COL_CQ, COL_CV)
CTX_KV_COLS = dict(ak=0, av=256, cq=512, ck=768, cv=1024, cz=1024)
LANES = 128
SUBLANES = 8
PROJ_TN = 1536
CTX_KV_TN = 512
LN_GROUP = 512
VMEM_LIMIT = 56 * 1024 * 1024
LOG2E = 1.4426950408889634


def _cparams(n_axes):
    return pltpu.CompilerParams(dimension_semantics=("arbitrary",) * n_axes,
                                vmem_limit_bytes=VMEM_LIMIT)


def _sigmoid(x):
    return 0.5 * jnp.tanh(0.5 * x) + 0.5


def _silu(x):
    h = 0.5 * x
    return h + h * jnp.tanh(h)


def _dot(a, b):
    return jnp.dot(a, b, preferred_element_type=F32)


def _dot_nt(a, b):
    return lax.dot_general(a, b, (((1,), (1,)), ((), ())), preferred_element_type=F32)


def _dot_tn(a, b):
    return lax.dot_general(a, b, (((0,), (0,)), ((), ())), preferred_element_type=F32)


def _mod_kernel(c_ref, w_ref, b_ref, o_ref):
    s = _silu(c_ref[...])
    o_ref[0] = _dot(s.astype(BF16), w_ref[0].astype(BF16)) + b_ref[0]


def _modulation(cc, w_mod, b_mod):
    depth = w_mod.shape[0]
    n_rows = cc.shape[0]
    return pl.pallas_call(
        _mod_kernel,
        grid=(depth, 3),
        in_specs=[
            pl.BlockSpec((n_rows, D_MODEL), lambda l, j: (0, 0)),
            pl.BlockSpec((1, D_MODEL, D_MODEL), lambda l, j: (l, 0, j)),
            pl.BlockSpec((1, 1, D_MODEL), lambda l, j: (l, 0, j)),
        ],
        out_specs=pl.BlockSpec((1, n_rows, D_MODEL), lambda l, j: (l, 0, j)),
        out_shape=jax.ShapeDtypeStruct((depth, n_rows, 3 * D_MODEL), F32),
        compiler_params=_cparams(2),
        name="modulation",
    )(cc, w_mod, b_mod.reshape(depth, 1, 3 * D_MODEL))


def _inproj_kernel(x_ref, shift_ref, scale_ref, w_ref, wr_ref, o_ref, or_ref, u_ref, *, tm, sub):
    j = pl.program_id(1)

    @pl.when(j == 0)
    def _():
        one_plus = 1.0 + scale_ref[0]
        shift = shift_ref[0]
        grp = min(LN_GROUP, tm)
        for g0 in range(0, tm, grp):
            for r0 in range(g0, g0 + grp, sub):
                x = x_ref[r0:r0 + sub, :]
                mu = jnp.mean(x, axis=-1, keepdims=True)
                xc = x - mu
                var = jnp.mean(xc * xc, axis=-1, keepdims=True)
                u = xc * lax.rsqrt(var + EPS) * one_plus + shift
                u_ref[r0:r0 + sub, :] = u.astype(BF16)
            u_grp = u_ref[g0:g0 + grp, :]
            or_ref[g0:g0 + grp, :] = _dot(u_grp, wr_ref[...]).astype(BF16)
            o_ref[g0:g0 + grp, :] = _dot(u_grp, w_ref[...]).astype(BF16)

    @pl.when(j != 0)
    def _():
        o_ref[...] = _dot(u_ref[...], w_ref[...]).astype(BF16)


def _in_projection(h, shift, scale, rows_per_mod, w_main, w_r, layer, tiles=None, tn=PROJ_TN):
    m = h.shape[0]
    tm = min(2048, rows_per_mod)
    assert m % tm == 0 and rows_per_mod % tm == 0
    tiles_per_mod = rows_per_mod // tm
    assert w_main.shape[2] % tn == 0
    n_tiles = len(tiles) if tiles else w_main.shape[2] // tn
    n_cols = n_tiles * tn

    def w_tile(j):
        if not tiles:
            return j
        idx = tiles[0]
        for n in range(1, len(tiles)):
            idx = idx + (tiles[n] - tiles[n - 1]) * jnp.minimum(jnp.maximum(j - n + 1, 0), 1)
        return idx

    kern = functools.partial(_inproj_kernel, tm=tm, sub=min(128, tm))
    return pl.pallas_call(
        kern,
        grid=(m // tm, n_tiles),
        in_specs=[
            pl.BlockSpec((tm, D_MODEL), lambda i, j: (i, 0)),
            pl.BlockSpec((1, 1, D_MODEL), lambda i, j: (i // tiles_per_mod, 0, 0)),
            pl.BlockSpec((1, 1, D_MODEL), lambda i, j: (i // tiles_per_mod, 0, 0)),
            pl.BlockSpec((None, D_MODEL, tn), lambda i, j: (layer, 0, w_tile(j))),
            pl.BlockSpec((None, D_MODEL, LANES), lambda i, j: (layer, 0, 0)),
        ],
        out_specs=[
            pl.BlockSpec((tm, tn), lambda i, j: (i, j)),
            pl.BlockSpec((tm, LANES), lambda i, j: (i, 0)),
        ],
        out_shape=[
            jax.ShapeDtypeStruct((m, n_cols), BF16),
            jax.ShapeDtypeStruct((m, LANES), BF16),
        ],
        scratch_shapes=[pltpu.VMEM((tm, D_MODEL), BF16)],
        compiler_params=_cparams(2),
        name="in_projection",
    )(h, shift, scale, w_main, w_r)


ATTN_KEY_BLOCKS = (768, 512, 256)
QK_AHEAD = 2


def _rms(x):
    return x * lax.rsqrt(jnp.mean(x * x, axis=-1, keepdims=True) + EPS)


def _rope(x, cos, sin):
    lane = lax.broadcasted_iota(jnp.int32, x.shape, 1)
    swapped = jnp.where(lane % 2 == 0, pltpu.roll(x, HEAD_DIM - 1, 1), pltpu.roll(x, 1, 1))
    return x * cos + swapped * sin


def _attn_kernel(*refs, n_lat, n_ctx, tq, kblk, kstep):
    if n_lat:
        (q_ref, z_ref, kl_ref, vl_ref, kc_ref, vc_ref, qg_ref, kg_ref, cos_ref, sin_ref,
         o_ref, kn_ref, vn_ref) = refs
    else:
        q_ref, z_ref, kc_ref, vc_ref, qg_ref, kg_ref, o_ref, kn_ref, vn_ref = refs
    qi = pl.program_id(2)
    kg = kg_ref[...]

    @pl.when(qi == 0)
    def _prep():
        if n_lat:
            def body(t, carry):
                r0 = pl.multiple_of(t * kblk, kblk)
                kn = _rms(kl_ref[0, pl.ds(r0, kblk), :].astype(F32)) * kg
                kn = _rope(kn, cos_ref[pl.ds(r0, kblk), :], sin_ref[pl.ds(r0, kblk), :])
                kn_ref[pl.ds(r0, kblk), :] = kn.astype(BF16)
                vn_ref[pl.ds(r0, kblk), 0:HEAD_DIM] = vl_ref[0, pl.ds(r0, kblk), :]
                return carry

            lax.fori_loop(0, n_lat // kblk, body, 0)
        kn_ref[n_lat:n_lat + n_ctx, :] = (_rms(kc_ref[0].astype(F32)) * kg).astype(BF16)
        vn_ref[n_lat:n_lat + n_ctx, 0:HEAD_DIM] = vc_ref[0]
        vn_ref[:, HEAD_DIM:2 * HEAD_DIM] = jnp.ones((n_lat + n_ctx, HEAD_DIM), BF16)

    qg = qg_ref[...]
    if n_lat:
        q0 = pl.multiple_of(qi * tq, tq)
        cos = cos_ref[pl.ds(q0, tq), :]
        sin = sin_ref[pl.ds(q0, tq), :]
    qns = []
    for g in range(2):
        qn = _rms(q_ref[0, :, g * HEAD_DIM:(g + 1) * HEAD_DIM].astype(F32)) * qg
        if n_lat:
            qn = _rope(qn, cos, sin)
        qns.append((qn * (HEAD_DIM ** -0.5 * LOG2E)).astype(BF16))

    items = [(g, k0) for k0 in range(0, n_lat + n_ctx, kstep) for g in range(2)]

    def scores(item):
        g, k0 = item
        return _dot_nt(qns[g], kn_ref[k0:k0 + kstep, :])

    m = [None, None]
    acc = [None, None]
    ahead = [scores(item) for item in items[:QK_AHEAD]]
    for n, (g, k0) in enumerate(items):
        s = ahead.pop(0)
        if n + QK_AHEAD < len(items):
            ahead.append(scores(items[n + QK_AHEAD]))
        m_blk = jnp.max(s, axis=-1, keepdims=True)
        m_new = m_blk if m[g] is None else jnp.maximum(m[g], m_blk)
        p = jnp.exp2((s - m_new).astype(BF16))
        pv = _dot(p, vn_ref[k0:k0 + kstep, :])
        acc[g] = pv if acc[g] is None else acc[g] * jnp.exp2(m[g] - m_new) + pv
        m[g] = m_new
    for g in range(2):
        lanes = slice(g * HEAD_DIM, (g + 1) * HEAD_DIM)
        o = acc[g][:, 0:HEAD_DIM] / acc[g][:, HEAD_DIM:2 * HEAD_DIM]
        o_ref[0, :, lanes] = (o * _silu(z_ref[0, :, lanes].astype(F32))).astype(BF16)


def _attention(p_q, p_lat, p_ctx, q_gain, k_gain, cos, sin, ctx_cols=None):
    b, t_q, _ = p_q.shape
    n_ctx = p_ctx.shape[1]
    n_lat = 0 if p_lat is None else p_lat.shape[1]
    tq = min(1024, t_q)
    assert t_q % tq == 0
    kblk = min(1024, n_lat) if n_lat else 0
    n_keys = n_lat + n_ctx
    kstep = next((c for c in ATTN_KEY_BLOCKS if n_keys % c == 0), n_keys)
    hw = 2 * HEAD_DIM
    q_spec = pl.BlockSpec((1, tq, hw), lambda i, kv, qi: (i, qi, COL_AQ // hw + kv))
    z_spec = pl.BlockSpec((1, tq, hw), lambda i, kv, qi: (i, qi, COL_AZ // hw + kv))

    def kv_spec(n, col):
        return pl.BlockSpec((1, n, HEAD_DIM), lambda i, kv, qi: (i, 0, col // HEAD_DIM + kv))

    gain_spec = pl.BlockSpec((1, HEAD_DIM), lambda i, kv, qi: (0, 0))
    args = [p_q, p_q]
    specs = [q_spec, z_spec]
    if n_lat:
        args += [p_lat, p_lat]
        specs += [kv_spec(n_lat, COL_AK), kv_spec(n_lat, COL_AV)]
    ctx_cols = ctx_cols or FULL_COLS
    args += [p_ctx, p_ctx, q_gain, k_gain]
    specs += [kv_spec(n_ctx, ctx_cols["ak"]), kv_spec(n_ctx, ctx_cols["av"]), gain_spec, gain_spec]
    if n_lat:
        tab_spec = pl.BlockSpec((n_lat, HEAD_DIM), lambda i, kv, qi: (0, 0))
        args += [cos, sin]
        specs += [tab_spec, tab_spec]
    kern = functools.partial(_attn_kernel, n_lat=n_lat, n_ctx=n_ctx, tq=tq, kblk=kblk, kstep=kstep)
    return pl.pallas_call(
        kern,
        grid=(b, A_KV_HEADS, t_q // tq),
        in_specs=specs,
        out_specs=pl.BlockSpec((1, tq, hw), lambda i, kv, qi: (i, qi, kv)),
        out_shape=jax.ShapeDtypeStruct((b, t_q, BR_W), BF16),
        scratch_shapes=[pltpu.VMEM((n_lat + n_ctx, HEAD_DIM), BF16),
                        pltpu.VMEM((n_lat + n_ctx, 2 * HEAD_DIM), BF16)],
        compiler_params=_cparams(3),
        name="attention_lat" if n_lat else "attention_ctx",
    )(*args)


D_HALO = 16
B_HALO = 16
CONV_RB = 32
CONV_ACCS = 4
ELEM_RB = 32
COPY_RB = 56


def _local_kernel(bg_ref, bc_ref, bcp_ref, bcn_ref, bx_ref, bxp_ref, bxn_ref, bz_ref,
                  da_ref, dap_ref, dan_ref, dg_ref, dgp_ref, dgn_ref, dz_ref,
                  bw_ref, dw_ref, db_ref, dgain_ref, dbeta_ref,
                  yb_ref, yd_ref, tbuf, gbuf, hbuf, *, tt, tiles_per_seq):
    i = pl.program_id(0)
    pos = i % tiles_per_seq
    keep_prev = (pos != 0).astype(F32)
    keep_next = (pos != tiles_per_seq - 1).astype(F32)

    def ld(ref, r0=0, n=None):
        return ref[r0:r0 + (n or ref.shape[0]), :].astype(F32)

    row_blocks = range(0, tt, ELEM_RB)

    tbuf[0:B_HALO, :] = ld(bcp_ref) * ld(bxp_ref) * keep_prev
    for r0 in row_blocks:
        tbuf[B_HALO + r0:B_HALO + r0 + ELEM_RB, :] = ld(bc_ref, r0, ELEM_RB) * ld(bx_ref, r0, ELEM_RB)
    tbuf[B_HALO + tt:2 * B_HALO + tt, :] = ld(bcn_ref) * ld(bxn_ref) * keep_next
    for r0 in row_blocks:
        lo = B_HALO + r0
        conv = (bw_ref[0:1, :] * tbuf[lo - 1:lo - 1 + ELEM_RB, :] + bw_ref[1:2, :] * tbuf[lo:lo + ELEM_RB, :]
                + bw_ref[2:3, :] * tbuf[lo + 1:lo + 1 + ELEM_RB, :])
        yb_ref[r0:r0 + ELEM_RB, :] = (ld(bg_ref, r0, ELEM_RB) * conv
                                      * _silu(ld(bz_ref, r0, ELEM_RB))).astype(BF16)

    gbuf[0, 0:D_HALO, :] = ld(dap_ref) * _sigmoid(ld(dgp_ref)) * keep_prev
    for r0 in row_blocks:
        gbuf[0, D_HALO + r0:D_HALO + r0 + ELEM_RB, :] = (ld(da_ref, r0, ELEM_RB)
                                                          * _sigmoid(ld(dg_ref, r0, ELEM_RB)))
    gbuf[0, D_HALO + tt:2 * D_HALO + tt, :] = ld(dan_ref) * _sigmoid(ld(dgn_ref)) * keep_next
    span = tt + 2 * D_HALO - SUBLANES
    for r in range(1, SUBLANES):
        for x0 in range(0, span, COPY_RB):
            n = min(COPY_RB, span - x0)
            gbuf[r, x0:x0 + n, :] = gbuf[0, r + x0:r + x0 + n, :]
    base = D_HALO - D_CONV // 2
    for cb in range(BR_W // LANES):
        lanes = slice(cb * LANES, (cb + 1) * LANES)
        for rb in range(tt // CONV_RB):
            accs = [None] * CONV_ACCS
            for k in range(D_CONV):
                off = base + k
                r0 = rb * CONV_RB + off - off % SUBLANES
                rows = gbuf[off % SUBLANES, r0:r0 + CONV_RB, lanes]
                term = rows.reshape(CONV_RB // SUBLANES, SUBLANES, LANES) * dw_ref[k, :, lanes][None]
                a = k % CONV_ACCS
                accs[a] = term if accs[a] is None else accs[a] + term
            while len(accs) > 1:
                accs = [accs[n] + accs[n + 1] for n in range(0, len(accs), 2)]
            hbuf[rb * CONV_RB:(rb + 1) * CONV_RB, lanes] = accs[0].reshape(CONV_RB, LANES)
    for r0 in row_blocks:
        hh = hbuf[r0:r0 + ELEM_RB, :] + db_ref[...]
        mu = jnp.mean(hh, axis=-1, keepdims=True)
        hc = hh - mu
        var = jnp.mean(hc * hc, axis=-1, keepdims=True)
        hn = hc * lax.rsqrt(var + EPS) * dgain_ref[...] + dbeta_ref[...]
        yd_ref[r0:r0 + ELEM_RB, :] = (_silu(hn) * _silu(ld(dz_ref, r0, ELEM_RB))).astype(BF16)


def _local_branches(p2, seq_len, b_w, d_w, d_b, d_g, d_beta):
    m = p2.shape[0]
    tt = min(512, seq_len)
    assert seq_len % tt == 0 and m % seq_len == 0
    tiles_per_seq = seq_len // tt
    n_tiles = m // tt

    def cur(col):
        return pl.BlockSpec((tt, BR_W), lambda i: (i, col // BR_W))

    def prev(col, halo):
        per = tt // halo
        return pl.BlockSpec((halo, BR_W), lambda i: (jnp.maximum(i * per - 1, 0), col // BR_W))

    def nxt(col, halo):
        per = tt // halo
        last = m // halo - 1
        return pl.BlockSpec((halo, BR_W), lambda i: (jnp.minimum((i + 1) * per, last), col // BR_W))

    def small(rows):
        return pl.BlockSpec((rows, BR_W), lambda i: (0, 0))

    specs = [cur(COL_BG),
             cur(COL_BC), prev(COL_BC, B_HALO), nxt(COL_BC, B_HALO),
             cur(COL_BX), prev(COL_BX, B_HALO), nxt(COL_BX, B_HALO),
             cur(COL_BZ),
             cur(COL_DA), prev(COL_DA, D_HALO), nxt(COL_DA, D_HALO),
             cur(COL_DG), prev(COL_DG, D_HALO), nxt(COL_DG, D_HALO),
             cur(COL_DZ),
             small(B_CONV), pl.BlockSpec((D_CONV, SUBLANES, BR_W), lambda i: (0, 0, 0)),
             small(1), small(1), small(1)]
    kern = functools.partial(_local_kernel, tt=tt, tiles_per_seq=tiles_per_seq)
    out_spec = pl.BlockSpec((tt, BR_W), lambda i: (i, 0))
    return pl.pallas_call(
        kern,
        grid=(n_tiles,),
        in_specs=specs,
        out_specs=[out_spec, out_spec],
        out_shape=[jax.ShapeDtypeStruct((m, BR_W), BF16), jax.ShapeDtypeStruct((m, BR_W), BF16)],
        scratch_shapes=[pltpu.VMEM((tt + 2 * B_HALO, BR_W), F32),
                        pltpu.VMEM((SUBLANES, tt + 2 * D_HALO, BR_W), F32),
                        pltpu.VMEM((tt, BR_W), F32)],
        compiler_params=_cparams(1),
        name="local_branches",
    )(*([p2] * 15), b_w, jnp.broadcast_to(d_w[:, None, :], (D_CONV, SUBLANES, BR_W)),
      d_b.reshape(1, BR_W), d_g.reshape(1, BR_W), d_beta.reshape(1, BR_W))


CH = C_CHUNK
PAIR_K = 2 * C_HEAD_K
PAIR_V = 2 * C_HEAD_V
LEVELS = (8, 4, 2)
BLK = SUBLANES
N_BLK = CH // BLK
GLA_UNROLL = 16
ROW_EQ = 0
ROW_EK = CH
ROW_LAST = 2 * CH
ROW_FK = 2 * CH + 8
ROW_LVL = ROW_FK + CH
ROW_FQ = ROW_LVL + len(LEVELS) * CH
N_FQ = BLK * (N_BLK * (N_BLK - 1) // 2)
N_EROWS = ROW_FQ + N_FQ


def _slabs(reverse):
    if reverse:
        return [(jb, 0, BLK * jb) for jb in range(1, N_BLK)]
    return [(jb, BLK * (jb + 1), CH - BLK * (jb + 1)) for jb in range(N_BLK - 1)]


def _gla_constants():
    idx = np.arange(CH)
    cols = np.arange(2 * CH) % CH
    emats, masks = [], []
    for reverse in (False, True):
        tri = (idx[None, :] >= idx[:, None]) if reverse else (idx[None, :] <= idx[:, None])
        tri = tri.astype(np.float32)
        edge = tri[0] if reverse else tri[CH - 1]
        blk_edge = (idx // BLK) * BLK + (0 if reverse else BLK - 1)
        blocks = [tri, edge[None, :] - tri, np.tile(edge[None, :], (8, 1)), tri[blk_edge] - tri]
        lvl_masks = []
        for grp in LEVELS:
            half = grp // 2
            in_q = (idx % grp < half) if reverse else (idx % grp >= half)
            ref = (idx // grp) * grp + (half if reverse else half - 1)
            d = tri - tri[ref]
            blocks.append(np.where(in_q[:, None], d, -d))
            in_q_col = (cols % grp < half) if reverse else (cols % grp >= half)
            lvl_masks.append((idx[:, None] // grp == cols[None, :] // grp) & in_q[:, None] & ~in_q_col[None, :])
        lvl_masks.append(idx[:, None] == cols[None, :])
        for jb, q0, nq in _slabs(reverse):
            edge_row = BLK * jb + (0 if reverse else BLK - 1)
            blocks.append(tri[q0:q0 + nq] - tri[edge_row][None, :])
        emat = np.concatenate(blocks, axis=0)
        assert emat.shape == (N_EROWS, CH) and emat.min() >= -1 and emat.max() <= 1
        emats.append(np.concatenate([emat] * 3, axis=1))
        masks.append(np.stack(lvl_masks).astype(np.float32))
    return (jnp.asarray(emats[0], BF16), jnp.asarray(emats[1], BF16), jnp.asarray(np.stack(masks), F32))


def _split3(g):
    g1 = g.astype(BF16)
    r1 = g - g1.astype(F32)
    g2 = r1.astype(BF16)
    g3 = (r1 - g2.astype(F32)).astype(BF16)
    return g1, g2, g3


def _log2_sigmoid(x):
    e = jnp.exp2(jnp.abs(x) * -LOG2E)
    return jnp.minimum(x, 0.0) * LOG2E - jnp.log2(1.0 + e)


def _gla_kernel(*refs, t_lat, t_ctx, want_ctx):
    (ql_ref, kl_ref, vl_ref, zl_ref, rl_ref, qc_ref, kc_ref, vc_ref, zc_ref, rc_ref,
     w2_ref, b2_ref, gain_ref, ematf_ref, ematb_ref, masks_ref) = refs[:16]
    if want_ctx:
        yl_ref, yc_ref = refs[16:18]
        scratch = refs[18:]
    else:
        yl_ref = refs[16]
        scratch = refs[17:]
        yc_ref = zc_ref = None
    stf_ref, stb_ref, of_ref, ob_ref, g1_ref, g2_ref, g3_ref = scratch

    def gates(r_ref, n_rows):
        blk = min(512, n_rows)

        def body(t, carry):
            rows = pl.ds(pl.multiple_of(t * blk, blk), blk)
            x = _dot(r_ref[0, rows, :], w2_ref[...]) + b2_ref[...]
            g1, g2, g3 = _split3(_log2_sigmoid(x) * (1.0 / C_GATE_TAU))
            g1_ref[rows, :] = g1
            g2_ref[rows, :] = g2
            g3_ref[rows, :] = g3
            return carry

        lax.fori_loop(0, n_rows // blk, body, 0)

    def run(q_ref, k_ref, v_ref, n_chunks, want_out):
        unroll = min(GLA_UNROLL, n_chunks)
        assert n_chunks % unroll == 0
        lane_k = lax.broadcasted_iota(jnp.int32, (CH, PAIR_K), 1)
        head0 = lane_k < C_HEAD_K
        zero_v = jnp.zeros((CH, C_HEAD_V), BF16)
        key_blk = (lax.broadcasted_iota(jnp.int32, (BLK, 2 * CH), 1) % CH) // BLK

        def body(i, carry):
            work = []
            for reverse in (False, True):
                lanes = slice(PAIR_K, 2 * PAIR_K) if reverse else slice(0, PAIR_K)
                rows = []
                for u in range(unroll):
                    c = i * unroll + u
                    c = (n_chunks - 1 - c) if reverse else c
                    rows.append(pl.ds(pl.multiple_of(c * CH, CH), CH))
                rhs = jnp.concatenate(
                    [jnp.concatenate([g_ref[r, lanes] for r in rows], axis=1)
                     for g_ref in (g1_ref, g2_ref, g3_ref)], axis=0)
                emat = ematb_ref[...] if reverse else ematf_ref[...]
                e_all = jnp.exp2(_dot(emat, rhs))
                for u in range(unroll):
                    work.append(dict(reverse=reverse, rows=rows[u], u=u,
                                     e=e_all[:, u * PAIR_K:(u + 1) * PAIR_K]))
            work.sort(key=lambda w: w["u"])

            for w in work:
                st_ref = stb_ref if w["reverse"] else stf_ref
                e = w["e"]
                q = q_ref[0, w["rows"], :].astype(F32) * C_HEAD_K ** -0.5
                k = k_ref[0, w["rows"], :].astype(F32)
                v = v_ref[0, w["rows"], :]
                k0 = jnp.where(head0, k, 0.0)
                k1 = jnp.where(head0, 0.0, k)
                st = st_ref[...]
                e_k = e[ROW_EK:ROW_EK + CH]
                k_dec = jnp.concatenate([k0 * e_k, k1 * e_k], axis=0).astype(BF16)
                v_rows = jnp.concatenate([v[:, :C_HEAD_V], v[:, C_HEAD_V:]], axis=0)
                st_ref[...] = e[ROW_LAST:ROW_LAST + 1] * st + _dot_tn(v_rows, k_dec)
                if want_out:
                    qe = q * e[ROW_EQ:ROW_EQ + CH]
                    q_rows = jnp.concatenate([jnp.where(head0, qe, 0.0), jnp.where(head0, 0.0, qe)], axis=0)
                    o_st = _dot(q_rows.astype(BF16), st.T.astype(BF16))
                    w.update(q=q, k0=k0, k1=k1, v=v, o=jnp.concatenate([o_st[:CH], o_st[CH:]], axis=1))
            if not want_out:
                return carry

            for w in work:
                masks = masks_ref.at[1 if w["reverse"] else 0]
                q, k0, k1, e = w["q"], w["k0"], w["k1"], w["e"]
                fk = e[ROW_FK:ROW_FK + CH]
                kk_t = jnp.concatenate([k0 * fk, k1 * fk], axis=0).T.astype(BF16)
                slabs = _slabs(w["reverse"])
                lhs, off = [], ROW_FQ
                for jb, q0, nq in slabs:
                    lhs.append(q[q0:q0 + nq] * e[off:off + nq])
                    off += nq
                r = _dot(jnp.concatenate(lhs, axis=0).astype(BF16), kk_t)
                rows, off = [jnp.zeros((BLK, 2 * CH), F32)] * N_BLK, 0
                for jb, q0, nq in slabs:
                    for ib in range(q0 // BLK, (q0 + nq) // BLK):
                        piece = r[off + ib * BLK - q0:off + (ib + 1) * BLK - q0]
                        rows[ib] = jnp.where(key_blk == jb, piece, rows[ib])
                    off += nq
                scores = jnp.concatenate(rows, axis=0)
                k_heads = jnp.concatenate([k0, k1], axis=0)
                scores = scores + _dot(q.astype(BF16), k_heads.T.astype(BF16)) * masks[len(LEVELS)]
                for lvl in range(len(LEVELS)):
                    e_l = e[ROW_LVL + lvl * CH:ROW_LVL + (lvl + 1) * CH]
                    kk_t = jnp.concatenate([k0 * e_l, k1 * e_l], axis=0).T.astype(BF16)
                    scores = scores + _dot((q * e_l).astype(BF16), kk_t) * masks[lvl]
                w["scores"] = scores.astype(BF16)

            for w in work:
                v = w["v"]
                v_bd = jnp.concatenate([jnp.concatenate([v[:, :C_HEAD_V], zero_v], axis=1),
                                        jnp.concatenate([zero_v, v[:, C_HEAD_V:]], axis=1)], axis=0)
                (ob_ref if w["reverse"] else of_ref)[w["rows"], :] = w["o"] + _dot(w["scores"], v_bd)
            return carry

        lax.fori_loop(0, n_chunks // unroll, body, 0)

    def finish(z_ref, y_ref, n_rows):
        blk = min(256, n_rows)

        def body(t, carry):
            rows = pl.ds(pl.multiple_of(t * blk, blk), blk)
            o = of_ref[rows, :] + ob_ref[rows, :]
            z = z_ref[0, rows, :].astype(F32)
            gain = gain_ref[...]
            for h in range(2):
                lanes = slice(h * C_HEAD_V, (h + 1) * C_HEAD_V)
                y = _rms(o[:, lanes]) * gain[:, lanes]
                y_ref[0, rows, lanes] = (y * _silu(z[:, lanes])).astype(BF16)
            return carry

        lax.fori_loop(0, n_rows // blk, body, 0)

    stf_ref[...] = jnp.zeros_like(stf_ref)
    stb_ref[...] = jnp.zeros_like(stb_ref)
    gates(rc_ref, t_ctx)
    run(qc_ref, kc_ref, vc_ref, t_ctx // CH, want_ctx)
    if want_ctx:
        finish(zc_ref, yc_ref, t_ctx)
    gates(rl_ref, t_lat)
    run(ql_ref, kl_ref, vl_ref, t_lat // CH, True)
    finish(zl_ref, yl_ref, t_lat)


def _gla(p_lat, r_lat, p_ctx, r_ctx, w2, b2, gain, consts, want_ctx, ctx_cols=None):
    b, t_lat, _ = p_lat.shape
    t_ctx = p_ctx.shape[1]
    n_pairs = C_HEADS // 2

    def side(t, with_z, cols):
        z_rows = t if with_z else 16
        return [pl.BlockSpec((1, t, PAIR_K), lambda i, hp: (i, 0, cols["cq"] // PAIR_K + hp)),
                pl.BlockSpec((1, t, PAIR_K), lambda i, hp: (i, 0, cols["ck"] // PAIR_K + hp)),
                pl.BlockSpec((1, t, PAIR_V), lambda i, hp: (i, 0, cols["cv"] // PAIR_V + hp)),
                pl.BlockSpec((1, z_rows, PAIR_V), lambda i, hp: (i, 0, cols["cz"] // PAIR_V + hp)),
                pl.BlockSpec((1, t, LANES), lambda i, hp: (i, 0, 0))]

    def whole(shape):
        nd = len(shape)
        return pl.BlockSpec(shape, lambda i, hp: (0,) * nd)

    emat_f, emat_b, masks = consts
    specs = side(t_lat, True, FULL_COLS) + side(t_ctx, want_ctx, ctx_cols or FULL_COLS) + [
        pl.BlockSpec((1, LANES, 2 * PAIR_K), lambda i, hp: (hp, 0, 0)),
        pl.BlockSpec((1, 1, 2 * PAIR_K), lambda i, hp: (hp, 0, 0)),
        pl.BlockSpec((1, 1, PAIR_V), lambda i, hp: (hp, 0, 0)),
        whole(emat_f.shape), whole(emat_b.shape), whole(masks.shape)]
    out_specs = [pl.BlockSpec((1, t_lat, PAIR_V), lambda i, hp: (i, 0, hp))]
    out_shape = [jax.ShapeDtypeStruct((b, t_lat, BR_W), BF16)]
    if want_ctx:
        out_specs.append(pl.BlockSpec((1, t_ctx, PAIR_V), lambda i, hp: (i, 0, hp)))
        out_shape.append(jax.ShapeDtypeStruct((b, t_ctx, BR_W), BF16))

    def kern(*refs):
        refs = list(refs)
        for n in range(10, 13):
            refs[n] = refs[n].at[0]
        _gla_kernel(*refs, t_lat=t_lat, t_ctx=t_ctx, want_ctx=want_ctx)

    res = pl.pallas_call(
        kern,
        grid=(b, n_pairs),
        in_specs=specs,
        out_specs=out_specs,
        out_shape=out_shape,
        scratch_shapes=[pltpu.VMEM((C_HEAD_V, PAIR_K), F32), pltpu.VMEM((C_HEAD_V, PAIR_K), F32),
                        pltpu.VMEM((t_lat, PAIR_V), F32), pltpu.VMEM((t_lat, PAIR_V), F32),
                        pltpu.VMEM((t_lat, 2 * PAIR_K), BF16), pltpu.VMEM((t_lat, 2 * PAIR_K), BF16),
                        pltpu.VMEM((t_lat, 2 * PAIR_K), BF16)],
        compiler_params=_cparams(2),
        name="gla",
    )(p_lat, p_lat, p_lat, p_lat, r_lat, p_ctx, p_ctx, p_ctx, p_ctx, r_ctx,
      w2, b2, gain, emat_f, emat_b, masks)
    return (res[0], res[1]) if want_ctx else (res[0], None)


def _merge_kernel(*refs, alpha):
    n_mg = N_BRANCH * D_MODEL // MG_BLK
    mg_refs = refs[:n_mg]
    ya_ref, yb_ref, yc_ref, yd_ref, h_ref, gate_ref, wbr_ref, wout_ref, lng_ref, lnb_ref, o_ref = refs[n_mg:]
    per = D_MODEL // MG_BLK
    tm = o_ref.shape[0]
    rows_per = tm // MERGE_SPLIT
    for r0 in range(0, tm, rows_per):
        rows = slice(r0, r0 + rows_per)
        acc = None
        for n, y_ref in enumerate((ya_ref, yb_ref, yc_ref, yd_ref)):
            mg = jnp.concatenate([r[rows, :] for r in mg_refs[n * per:(n + 1) * per]], axis=1)
            gate = _sigmoid(mg.astype(F32))
            term = gate * _dot(y_ref[rows, :], wbr_ref[n])
            acc = term if acc is None else acc + term
        y = _dot(acc.astype(BF16), wout_ref[...])
        t = alpha * h_ref[rows, :] + gate_ref[0] * y
        mu = jnp.mean(t, axis=-1, keepdims=True)
        tc = t - mu
        var = jnp.mean(tc * tc, axis=-1, keepdims=True)
        o_ref[rows, :] = tc * lax.rsqrt(var + EPS) * lng_ref[...] + lnb_ref[...]


def _merge(p2, ya, yb, yc, yd, h, gate, rows_per_mod, w_br, w_out, ln_g, ln_b, alpha):
    m = h.shape[0]
    tm = min(512, m)
    assert m % tm == 0 and rows_per_mod % tm == 0
    tiles_per_mod = rows_per_mod // tm
    n_mg = N_BRANCH * D_MODEL // MG_BLK
    br_spec = pl.BlockSpec((tm, BR_W), lambda i: (i, 0))
    vec_spec = pl.BlockSpec((1, D_MODEL), lambda i: (0, 0))
    return pl.pallas_call(
        functools.partial(_merge_kernel, alpha=alpha),
        grid=(m // tm,),
        in_specs=[pl.BlockSpec((tm, MG_BLK), functools.partial(lambda n, i: (i, COL_MG // MG_BLK + n), n))
                  for n in range(n_mg)] + [
            br_spec, br_spec, br_spec, br_spec,
            pl.BlockSpec((tm, D_MODEL), lambda i: (i, 0)),
            pl.BlockSpec((1, 1, D_MODEL), lambda i: (i // tiles_per_mod, 0, 0)),
            pl.BlockSpec((N_BRANCH, BR_W, D_MODEL), lambda i: (0, 0, 0)),
            pl.BlockSpec((D_MODEL, D_MODEL), lambda i: (0, 0)),
            vec_spec, vec_spec,
        ],
        out_specs=pl.BlockSpec((tm, D_MODEL), lambda i: (i, 0)),
        out_shape=jax.ShapeDtypeStruct((m, D_MODEL), F32),
        compiler_params=_cparams(1),
        name="merge",
    )(*([p2] * n_mg), ya, yb, yc, yd, h, gate, w_br, w_out, ln_g.reshape(1, D_MODEL), ln_b.reshape(1, D_MODEL))


def _projection_weights(w_in):
    r0 = int(sum(IN_WIDTHS[:12]))
    r1 = r0 + 2 * C_GATE_RANK
    w_main = jnp.concatenate([w_in[:, :, :r0], w_in[:, :, r1:]], axis=-1).astype(BF16)
    w_r = jnp.pad(w_in[:, :, r0:r1], ((0, 0), (0, 0), (0, LANES - 2 * C_GATE_RANK))).astype(BF16)
    return w_main, w_r


def _gate_weights(c_w2_l, c_b2_l):
    n_pairs = C_HEADS // 2

    def gate_w(i):
        w = jnp.pad(c_w2_l[i], ((i * C_GATE_RANK, LANES - (i + 1) * C_GATE_RANK), (0, 0)))
        return w.reshape(LANES, n_pairs, PAIR_K).transpose(1, 0, 2)

    w2 = jnp.concatenate([gate_w(0), gate_w(1)], axis=-1).astype(BF16)
    b2 = jnp.concatenate([c_b2_l[0].reshape(n_pairs, 1, PAIR_K), c_b2_l[1].reshape(n_pairs, 1, PAIR_K)], axis=-1)
    return w2, b2


def _rope_tables(t):
    rows = t // GRID_W
    row = np.repeat(np.arange(rows), GRID_W).astype(np.float32)
    col = np.tile(np.arange(GRID_W), rows).astype(np.float32)
    inv = (ROPE_THETA ** (-np.arange(0, AXIS_DIM, 2, dtype=np.float32) / AXIS_DIM)).astype(np.float32)
    ang = np.concatenate([row[:, None] * inv, col[:, None] * inv], -1).astype(np.float64)
    cos, sin = np.repeat(np.cos(ang), 2, axis=-1), np.repeat(np.sin(ang), 2, axis=-1)
    sin[:, 0::2] *= -1.0
    return jnp.asarray(cos, F32), jnp.asarray(sin, F32)


def kernel(x, c, ctx, c_ctx, w_mod, b_mod, w_in, q_norm, k_norm, b_conv, c_gate_w2, c_gate_b, c_norm,
           d_conv_w, d_conv_b, d_norm_g, d_norm_b, w_br, w_out, ln_g, ln_b):
    b, t, d = x.shape
    t_ctx = ctx.shape[1]
    depth = w_mod.shape[0]
    assert d == D_MODEL and b < 16
    alpha = (2 * depth) ** 0.25

    cc = jnp.concatenate([c, c_ctx[None], jnp.zeros((16 - b - 1, D_MODEL), F32)], axis=0)
    mod = _modulation(cc, w_mod, b_mod)
    cos, sin = _rope_tables(t)
    consts = _gla_constants()
    w_main, w_r = _projection_weights(w_in)
    w_br_bf, w_out_bf = w_br.astype(BF16), w_out.astype(BF16)
    n_pairs = C_HEADS // 2

    h_lat = x.reshape(b * t, D_MODEL)
    h_ctx = ctx.reshape(b * t_ctx, D_MODEL)
    for l in range(depth):
        want_ctx = l < depth - 1
        w2, b2 = _gate_weights(c_gate_w2[l], c_gate_b[l])
        q_gain, k_gain = q_norm[l].reshape(1, HEAD_DIM), k_norm[l].reshape(1, HEAD_DIM)
        c_gain = c_norm[l].reshape(n_pairs, 1, PAIR_V)
        shift, scale, gate = [mod[l, :, n * D_MODEL:(n + 1) * D_MODEL].reshape(16, 1, D_MODEL) for n in range(3)]
        p_lat, r_lat = _in_projection(h_lat, shift, scale, t, w_main, w_r, l)
        ctx_tiles = None if want_ctx else tuple(c // CTX_KV_TN for c in CTX_KV_TILES)
        ctx_cols = FULL_COLS if want_ctx else CTX_KV_COLS
        p_ctx, r_ctx = _in_projection(h_ctx, shift[b:], scale[b:], b * t_ctx, w_main, w_r, l, ctx_tiles,
                                      PROJ_TN if want_ctx else CTX_KV_TN)
        p_lat3 = p_lat.reshape(b, t, N_MAIN)
        p_ctx3 = p_ctx.reshape(b, t_ctx, p_ctx.shape[1])
        r_lat3 = r_lat.reshape(b, t, LANES)
        r_ctx3 = r_ctx.reshape(b, t_ctx, LANES)
        w_br_l, w_out_l = w_br_bf[l], w_out_bf[l]

        ya_l = _attention(p_lat3, p_lat3, p_ctx3, q_gain, k_gain, cos, sin, ctx_cols)
        yb_l, yd_l = _local_branches(p_lat, t, b_conv[l], d_conv_w[l], d_conv_b[l], d_norm_g[l], d_norm_b[l])
        yc_l, yc_c = _gla(p_lat3, r_lat3, p_ctx3, r_ctx3, w2, b2, c_gain, consts, want_ctx, ctx_cols)
        h_lat_new = _merge(p_lat, ya_l.reshape(b * t, BR_W), yb_l, yc_l.reshape(b * t, BR_W), yd_l,
                           h_lat, gate, t, w_br_l, w_out_l, ln_g[l], ln_b[l], alpha)
        if want_ctx:
            ya_c = _attention(p_ctx3, None, p_ctx3, q_gain, k_gain, None, None)
            yb_c, yd_c = _local_branches(p_ctx, t_ctx, b_conv[l], d_conv_w[l], d_conv_b[l], d_norm_g[l],
                                         d_norm_b[l])
            h_ctx = _merge(p_ctx, ya_c.reshape(b * t_ctx, BR_W), yb_c, yc_c.reshape(b * t_ctx, BR_W), yd_c,
                           h_ctx, gate[b:], b * t_ctx, w_br_l, w_out_l, ln_g[l], ln_b[l], alpha)
        h_lat = h_lat_new
    return h_lat.reshape(b, t, D_MODEL)
```

```python
import functools

import numpy as np
import jax
import jax.numpy as jnp
from jax import lax
from jax.experimental import pallas as pl
from jax.experimental.pallas import tpu as pltpu

F32 = jnp.float32
BF16 = jnp.bfloat16

D_MODEL = 1024
GRID_W = 64
N_BRANCH = 4
BR_W = D_MODEL // 2
HEAD_DIM = 128
A_HEADS = BR_W // HEAD_DIM
A_KV_HEADS = A_HEADS // 2
ROPE_THETA = 10000.0
AXIS_DIM = HEAD_DIM // 2
B_CONV = 3
C_HEADS = 4
C_HEAD_K = BR_W // (2 * C_HEADS)
C_HEAD_V = BR_W // C_HEADS
C_KEY_W = C_HEADS * C_HEAD_K
C_GATE_RANK = 16
C_GATE_TAU = 16.0
C_CHUNK = 64
D_CONV = 31
EPS = 1e-6
IN_WIDTHS = (
    A_HEADS * HEAD_DIM, A_KV_HEADS * HEAD_DIM, A_KV_HEADS * HEAD_DIM, BR_W,
    BR_W, BR_W, BR_W, BR_W,
    C_KEY_W, C_KEY_W, C_HEADS * C_HEAD_V, BR_W, 2 * C_GATE_RANK,
    2 * BR_W, BR_W,
    N_BRANCH * D_MODEL,
)

COL_AQ = 0
COL_AK = COL_AQ + 512
COL_AV = COL_AK + 256
COL_AZ = COL_AV + 256
COL_BG = COL_AZ + 512
COL_BC = COL_BG + 512
COL_BX = COL_BC + 512
COL_BZ = COL_BX + 512
COL_CQ = COL_BZ + 512
COL_CK = COL_CQ + 256
COL_CV = COL_CK + 256
COL_CZ = COL_CV + 512
COL_DA = COL_CZ + 512
COL_DG = COL_DA + 512
COL_DZ = COL_DG + 512
COL_MG = COL_DZ + 512
MG_BLK = 512
MERGE_SPLIT = 1
N_MAIN = COL_MG + N_BRANCH * D_MODEL
FULL_COLS = dict(ak=COL_AK, av=COL_AV, cq=COL_CQ, ck=COL_CK, cv=COL_CV, cz=COL_CZ)
CTX_KV_TILES = (COL_AK, COL_CQ, COL_CV)
CTX_KV_COLS = dict(ak=0, av=256, cq=512, ck=768, cv=1024, cz=1024)
LANES = 128
SUBLANES = 8
PROJ_TN = 1536
CTX_KV_TN = 512
LN_GROUP = 512
VMEM_LIMIT = 56 * 1024 * 1024
LOG2E = 1.4426950408889634


def _cparams(n_axes):
    return pltpu.CompilerParams(dimension_semantics=("arbitrary",) * n_axes,
                                vmem_limit_bytes=VMEM_LIMIT)


def _sigmoid(x):
    return 0.5 * jnp.tanh(0.5 * x) + 0.5


def _silu(x):
    h = 0.5 * x
    return h + h * jnp.tanh(h)


def _dot(a, b):
    return jnp.dot(a, b, preferred_element_type=F32)


def _dot_nt(a, b):
    return lax.dot_general(a, b, (((1,), (1,)), ((), ())), preferred_element_type=F32)


def _dot_tn(a, b):
    return lax.dot_general(a, b, (((0,), (0,)), ((), ())), preferred_element_type=F32)


def _mod_kernel(c_ref, w_ref, b_ref, o_ref):
    s = _silu(c_ref[...])
    o_ref[0] = _dot(s.astype(BF16), w_ref[0].astype(BF16)) + b_ref[0]


def _modulation(cc, w_mod, b_mod):
    depth = w_mod.shape[0]
    n_rows = cc.shape[0]
    return pl.pallas_call(
        _mod_kernel,
        grid=(depth, 1),
        in_specs=[
            pl.BlockSpec((n_rows, D_MODEL), lambda l, j: (0, 0)),
            pl.BlockSpec((1, D_MODEL, 3 * D_MODEL), lambda l, j: (l, 0, j)),
            pl.BlockSpec((1, 1, 3 * D_MODEL), lambda l, j: (l, 0, j)),
        ],
        out_specs=pl.BlockSpec((1, n_rows, 3 * D_MODEL), lambda l, j: (l, 0, j)),
        out_shape=jax.ShapeDtypeStruct((depth, n_rows, 3 * D_MODEL), F32),
        compiler_params=_cparams(2),
        name="modulation",
    )(cc, w_mod, b_mod.reshape(depth, 1, 3 * D_MODEL))


def _inproj_kernel(x_ref, shift_ref, scale_ref, w_ref, wr_ref, o_ref, or_ref, u_ref, *, tm, sub):
    j = pl.program_id(1)

    @pl.when(j == 0)
    def _():
        one_plus = 1.0 + scale_ref[0]
        shift = shift_ref[0]
        grp = min(LN_GROUP, tm)
        for g0 in range(0, tm, grp):
            for r0 in range(g0, g0 + grp, sub):
                x = x_ref[r0:r0 + sub, :]
                mu = jnp.mean(x, axis=-1, keepdims=True)
                xc = x - mu
                var = jnp.mean(xc * xc, axis=-1, keepdims=True)
                u = xc * lax.rsqrt(var + EPS) * one_plus + shift
                u_ref[r0:r0 + sub, :] = u.astype(BF16)
            u_grp = u_ref[g0:g0 + grp, :]
            or_ref[g0:g0 + grp, :] = _dot(u_grp, wr_ref[...]).astype(BF16)
            o_ref[g0:g0 + grp, :] = _dot(u_grp, w_ref[...]).astype(BF16)

    @pl.when(j != 0)
    def _():
        o_ref[...] = _dot(u_ref[...], w_ref[...]).astype(BF16)


def _in_projection(h, shift, scale, rows_per_mod, w_main, w_r, layer, tiles=None, tn=PROJ_TN):
    m = h.shape[0]
    tm = min(2048, rows_per_mod)
    assert m % tm == 0 and rows_per_mod % tm == 0
    tiles_per_mod = rows_per_mod // tm
    assert w_main.shape[2] % tn == 0
    n_tiles = len(tiles) if tiles else w_main.shape[2] // tn
    n_cols = n_tiles * tn

    def w_tile(j):
        if not tiles:
            return j
        idx = tiles[0]
        for n in range(1, len(tiles)):
            idx = idx + (tiles[n] - tiles[n - 1]) * jnp.minimum(jnp.maximum(j - n + 1, 0), 1)
        return idx

    kern = functools.partial(_inproj_kernel, tm=tm, sub=min(128, tm))
    return pl.pallas_call(
        kern,
        grid=(m // tm, n_tiles),
        in_specs=[
            pl.BlockSpec((tm, D_MODEL), lambda i, j: (i, 0)),
            pl.BlockSpec((1, 1, D_MODEL), lambda i, j: (i // tiles_per_mod, 0, 0)),
            pl.BlockSpec((1, 1, D_MODEL), lambda i, j: (i // tiles_per_mod, 0, 0)),
            pl.BlockSpec((None, D_MODEL, tn), lambda i, j: (layer, 0, w_tile(j))),
            pl.BlockSpec((None, D_MODEL, LANES), lambda i, j: (layer, 0, 0)),
        ],
        out_specs=[
            pl.BlockSpec((tm, tn), lambda i, j: (i, j)),
            pl.BlockSpec((tm, LANES), lambda i, j: (i, 0)),
        ],
        out_shape=[
            jax.ShapeDtypeStruct((m, n_cols), BF16),
            jax.ShapeDtypeStruct((m, LANES), BF16),
        ],
        scratch_shapes=[pltpu.VMEM((tm, D_MODEL), BF16)],
        compiler_params=_cparams(2),
        name="in_projection",
    )(h, shift, scale, w_main, w_r)


ATTN_KEY_BLOCKS = (768, 512, 256)
QK_AHEAD = 2


def _rms(x):
    return x * lax.rsqrt(jnp.mean(x * x, axis=-1, keepdims=True) + EPS)


def _rope(x, cos, sin):
    lane = lax.broadcasted_iota(jnp.int32, x.shape, 1)
    swapped = jnp.where(lane % 2 == 0, pltpu.roll(x, HEAD_DIM - 1, 1), pltpu.roll(x, 1, 1))
    return x * cos + swapped * sin


def _attn_kernel(*refs, n_lat, n_ctx, tq, kblk, kstep):
    if n_lat:
        (q_ref, z_ref, kl_ref, vl_ref, kc_ref, vc_ref, qg_ref, kg_ref, cos_ref, sin_ref,
         o_ref, kn_ref, vn_ref) = refs
    else:
        q_ref, z_ref, kc_ref, vc_ref, qg_ref, kg_ref, o_ref, kn_ref, vn_ref = refs
    qi = pl.program_id(2)
    kg = kg_ref[...]

    @pl.when(qi == 0)
    def _prep():
        if n_lat:
            def body(t, carry):
                r0 = pl.multiple_of(t * kblk, kblk)
                kn = _rms(kl_ref[0, pl.ds(r0, kblk), :].astype(F32)) * kg
                kn = _rope(kn, cos_ref[pl.ds(r0, kblk), :], sin_ref[pl.ds(r0, kblk), :])
                kn_ref[pl.ds(r0, kblk), :] = kn.astype(BF16)
                vn_ref[pl.ds(r0, kblk), 0:HEAD_DIM] = vl_ref[0, pl.ds(r0, kblk), :]
                return carry

            lax.fori_loop(0, n_lat // kblk, body, 0)
        kn_ref[n_lat:n_lat + n_ctx, :] = (_rms(kc_ref[0].astype(F32)) * kg).astype(BF16)
        vn_ref[n_lat:n_lat + n_ctx, 0:HEAD_DIM] = vc_ref[0]
        vn_ref[:, HEAD_DIM:2 * HEAD_DIM] = jnp.ones((n_lat + n_ctx, HEAD_DIM), BF16)

    qg = qg_ref[...]
    if n_lat:
        q0 = pl.multiple_of(qi * tq, tq)
        cos = cos_ref[pl.ds(q0, tq), :]
        sin = sin_ref[pl.ds(q0, tq), :]
    qns = []
    for g in range(2):
        qn = _rms(q_ref[0, :, g * HEAD_DIM:(g + 1) * HEAD_DIM].astype(F32)) * qg
        if n_lat:
            qn = _rope(qn, cos, sin)
        qns.append((qn * (HEAD_DIM ** -0.5 * LOG2E)).astype(BF16))

    items = [(g, k0) for k0 in range(0, n_lat + n_ctx, kstep) for g in range(2)]

    def scores(item):
        g, k0 = item
        return _dot_nt(qns[g], kn_ref[k0:k0 + kstep, :])

    m = [None, None]
    acc = [None, None]
    ahead = [scores(item) for item in items[:QK_AHEAD]]
    for n, (g, k0) in enumerate(items):
        s = ahead.pop(0)
        if n + QK_AHEAD < len(items):
            ahead.append(scores(items[n + QK_AHEAD]))
        m_blk = jnp.max(s, axis=-1, keepdims=True)
        m_new = m_blk if m[g] is None else jnp.maximum(m[g], m_blk)
        p = jnp.exp2((s - m_new).astype(BF16))
        pv = _dot(p, vn_ref[k0:k0 + kstep, :])
        acc[g] = pv if acc[g] is None else acc[g] * jnp.exp2(m[g] - m_new) + pv
        m[g] = m_new
    for g in range(2):
        lanes = slice(g * HEAD_DIM, (g + 1) * HEAD_DIM)
        o = acc[g][:, 0:HEAD_DIM] / acc[g][:, HEAD_DIM:2 * HEAD_DIM]
        o_ref[0, :, lanes] = (o * _silu(z_ref[0, :, lanes].astype(F32))).astype(BF16)


def _attention(p_q, p_lat, p_ctx, q_gain, k_gain, cos, sin, ctx_cols=None):
    b, t_q, _ = p_q.shape
    n_ctx = p_ctx.shape[1]
    n_lat = 0 if p_lat is None else p_lat.shape[1]
    tq = min(1024, t_q)
    assert t_q % tq == 0
    kblk = min(1024, n_lat) if n_lat else 0
    n_keys = n_lat + n_ctx
    kstep = next((c for c in ATTN_KEY_BLOCKS if n_keys % c == 0), n_keys)
    hw = 2 * HEAD_DIM
    q_spec = pl.BlockSpec((1, tq, hw), lambda i, kv, qi: (i, qi, COL_AQ // hw + kv))
    z_spec = pl.BlockSpec((1, tq, hw), lambda i, kv, qi: (i, qi, COL_AZ // hw + kv))

    def kv_spec(n, col):
        return pl.BlockSpec((1, n, HEAD_DIM), lambda i, kv, qi: (i, 0, col // HEAD_DIM + kv))

    gain_spec = pl.BlockSpec((1, HEAD_DIM), lambda i, kv, qi: (0, 0))
    args = [p_q, p_q]
    specs = [q_spec, z_spec]
    if n_lat:
        args += [p_lat, p_lat]
        specs += [kv_spec(n_lat, COL_AK), kv_spec(n_lat, COL_AV)]
    ctx_cols = ctx_cols or FULL_COLS
    args += [p_ctx, p_ctx, q_gain, k_gain]
    specs += [kv_spec(n_ctx, ctx_cols["ak"]), kv_spec(n_ctx, ctx_cols["av"]), gain_spec, gain_spec]
    if n_lat:
        tab_spec = pl.BlockSpec((n_lat, HEAD_DIM), lambda i, kv, qi: (0, 0))
        args += [cos, sin]
        specs += [tab_spec, tab_spec]
    kern = functools.partial(_attn_kernel, n_lat=n_lat, n_ctx=n_ctx, tq=tq, kblk=kblk, kstep=kstep)
    return pl.pallas_call(
        kern,
        grid=(b, A_KV_HEADS, t_q // tq),
        in_specs=specs,
        out_specs=pl.BlockSpec((1, tq, hw), lambda i, kv, qi: (i, qi, kv)),
        out_shape=jax.ShapeDtypeStruct((b, t_q, BR_W), BF16),
        scratch_shapes=[pltpu.VMEM((n_lat + n_ctx, HEAD_DIM), BF16),
                        pltpu.VMEM((n_lat + n_ctx, 2 * HEAD_DIM), BF16)],
        compiler_params=_cparams(3),
        name="attention_lat" if n_lat else "attention_ctx",
    )(*args)


D_HALO = 16
B_HALO = 16
CONV_RB = 32
CONV_ACCS = 4
ELEM_RB = 32
COPY_RB = 56


def _local_kernel(bg_ref, bc_ref, bcp_ref, bcn_ref, bx_ref, bxp_ref, bxn_ref, bz_ref,
                  da_ref, dap_ref, dan_ref, dg_ref, dgp_ref, dgn_ref, dz_ref,
                  bw_ref, dw_ref, db_ref, dgain_ref, dbeta_ref,
                  yb_ref, yd_ref, tbuf, gbuf, hbuf, *, tt, tiles_per_seq):
    i = pl.program_id(0)
    pos = i % tiles_per_seq
    keep_prev = (pos != 0).astype(F32)
    keep_next = (pos != tiles_per_seq - 1).astype(F32)

    def ld(ref, r0=0, n=None):
        return ref[r0:r0 + (n or ref.shape[0]), :].astype(F32)

    row_blocks = range(0, tt, ELEM_RB)

    tbuf[0:B_HALO, :] = ld(bcp_ref) * ld(bxp_ref) * keep_prev
    for r0 in row_blocks:
        tbuf[B_HALO + r0:B_HALO + r0 + ELEM_RB, :] = ld(bc_ref, r0, ELEM_RB) * ld(bx_ref, r0, ELEM_RB)
    tbuf[B_HALO + tt:2 * B_HALO + tt, :] = ld(bcn_ref) * ld(bxn_ref) * keep_next
    for r0 in row_blocks:
        lo = B_HALO + r0
        conv = (bw_ref[0:1, :] * tbuf[lo - 1:lo - 1 + ELEM_RB, :] + bw_ref[1:2, :] * tbuf[lo:lo + ELEM_RB, :]
                + bw_ref[2:3, :] * tbuf[lo + 1:lo + 1 + ELEM_RB, :])
        yb_ref[r0:r0 + ELEM_RB, :] = (ld(bg_ref, r0, ELEM_RB) * conv
                                      * _silu(ld(bz_ref, r0, ELEM_RB))).astype(BF16)

    gbuf[0, 0:D_HALO, :] = ld(dap_ref) * _sigmoid(ld(dgp_ref)) * keep_prev
    for r0 in row_blocks:
        gbuf[0, D_HALO + r0:D_HALO + r0 + ELEM_RB, :] = (ld(da_ref, r0, ELEM_RB)
                                                          * _sigmoid(ld(dg_ref, r0, ELEM_RB)))
    gbuf[0, D_HALO + tt:2 * D_HALO + tt, :] = ld(dan_ref) * _sigmoid(ld(dgn_ref)) * keep_next
    span = tt + 2 * D_HALO - SUBLANES
    for r in range(1, SUBLANES):
        for x0 in range(0, span, COPY_RB):
            n = min(COPY_RB, span - x0)
            gbuf[r, x0:x0 + n, :] = gbuf[0, r + x0:r + x0 + n, :]
    base = D_HALO - D_CONV // 2
    for cb in range(BR_W // LANES):
        lanes = slice(cb * LANES, (cb + 1) * LANES)
        for rb in range(tt // CONV_RB):
            accs = [None] * CONV_ACCS
            for k in range(D_CONV):
                off = base + k
                r0 = rb * CONV_RB + off - off % SUBLANES
                rows = gbuf[off % SUBLANES, r0:r0 + CONV_RB, lanes]
                term = rows.reshape(CONV_RB // SUBLANES, SUBLANES, LANES) * dw_ref[k, :, lanes][None]
                a = k % CONV_ACCS
                accs[a] = term if accs[a] is None else accs[a] + term
            while len(accs) > 1:
                accs = [accs[n] + accs[n + 1] for n in range(0, len(accs), 2)]
            hbuf[rb * CONV_RB:(rb + 1) * CONV_RB, lanes] = accs[0].reshape(CONV_RB, LANES)
    for r0 in row_blocks:
        hh = hbuf[r0:r0 + ELEM_RB, :] + db_ref[...]
        mu = jnp.mean(hh, axis=-1, keepdims=True)
        hc = hh - mu
        var = jnp.mean(hc * hc, axis=-1, keepdims=True)
        hn = hc * lax.rsqrt(var + EPS) * dgain_ref[...] + dbeta_ref[...]
        yd_ref[r0:r0 + ELEM_RB, :] = (_silu(hn) * _silu(ld(dz_ref, r0, ELEM_RB))).astype(BF16)


def _local_branches(p2, seq_len, b_w, d_w, d_b, d_g, d_beta):
    m = p2.shape[0]
    tt = min(512, seq_len)
    assert seq_len % tt == 0 and m % seq_len == 0
    tiles_per_seq = seq_len // tt
    n_tiles = m // tt

    def cur(col):
        return pl.BlockSpec((tt, BR_W), lambda i: (i, col // BR_W))

    def prev(col, halo):
        per = tt // halo
        return pl.BlockSpec((halo, BR_W), lambda i: (jnp.maximum(i * per - 1, 0), col // BR_W))

    def nxt(col, halo):
        per = tt // halo
        last = m // halo - 1
        return pl.BlockSpec((halo, BR_W), lambda i: (jnp.minimum((i + 1) * per, last), col // BR_W))

    def small(rows):
        return pl.BlockSpec((rows, BR_W), lambda i: (0, 0))

    specs = [cur(COL_BG),
             cur(COL_BC), prev(COL_BC, B_HALO), nxt(COL_BC, B_HALO),
             cur(COL_BX), prev(COL_BX, B_HALO), nxt(COL_BX, B_HALO),
             cur(COL_BZ),
             cur(COL_DA), prev(COL_DA, D_HALO), nxt(COL_DA, D_HALO),
             cur(COL_DG), prev(COL_DG, D_HALO), nxt(COL_DG, D_HALO),
             cur(COL_DZ),
             small(B_CONV), pl.BlockSpec((D_CONV, SUBLANES, BR_W), lambda i: (0, 0, 0)),
             small(1), small(1), small(1)]
    kern = functools.partial(_local_kernel, tt=tt, tiles_per_seq=tiles_per_seq)
    out_spec = pl.BlockSpec((tt, BR_W), lambda i: (i, 0))
    return pl.pallas_call(
        kern,
        grid=(n_tiles,),
        in_specs=specs,
        out_specs=[out_spec, out_spec],
        out_shape=[jax.ShapeDtypeStruct((m, BR_W), BF16), jax.ShapeDtypeStruct((m, BR_W), BF16)],
        scratch_shapes=[pltpu.VMEM((tt + 2 * B_HALO, BR_W), F32),
                        pltpu.VMEM((SUBLANES, tt + 2 * D_HALO, BR_W), F32),
                        pltpu.VMEM((tt, BR_W), F32)],
        compiler_params=_cparams(1),
        name="local_branches",
    )(*([p2] * 15), b_w, jnp.broadcast_to(d_w[:, None, :], (D_CONV, SUBLANES, BR_W)),
      d_b.reshape(1, BR_W), d_g.reshape(1, BR_W), d_beta.reshape(1, BR_W))


CH = C_CHUNK
PAIR_K = 2 * C_HEAD_K
PAIR_V = 2 * C_HEAD_V
LEVELS = (8, 4, 2)
BLK = SUBLANES
N_BLK = CH // BLK
GLA_UNROLL = 16
ROW_EQ = 0
ROW_EK = CH
ROW_LAST = 2 * CH
ROW_FK = 2 * CH + 8
ROW_LVL = ROW_FK + CH
ROW_FQ = ROW_LVL + len(LEVELS) * CH
N_FQ = BLK * (N_BLK * (N_BLK - 1) // 2)
N_EROWS = ROW_FQ + N_FQ


def _slabs(reverse):
    if reverse:
        return [(jb, 0, BLK * jb) for jb in range(1, N_BLK)]
    return [(jb, BLK * (jb + 1), CH - BLK * (jb + 1)) for jb in range(N_BLK - 1)]


def _gla_constants():
    idx = np.arange(CH)
    cols = np.arange(2 * CH) % CH
    emats, masks = [], []
    for reverse in (False, True):
        tri = (idx[None, :] >= idx[:, None]) if reverse else (idx[None, :] <= idx[:, None])
        tri = tri.astype(np.float32)
        edge = tri[0] if reverse else tri[CH - 1]
        blk_edge = (idx // BLK) * BLK + (0 if reverse else BLK - 1)
        blocks = [tri, edge[None, :] - tri, np.tile(edge[None, :], (8, 1)), tri[blk_edge] - tri]
        lvl_masks = []
        for grp in LEVELS:
            half = grp // 2
            in_q = (idx % grp < half) if reverse else (idx % grp >= half)
            ref = (idx // grp) * grp + (half if reverse else half - 1)
            d = tri - tri[ref]
            blocks.append(np.where(in_q[:, None], d, -d))
            in_q_col = (cols % grp < half) if reverse else (cols % grp >= half)
            lvl_masks.append((idx[:, None] // grp == cols[None, :] // grp) & in_q[:, None] & ~in_q_col[None, :])
        lvl_masks.append(idx[:, None] == cols[None, :])
        for jb, q0, nq in _slabs(reverse):
            edge_row = BLK * jb + (0 if reverse else BLK - 1)
            blocks.append(tri[q0:q0 + nq] - tri[edge_row][None, :])
        emat = np.concatenate(blocks, axis=0)
        assert emat.shape == (N_EROWS, CH) and emat.min() >= -1 and emat.max() <= 1
        emats.append(np.concatenate([emat] * 3, axis=1))
        masks.append(np.stack(lvl_masks).astype(np.float32))
    return (jnp.asarray(emats[0], BF16), jnp.asarray(emats[1], BF16), jnp.asarray(np.stack(masks), F32))


def _split3(g):
    g1 = g.astype(BF16)
    r1 = g - g1.astype(F32)
    g2 = r1.astype(BF16)
    g3 = (r1 - g2.astype(F32)).astype(BF16)
    return g1, g2, g3


def _log2_sigmoid(x):
    e = jnp.exp2(jnp.abs(x) * -LOG2E)
    return jnp.minimum(x, 0.0) * LOG2E - jnp.log2(1.0 + e)


def _gla_kernel(*refs, t_lat, t_ctx, want_ctx):
    (ql_ref, kl_ref, vl_ref, zl_ref, rl_ref, qc_ref, kc_ref, vc_ref, zc_ref, rc_ref,
     w2_ref, b2_ref, gain_ref, ematf_ref, ematb_ref, masks_ref) = refs[:16]
    if want_ctx:
        yl_ref, yc_ref = refs[16:18]
        scratch = refs[18:]
    else:
        yl_ref = refs[16]
        scratch = refs[17:]
        yc_ref = zc_ref = None
    stf_ref, stb_ref, of_ref, ob_ref, g1_ref, g2_ref, g3_ref = scratch

    def gates(r_ref, n_rows):
        blk = min(512, n_rows)

        def body(t, carry):
            rows = pl.ds(pl.multiple_of(t * blk, blk), blk)
            x = _dot(r_ref[0, rows, :], w2_ref[...]) + b2_ref[...]
            g1, g2, g3 = _split3(_log2_sigmoid(x) * (1.0 / C_GATE_TAU))
            g1_ref[rows, :] = g1
            g2_ref[rows, :] = g2
            g3_ref[rows, :] = g3
            return carry

        lax.fori_loop(0, n_rows // blk, body, 0)

    def run(q_ref, k_ref, v_ref, n_chunks, want_out):
        unroll = min(GLA_UNROLL, n_chunks)
        assert n_chunks % unroll == 0
        lane_k = lax.broadcasted_iota(jnp.int32, (CH, PAIR_K), 1)
        head0 = lane_k < C_HEAD_K
        zero_v = jnp.zeros((CH, C_HEAD_V), BF16)
        key_blk = (lax.broadcasted_iota(jnp.int32, (BLK, 2 * CH), 1) % CH) // BLK

        def body(i, carry):
            work = []
            for reverse in (False, True):
                lanes = slice(PAIR_K, 2 * PAIR_K) if reverse else slice(0, PAIR_K)
                rows = []
                for u in range(unroll):
                    c = i * unroll + u
                    c = (n_chunks - 1 - c) if reverse else c
                    rows.append(pl.ds(pl.multiple_of(c * CH, CH), CH))
                rhs = jnp.concatenate(
                    [jnp.concatenate([g_ref[r, lanes] for r in rows], axis=1)
                     for g_ref in (g1_ref, g2_ref, g3_ref)], axis=0)
                emat = ematb_ref[...] if reverse else ematf_ref[...]
                e_all = jnp.exp2(_dot(emat, rhs))
                for u in range(unroll):
                    work.append(dict(reverse=reverse, rows=rows[u], u=u,
                                     e=e_all[:, u * PAIR_K:(u + 1) * PAIR_K]))
            work.sort(key=lambda w: w["u"])

            for w in work:
                st_ref = stb_ref if w["reverse"] else stf_ref
                e = w["e"]
                q = q_ref[0, w["rows"], :].astype(F32) * C_HEAD_K ** -0.5
                k = k_ref[0, w["rows"], :].astype(F32)
                v = v_ref[0, w["rows"], :]
                k0 = jnp.where(head0, k, 0.0)
                k1 = jnp.where(head0, 0.0, k)
                st = st_ref[...]
                e_k = e[ROW_EK:ROW_EK + CH]
                k_dec = jnp.concatenate([k0 * e_k, k1 * e_k], axis=0).astype(BF16)
                v_rows = jnp.concatenate([v[:, :C_HEAD_V], v[:, C_HEAD_V:]], axis=0)
                st_ref[...] = e[ROW_LAST:ROW_LAST + 1] * st + _dot_tn(v_rows, k_dec)
                if want_out:
                    qe = q * e[ROW_EQ:ROW_EQ + CH]
                    q_rows = jnp.concatenate([jnp.where(head0, qe, 0.0), jnp.where(head0, 0.0, qe)], axis=0)
                    o_st = _dot(q_rows.astype(BF16), st.T.astype(BF16))
                    w.update(q=q, k0=k0, k1=k1, v=v, o=jnp.concatenate([o_st[:CH], o_st[CH:]], axis=1))
            if not want_out:
                return carry

            for w in work:
                masks = masks_ref.at[1 if w["reverse"] else 0]
                q, k0, k1, e = w["q"], w["k0"], w["k1"], w["e"]
                fk = e[ROW_FK:ROW_FK + CH]
                kk_t = jnp.concatenate([k0 * fk, k1 * fk], axis=0).T.astype(BF16)
                slabs = _slabs(w["reverse"])
                lhs, off = [], ROW_FQ
                for jb, q0, nq in slabs:
                    lhs.append(q[q0:q0 + nq] * e[off:off + nq])
                    off += nq
                r = _dot(jnp.concatenate(lhs, axis=0).astype(BF16), kk_t)
                rows, off = [jnp.zeros((BLK, 2 * CH), F32)] * N_BLK, 0
                for jb, q0, nq in slabs:
                    for ib in range(q0 // BLK, (q0 + nq) // BLK):
                        piece = r[off + ib * BLK - q0:off + (ib + 1) * BLK - q0]
                        rows[ib] = jnp.where(key_blk == jb, piece, rows[ib])
                    off += nq
                scores = jnp.concatenate(rows, axis=0)
                k_heads = jnp.concatenate([k0, k1], axis=0)
                scores = scores + _dot(q.astype(BF16), k_heads.T.astype(BF16)) * masks[len(LEVELS)]
                for lvl in range(len(LEVELS)):
                    e_l = e[ROW_LVL + lvl * CH:ROW_LVL + (lvl + 1) * CH]
                    kk_t = jnp.concatenate([k0 * e_l, k1 * e_l], axis=0).T.astype(BF16)
                    scores = scores + _dot((q * e_l).astype(BF16), kk_t) * masks[lvl]
                w["scores"] = scores.astype(BF16)

            for w in work:
                v = w["v"]
                v_bd = jnp.concatenate([jnp.concatenate([v[:, :C_HEAD_V], zero_v], axis=1),
                                        jnp.concatenate([zero_v, v[:, C_HEAD_V:]], axis=1)], axis=0)
                (ob_ref if w["reverse"] else of_ref)[w["rows"], :] = w["o"] + _dot(w["scores"], v_bd)
            return carry

        lax.fori_loop(0, n_chunks // unroll, body, 0)

    def finish(z_ref, y_ref, n_rows):
        blk = min(256, n_rows)

        def body(t, carry):
            rows = pl.ds(pl.multiple_of(t * blk, blk), blk)
            o = of_ref[rows, :] + ob_ref[rows, :]
            z = z_ref[0, rows, :].astype(F32)
            gain = gain_ref[...]
            for h in range(2):
                lanes = slice(h * C_HEAD_V, (h + 1) * C_HEAD_V)
                y = _rms(o[:, lanes]) * gain[:, lanes]
                y_ref[0, rows, lanes] = (y * _silu(z[:, lanes])).astype(BF16)
            return carry

        lax.fori_loop(0, n_rows // blk, body, 0)

    stf_ref[...] = jnp.zeros_like(stf_ref)
    stb_ref[...] = jnp.zeros_like(stb_ref)
    gates(rc_ref, t_ctx)
    run(qc_ref, kc_ref, vc_ref, t_ctx // CH, want_ctx)
    if want_ctx:
        finish(zc_ref, yc_ref, t_ctx)
    gates(rl_ref, t_lat)
    run(ql_ref, kl_ref, vl_ref, t_lat // CH, True)
    finish(zl_ref, yl_ref, t_lat)


def _gla(p_lat, r_lat, p_ctx, r_ctx, w2, b2, gain, consts, want_ctx, ctx_cols=None):
    b, t_lat, _ = p_lat.shape
    t_ctx = p_ctx.shape[1]
    n_pairs = C_HEADS // 2

    def side(t, with_z, cols):
        z_rows = t if with_z else 16
        return [pl.BlockSpec((1, t, PAIR_K), lambda i, hp: (i, 0, cols["cq"] // PAIR_K + hp)),
                pl.BlockSpec((1, t, PAIR_K), lambda i, hp: (i, 0, cols["ck"] // PAIR_K + hp)),
                pl.BlockSpec((1, t, PAIR_V), lambda i, hp: (i, 0, cols["cv"] // PAIR_V + hp)),
                pl.BlockSpec((1, z_rows, PAIR_V), lambda i, hp: (i, 0, cols["cz"] // PAIR_V + hp)),
                pl.BlockSpec((1, t, LANES), lambda i, hp: (i, 0, 0))]

    def whole(shape):
        nd = len(shape)
        return pl.BlockSpec(shape, lambda i, hp: (0,) * nd)

    emat_f, emat_b, masks = consts
    specs = side(t_lat, True, FULL_COLS) + side(t_ctx, want_ctx, ctx_cols or FULL_COLS) + [
        pl.BlockSpec((1, LANES, 2 * PAIR_K), lambda i, hp: (hp, 0, 0)),
        pl.BlockSpec((1, 1, 2 * PAIR_K), lambda i, hp: (hp, 0, 0)),
        pl.BlockSpec((1, 1, PAIR_V), lambda i, hp: (hp, 0, 0)),
        whole(emat_f.shape), whole(emat_b.shape), whole(masks.shape)]
    out_specs = [pl.BlockSpec((1, t_lat, PAIR_V), lambda i, hp: (i, 0, hp))]
    out_shape = [jax.ShapeDtypeStruct((b, t_lat, BR_W), BF16)]
    if want_ctx:
        out_specs.append(pl.BlockSpec((1, t_ctx, PAIR_V), lambda i, hp: (i, 0, hp)))
        out_shape.append(jax.ShapeDtypeStruct((b, t_ctx, BR_W), BF16))

    def kern(*refs):
        refs = list(refs)
        for n in range(10, 13):
            refs[n] = refs[n].at[0]
        _gla_kernel(*refs, t_lat=t_lat, t_ctx=t_ctx, want_ctx=want_ctx)

    res = pl.pallas_call(
        kern,
        grid=(b, n_pairs),
        in_specs=specs,
        out_specs=out_specs,
        out_shape=out_shape,
        scratch_shapes=[pltpu.VMEM((C_HEAD_V, PAIR_K), F32), pltpu.VMEM((C_HEAD_V, PAIR_K), F32),
                        pltpu.VMEM((t_lat, PAIR_V), F32), pltpu.VMEM((t_lat, PAIR_V), F32),
                        pltpu.VMEM((t_lat, 2 * PAIR_K), BF16), pltpu.VMEM((t_lat, 2 * PAIR_K), BF16),
                        pltpu.VMEM((t_lat, 2 * PAIR_K), BF16)],
        compiler_params=_cparams(2),
        name="gla",
    )(p_lat, p_lat, p_lat, p_lat, r_lat, p_ctx, p_ctx, p_ctx, p_ctx, r_ctx,
      w2, b2, gain, emat_f, emat_b, masks)
    return (res[0], res[1]) if want_ctx else (res[0], None)


def _merge_kernel(*refs, alpha):
    n_mg = N_BRANCH * D_MODEL // MG_BLK
    mg_refs = refs[:n_mg]
    ya_ref, yb_ref, yc_ref, yd_ref, h_ref, gate_ref, wbr_ref, wout_ref, lng_ref, lnb_ref, o_ref = refs[n_mg:]
    per = D_MODEL // MG_BLK
    tm = o_ref.shape[0]
    rows_per = tm // MERGE_SPLIT
    for r0 in range(0, tm, rows_per):
        rows = slice(r0, r0 + rows_per)
        acc = None
        for n, y_ref in enumerate((ya_ref, yb_ref, yc_ref, yd_ref)):
            mg = jnp.concatenate([r[rows, :] for r in mg_refs[n * per:(n + 1) * per]], axis=1)
            gate = _sigmoid(mg.astype(F32))
            term = gate * _dot(y_ref[rows, :], wbr_ref[n])
            acc = term if acc is None else acc + term
        y = _dot(acc.astype(BF16), wout_ref[...])
        t = alpha * h_ref[rows, :] + gate_ref[0] * y
        mu = jnp.mean(t, axis=-1, keepdims=True)
        tc = t - mu
        var = jnp.mean(tc * tc, axis=-1, keepdims=True)
        o_ref[rows, :] = tc * lax.rsqrt(var + EPS) * lng_ref[...] + lnb_ref[...]


def _merge(p2, ya, yb, yc, yd, h, gate, rows_per_mod, w_br, w_out, ln_g, ln_b, alpha):
    m = h.shape[0]
    tm = min(1024, m)
    assert m % tm == 0 and rows_per_mod % tm == 0
    tiles_per_mod = rows_per_mod // tm
    n_mg = N_BRANCH * D_MODEL // MG_BLK
    br_spec = pl.BlockSpec((tm, BR_W), lambda i: (i, 0))
    vec_spec = pl.BlockSpec((1, D_MODEL), lambda i: (0, 0))
    return pl.pallas_call(
        functools.partial(_merge_kernel, alpha=alpha),
        grid=(m // tm,),
        in_specs=[pl.BlockSpec((tm, MG_BLK), functools.partial(lambda n, i: (i, COL_MG // MG_BLK + n), n))
                  for n in range(n_mg)] + [
            br_spec, br_spec, br_spec, br_spec,
            pl.BlockSpec((tm, D_MODEL), lambda i: (i, 0)),
            pl.BlockSpec((1, 1, D_MODEL), lambda i: (i // tiles_per_mod, 0, 0)),
            pl.BlockSpec((N_BRANCH, BR_W, D_MODEL), lambda i: (0, 0, 0), pipeline_mode=pl.Buffered(1)),
            pl.BlockSpec((D_MODEL, D_MODEL), lambda i: (0, 0), pipeline_mode=pl.Buffered(1)),
            vec_spec, vec_spec,
        ],
        out_specs=pl.BlockSpec((tm, D_MODEL), lambda i: (i, 0)),
        out_shape=jax.ShapeDtypeStruct((m, D_MODEL), F32),
        compiler_params=_cparams(1),
        name="merge",
    )(*([p2] * n_mg), ya, yb, yc, yd, h, gate, w_br, w_out, ln_g.reshape(1, D_MODEL), ln_b.reshape(1, D_MODEL))


def _projection_weights(w_in):
    r0 = int(sum(IN_WIDTHS[:12]))
    r1 = r0 + 2 * C_GATE_RANK
    w_main = jnp.concatenate([w_in[:, :, :r0], w_in[:, :, r1:]], axis=-1).astype(BF16)
    w_r = jnp.pad(w_in[:, :, r0:r1], ((0, 0), (0, 0), (0, LANES - 2 * C_GATE_RANK))).astype(BF16)
    return w_main, w_r


def _gate_weights(c_w2_l, c_b2_l):
    n_pairs = C_HEADS // 2

    def gate_w(i):
        w = jnp.pad(c_w2_l[i], ((i * C_GATE_RANK, LANES - (i + 1) * C_GATE_RANK), (0, 0)))
        return w.reshape(LANES, n_pairs, PAIR_K).transpose(1, 0, 2)

    w2 = jnp.concatenate([gate_w(0), gate_w(1)], axis=-1).astype(BF16)
    b2 = jnp.concatenate([c_b2_l[0].reshape(n_pairs, 1, PAIR_K), c_b2_l[1].reshape(n_pairs, 1, PAIR_K)], axis=-1)
    return w2, b2


def _rope_tables(t):
    rows = t // GRID_W
    row = np.repeat(np.arange(rows), GRID_W).astype(np.float32)
    col = np.tile(np.arange(GRID_W), rows).astype(np.float32)
    inv = (ROPE_THETA ** (-np.arange(0, AXIS_DIM, 2, dtype=np.float32) / AXIS_DIM)).astype(np.float32)
    ang = np.concatenate([row[:, None] * inv, col[:, None] * inv], -1).astype(np.float64)
    cos, sin = np.repeat(np.cos(ang), 2, axis=-1), np.repeat(np.sin(ang), 2, axis=-1)
    sin[:, 0::2] *= -1.0
    return jnp.asarray(cos, F32), jnp.asarray(sin, F32)


def kernel(x, c, ctx, c_ctx, w_mod, b_mod, w_in, q_norm, k_norm, b_conv, c_gate_w2, c_gate_b, c_norm,
           d_conv_w, d_conv_b, d_norm_g, d_norm_b, w_br, w_out, ln_g, ln_b):
    b, t, d = x.shape
    t_ctx = ctx.shape[1]
    depth = w_mod.shape[0]
    assert d == D_MODEL and b < 16
    alpha = (2 * depth) ** 0.25

    cc = jnp.concatenate([c, c_ctx[None], jnp.zeros((16 - b - 1, D_MODEL), F32)], axis=0)
    mod = _modulation(cc, w_mod, b_mod)
    cos, sin = _rope_tables(t)
    consts = _gla_constants()
    w_main, w_r = _projection_weights(w_in)
    w_br_bf, w_out_bf = w_br.astype(BF16), w_out.astype(BF16)
    n_pairs = C_HEADS // 2

    h_lat = x.reshape(b * t, D_MODEL)
    h_ctx = ctx.reshape(b * t_ctx, D_MODEL)
    for l in range(depth):
        want_ctx = l < depth - 1
        w2, b2 = _gate_weights(c_gate_w2[l], c_gate_b[l])
        q_gain, k_gain = q_norm[l].reshape(1, HEAD_DIM), k_norm[l].reshape(1, HEAD_DIM)
        c_gain = c_norm[l].reshape(n_pairs, 1, PAIR_V)
        shift, scale, gate = [mod[l, :, n * D_MODEL:(n + 1) * D_MODEL].reshape(16, 1, D_MODEL) for n in range(3)]
        p_lat, r_lat = _in_projection(h_lat, shift, scale, t, w_main, w_r, l)
        ctx_tiles = None if want_ctx else tuple(c // CTX_KV_TN for c in CTX_KV_TILES)
        ctx_cols = FULL_COLS if want_ctx else CTX_KV_COLS
        p_ctx, r_ctx = _in_projection(h_ctx, shift[b:], scale[b:], b * t_ctx, w_main, w_r, l, ctx_tiles,
                                      PROJ_TN if want_ctx else CTX_KV_TN)
        p_lat3 = p_lat.reshape(b, t, N_MAIN)
        p_ctx3 = p_ctx.reshape(b, t_ctx, p_ctx.shape[1])
        r_lat3 = r_lat.reshape(b, t, LANES)
        r_ctx3 = r_ctx.reshape(b, t_ctx, LANES)
        w_br_l, w_out_l = w_br_bf[l], w_out_bf[l]

        ya_l = _attention(p_lat3, p_lat3, p_ctx3, q_gain, k_gain, cos, sin, ctx_cols)
        yb_l, yd_l = _local_branches(p_lat, t, b_conv[l], d_conv_w[l], d_conv_b[l], d_norm_g[l], d_norm_b[l])
        yc_l, yc_c = _gla(p_lat3, r_lat3, p_ctx3, r_ctx3, w2, b2, c_gain, consts, want_ctx, ctx_cols)
        h_lat_new = _merge(p_lat, ya_l.reshape(b * t, BR_W), yb_l, yc_l.reshape(b * t, BR_W), yd_l,
                           h_lat, gate, t, w_br_l, w_out_l, ln_g[l], ln_b[l], alpha)
        if want_ctx:
            ya_c = _attention(p_ctx3, None, p_ctx3, q_gain, k_gain, None, None)
            yb_c, yd_c = _local_branches(p_ctx, t_ctx, b_conv[l], d_conv_w[l], d_conv_b[l], d_norm_g[l],
                                         d_norm_b[l])
            h_ctx = _merge(p_ctx, ya_c.reshape(b * t_ctx, BR_W), yb_c, yc_c.reshape(b * t_ctx, BR_W), yd_c,
                           h_ctx, gate[b:], b * t_ctx, w_br_l, w_out_l, ln_g[l], ln_b[l], alpha)
        h_lat = h_lat_new
    return h_lat.reshape(b, t, D_MODEL)
```

```python
import functools

import numpy as np
import jax
import jax.numpy as jnp
from jax import lax
from jax.experimental import pallas as pl
from jax.experimental.pallas import tpu as pltpu

F32 = jnp.float32
BF16 = jnp.bfloat16

D_MODEL = 1024
GRID_W = 64
N_BRANCH = 4
BR_W = D_MODEL // 2
HEAD_DIM = 128
A_HEADS = BR_W // HEAD_DIM
A_KV_HEADS = A_HEADS // 2
ROPE_THETA = 10000.0
AXIS_DIM = HEAD_DIM // 2
B_CONV = 3
C_HEADS = 4
C_HEAD_K = BR_W // (2 * C_HEADS)
C_HEAD_V = BR_W // C_HEADS
C_KEY_W = C_HEADS * C_HEAD_K
C_GATE_RANK = 16
C_GATE_TAU = 16.0
C_CHUNK = 64
D_CONV = 31
EPS = 1e-6
IN_WIDTHS = (
    A_HEADS * HEAD_DIM, A_KV_HEADS * HEAD_DIM, A_KV_HEADS * HEAD_DIM, BR_W,
    BR_W, BR_W, BR_W, BR_W,
    C_KEY_W, C_KEY_W, C_HEADS * C_HEAD_V, BR_W, 2 * C_GATE_RANK,
    2 * BR_W, BR_W,
    N_BRANCH * D_MODEL,
)

COL_AQ = 0
COL_AK = COL_AQ + 512
COL_AV = COL_AK + 256
COL_AZ = COL_AV + 256
COL_BG = COL_AZ + 512
COL_BC = COL_BG + 512
COL_BX = COL_BC + 512
COL_BZ = COL_BX + 512
COL_CQ = COL_BZ + 512
COL_CK = COL_CQ + 256
COL_CV = COL_CK + 256
COL_CZ = COL_CV + 512
COL_DA = COL_CZ + 512
COL_DG = COL_DA + 512
COL_DZ = COL_DG + 512
COL_MG = COL_DZ + 512
MG_BLK = 512
MERGE_SPLIT = 1
N_MAIN = COL_MG + N_BRANCH * D_MODEL
FULL_COLS = dict(ak=COL_AK, av=COL_AV, cq=COL_CQ, ck=COL_CK, cv=COL_CV, cz=COL_CZ)
CTX_KV_TILES = (COL_AK, COL_CQ, COL_CV)
CTX_KV_COLS = dict(ak=0, av=256, cq=512, ck=768, cv=1024, cz=1024)
LANES = 128
SUBLANES = 8
PROJ_TN = 1536
CTX_KV_TN = 512
LN_GROUP = 512
VMEM_LIMIT = 56 * 1024 * 1024
LOG2E = 1.4426950408889634


def _cparams(n_axes):
    return pltpu.CompilerParams(dimension_semantics=("arbitrary",) * n_axes,
                                vmem_limit_bytes=VMEM_LIMIT)


def _sigmoid(x):
    return 0.5 * jnp.tanh(0.5 * x) + 0.5


def _silu(x):
    h = 0.5 * x
    return h + h * jnp.tanh(h)


def _dot(a, b):
    return jnp.dot(a, b, preferred_element_type=F32)


def _dot_nt(a, b):
    return lax.dot_general(a, b, (((1,), (1,)), ((), ())), preferred_element_type=F32)


def _dot_tn(a, b):
    return lax.dot_general(a, b, (((0,), (0,)), ((), ())), preferred_element_type=F32)


def _mod_kernel(c_ref, w_ref, b_ref, o_ref):
    s = _silu(c_ref[...])
    o_ref[0] = _dot(s.astype(BF16), w_ref[0].astype(BF16)) + b_ref[0]


def _modulation(cc, w_mod, b_mod):
    depth = w_mod.shape[0]
    n_rows = cc.shape[0]
    return pl.pallas_call(
        _mod_kernel,
        grid=(depth, 1),
        in_specs=[
            pl.BlockSpec((n_rows, D_MODEL), lambda l, j: (0, 0)),
            pl.BlockSpec((1, D_MODEL, 3 * D_MODEL), lambda l, j: (l, 0, j)),
            pl.BlockSpec((1, 1, 3 * D_MODEL), lambda l, j: (l, 0, j)),
        ],
        out_specs=pl.BlockSpec((1, n_rows, 3 * D_MODEL), lambda l, j: (l, 0, j)),
        out_shape=jax.ShapeDtypeStruct((depth, n_rows, 3 * D_MODEL), F32),
        compiler_params=_cparams(2),
        name="modulation",
    )(cc, w_mod, b_mod.reshape(depth, 1, 3 * D_MODEL))


def _inproj_kernel(x_ref, shift_ref, scale_ref, w_ref, wr_ref, o_ref, or_ref, u_ref, *, tm, sub):
    j = pl.program_id(1)

    @pl.when(j == 0)
    def _():
        one_plus = 1.0 + scale_ref[0]
        shift = shift_ref[0]
        grp = min(LN_GROUP, tm)
        for g0 in range(0, tm, grp):
            for r0 in range(g0, g0 + grp, sub):
                x = x_ref[r0:r0 + sub, :]
                mu = jnp.mean(x, axis=-1, keepdims=True)
                xc = x - mu
                var = jnp.mean(xc * xc, axis=-1, keepdims=True)
                u = xc * lax.rsqrt(var + EPS) * one_plus + shift
                u_ref[r0:r0 + sub, :] = u.astype(BF16)
            u_grp = u_ref[g0:g0 + grp, :]
            or_ref[g0:g0 + grp, :] = _dot(u_grp, wr_ref[...]).astype(BF16)
            o_ref[g0:g0 + grp, :] = _dot(u_grp, w_ref[...]).astype(BF16)

    @pl.when(j != 0)
    def _():
        o_ref[...] = _dot(u_ref[...], w_ref[...]).astype(BF16)


def _in_projection(h, shift, scale, rows_per_mod, w_main, w_r, layer, tiles=None, tn=PROJ_TN):
    m = h.shape[0]
    tm = min(2048, rows_per_mod)
    assert m % tm == 0 and rows_per_mod % tm == 0
    tiles_per_mod = rows_per_mod // tm
    assert w_main.shape[2] % tn == 0
    n_tiles = len(tiles) if tiles else w_main.shape[2] // tn
    n_cols = n_tiles * tn

    def w_tile(j):
        if not tiles:
            return j
        idx = tiles[0]
        for n in range(1, len(tiles)):
            idx = idx + (tiles[n] - tiles[n - 1]) * jnp.minimum(jnp.maximum(j - n + 1, 0), 1)
        return idx

    kern = functools.partial(_inproj_kernel, tm=tm, sub=min(128, tm))
    return pl.pallas_call(
        kern,
        grid=(m // tm, n_tiles),
        in_specs=[
            pl.BlockSpec((tm, D_MODEL), lambda i, j: (i, 0)),
            pl.BlockSpec((1, 1, D_MODEL), lambda i, j: (i // tiles_per_mod, 0, 0)),
            pl.BlockSpec((1, 1, D_MODEL), lambda i, j: (i // tiles_per_mod, 0, 0)),
            pl.BlockSpec((None, D_MODEL, tn), lambda i, j: (layer, 0, w_tile(j))),
            pl.BlockSpec((None, D_MODEL, LANES), lambda i, j: (layer, 0, 0)),
        ],
        out_specs=[
            pl.BlockSpec((tm, tn), lambda i, j: (i, j)),
            pl.BlockSpec((tm, LANES), lambda i, j: (i, 0)),
        ],
        out_shape=[
            jax.ShapeDtypeStruct((m, n_cols), BF16),
            jax.ShapeDtypeStruct((m, LANES), BF16),
        ],
        scratch_shapes=[pltpu.VMEM((tm, D_MODEL), BF16)],
        compiler_params=_cparams(2),
        name="in_projection",
    )(h, shift, scale, w_main, w_r)


ATTN_KEY_BLOCKS = (768, 512, 256)
QK_AHEAD = 2


def _rms(x):
    return x * lax.rsqrt(jnp.mean(x * x, axis=-1, keepdims=True) + EPS)


def _rope(x, cos, sin):
    lane = lax.broadcasted_iota(jnp.int32, x.shape, 1)
    swapped = jnp.where(lane % 2 == 0, pltpu.roll(x, HEAD_DIM - 1, 1), pltpu.roll(x, 1, 1))
    return x * cos + swapped * sin


def _attn_kernel(*refs, n_lat, n_ctx, tq, kblk, kstep):
    if n_lat:
        (q_ref, z_ref, kl_ref, vl_ref, kc_ref, vc_ref, qg_ref, kg_ref, cos_ref, sin_ref,
         o_ref, kn_ref, vn_ref) = refs
    else:
        q_ref, z_ref, kc_ref, vc_ref, qg_ref, kg_ref, o_ref, kn_ref, vn_ref = refs
    qi = pl.program_id(2)
    kg = kg_ref[...]

    @pl.when(qi == 0)
    def _prep():
        if n_lat:
            def body(t, carry):
                r0 = pl.multiple_of(t * kblk, kblk)
                kn = _rms(kl_ref[0, pl.ds(r0, kblk), :].astype(F32)) * kg
                kn = _rope(kn, cos_ref[pl.ds(r0, kblk), :], sin_ref[pl.ds(r0, kblk), :])
                kn_ref[pl.ds(r0, kblk), :] = kn.astype(BF16)
                vn_ref[pl.ds(r0, kblk), 0:HEAD_DIM] = vl_ref[0, pl.ds(r0, kblk), :]
                return carry

            lax.fori_loop(0, n_lat // kblk, body, 0)
        kn_ref[n_lat:n_lat + n_ctx, :] = (_rms(kc_ref[0].astype(F32)) * kg).astype(BF16)
        vn_ref[n_lat:n_lat + n_ctx, 0:HEAD_DIM] = vc_ref[0]
        vn_ref[:, HEAD_DIM:2 * HEAD_DIM] = jnp.ones((n_lat + n_ctx, HEAD_DIM), BF16)

    qg = qg_ref[...]
    if n_lat:
        q0 = pl.multiple_of(qi * tq, tq)
        cos = cos_ref[pl.ds(q0, tq), :]
        sin = sin_ref[pl.ds(q0, tq), :]
    qns = []
    for g in range(2):
        qn = _rms(q_ref[0, :, g * HEAD_DIM:(g + 1) * HEAD_DIM].astype(F32)) * qg
        if n_lat:
            qn = _rope(qn, cos, sin)
        qns.append((qn * (HEAD_DIM ** -0.5 * LOG2E)).astype(BF16))

    items = [(g, k0) for k0 in range(0, n_lat + n_ctx, kstep) for g in range(2)]

    def scores(item):
        g, k0 = item
        return _dot_nt(qns[g], kn_ref[k0:k0 + kstep, :])

    m = [None, None]
    acc = [None, None]
    ahead = [scores(item) for item in items[:QK_AHEAD]]
    for n, (g, k0) in enumerate(items):
        s = ahead.pop(0)
        if n + QK_AHEAD < len(items):
            ahead.append(scores(items[n + QK_AHEAD]))
        m_blk = jnp.max(s, axis=-1, keepdims=True)
        m_new = m_blk if m[g] is None else jnp.maximum(m[g], m_blk)
        p = jnp.exp2((s - m_new).astype(BF16))
        pv = _dot(p, vn_ref[k0:k0 + kstep, :])
        acc[g] = pv if acc[g] is None else acc[g] * jnp.exp2(m[g] - m_new) + pv
        m[g] = m_new
    for g in range(2):
        lanes = slice(g * HEAD_DIM, (g + 1) * HEAD_DIM)
        o = acc[g][:, 0:HEAD_DIM] / acc[g][:, HEAD_DIM:2 * HEAD_DIM]
        o_ref[0, :, lanes] = (o * _silu(z_ref[0, :, lanes].astype(F32))).astype(BF16)


def _attention(p_q, p_lat, p_ctx, q_gain, k_gain, cos, sin, ctx_cols=None):
    b, t_q, _ = p_q.shape
    n_ctx = p_ctx.shape[1]
    n_lat = 0 if p_lat is None else p_lat.shape[1]
    tq = min(1024, t_q)
    assert t_q % tq == 0
    kblk = min(1024, n_lat) if n_lat else 0
    n_keys = n_lat + n_ctx
    kstep = next((c for c in ATTN_KEY_BLOCKS if n_keys % c == 0), n_keys)
    hw = 2 * HEAD_DIM
    q_spec = pl.BlockSpec((1, tq, hw), lambda i, kv, qi: (i, qi, COL_AQ // hw + kv))
    z_spec = pl.BlockSpec((1, tq, hw), lambda i, kv, qi: (i, qi, COL_AZ // hw + kv))

    def kv_spec(n, col):
        return pl.BlockSpec((1, n, HEAD_DIM), lambda i, kv, qi: (i, 0, col // HEAD_DIM + kv))

    gain_spec = pl.BlockSpec((1, HEAD_DIM), lambda i, kv, qi: (0, 0))
    args = [p_q, p_q]
    specs = [q_spec, z_spec]
    if n_lat:
        args += [p_lat, p_lat]
        specs += [kv_spec(n_lat, COL_AK), kv_spec(n_lat, COL_AV)]
    ctx_cols = ctx_cols or FULL_COLS
    args += [p_ctx, p_ctx, q_gain, k_gain]
    specs += [kv_spec(n_ctx, ctx_cols["ak"]), kv_spec(n_ctx, ctx_cols["av"]), gain_spec, gain_spec]
    if n_lat:
        tab_spec = pl.BlockSpec((n_lat, HEAD_DIM), lambda i, kv, qi: (0, 0))
        args += [cos, sin]
        specs += [tab_spec, tab_spec]
    kern = functools.partial(_attn_kernel, n_lat=n_lat, n_ctx=n_ctx, tq=tq, kblk=kblk, kstep=kstep)
    return pl.pallas_call(
        kern,
        grid=(b, A_KV_HEADS, t_q // tq),
        in_specs=specs,
        out_specs=pl.BlockSpec((1, tq, hw), lambda i, kv, qi: (i, qi, kv)),
        out_shape=jax.ShapeDtypeStruct((b, t_q, BR_W), BF16),
        scratch_shapes=[pltpu.VMEM((n_lat + n_ctx, HEAD_DIM), BF16),
                        pltpu.VMEM((n_lat + n_ctx, 2 * HEAD_DIM), BF16)],
        compiler_params=_cparams(3),
        name="attention_lat" if n_lat else "attention_ctx",
    )(*args)


D_HALO = 16
B_HALO = 16
CONV_RB = 16
CONV_ACCS = 2
ELEM_RB = 32
COPY_RB = 56


def _local_kernel(bg_ref, bc_ref, bcp_ref, bcn_ref, bx_ref, bxp_ref, bxn_ref, bz_ref,
                  da_ref, dap_ref, dan_ref, dg_ref, dgp_ref, dgn_ref, dz_ref,
                  bw_ref, dw_ref, db_ref, dgain_ref, dbeta_ref,
                  yb_ref, yd_ref, tbuf, gbuf, hbuf, *, tt, tiles_per_seq):
    i = pl.program_id(0)
    pos = i % tiles_per_seq
    keep_prev = (pos != 0).astype(F32)
    keep_next = (pos != tiles_per_seq - 1).astype(F32)

    def ld(ref, r0=0, n=None):
        return ref[r0:r0 + (n or ref.shape[0]), :].astype(F32)

    row_blocks = range(0, tt, ELEM_RB)

    tbuf[0:B_HALO, :] = ld(bcp_ref) * ld(bxp_ref) * keep_prev
    for r0 in row_blocks:
        tbuf[B_HALO + r0:B_HALO + r0 + ELEM_RB, :] = ld(bc_ref, r0, ELEM_RB) * ld(bx_ref, r0, ELEM_RB)
    tbuf[B_HALO + tt:2 * B_HALO + tt, :] = ld(bcn_ref) * ld(bxn_ref) * keep_next
    for r0 in row_blocks:
        lo = B_HALO + r0
        conv = (bw_ref[0:1, :] * tbuf[lo - 1:lo - 1 + ELEM_RB, :] + bw_ref[1:2, :] * tbuf[lo:lo + ELEM_RB, :]
                + bw_ref[2:3, :] * tbuf[lo + 1:lo + 1 + ELEM_RB, :])
        yb_ref[r0:r0 + ELEM_RB, :] = (ld(bg_ref, r0, ELEM_RB) * conv
                                      * _silu(ld(bz_ref, r0, ELEM_RB))).astype(BF16)

    gbuf[0, 0:D_HALO, :] = ld(dap_ref) * _sigmoid(ld(dgp_ref)) * keep_prev
    for r0 in row_blocks:
        gbuf[0, D_HALO + r0:D_HALO + r0 + ELEM_RB, :] = (ld(da_ref, r0, ELEM_RB)
                                                          * _sigmoid(ld(dg_ref, r0, ELEM_RB)))
    gbuf[0, D_HALO + tt:2 * D_HALO + tt, :] = ld(dan_ref) * _sigmoid(ld(dgn_ref)) * keep_next
    span = tt + 2 * D_HALO - SUBLANES
    for r in range(1, SUBLANES):
        for x0 in range(0, span, COPY_RB):
            n = min(COPY_RB, span - x0)
            gbuf[r, x0:x0 + n, :] = gbuf[0, r + x0:r + x0 + n, :]
    base = D_HALO - D_CONV // 2
    for cb in range(BR_W // LANES):
        lanes = slice(cb * LANES, (cb + 1) * LANES)
        for rb in range(tt // CONV_RB):
            accs = [None] * CONV_ACCS
            for k in range(D_CONV):
                off = base + k
                r0 = rb * CONV_RB + off - off % SUBLANES
                rows = gbuf[off % SUBLANES, r0:r0 + CONV_RB, lanes]
                term = rows.reshape(CONV_RB // SUBLANES, SUBLANES, LANES) * dw_ref[k, :, lanes][None]
                a = k % CONV_ACCS
                accs[a] = term if accs[a] is None else accs[a] + term
            while len(accs) > 1:
                accs = [accs[n] + accs[n + 1] for n in range(0, len(accs), 2)]
            hbuf[rb * CONV_RB:(rb + 1) * CONV_RB, lanes] = accs[0].reshape(CONV_RB, LANES)
    for r0 in row_blocks:
        hh = hbuf[r0:r0 + ELEM_RB, :] + db_ref[...]
        mu = jnp.mean(hh, axis=-1, keepdims=True)
        hc = hh - mu
        var = jnp.mean(hc * hc, axis=-1, keepdims=True)
        hn = hc * lax.rsqrt(var + EPS) * dgain_ref[...] + dbeta_ref[...]
        yd_ref[r0:r0 + ELEM_RB, :] = (_silu(hn) * _silu(ld(dz_ref, r0, ELEM_RB))).astype(BF16)


def _local_branches(p2, seq_len, b_w, d_w, d_b, d_g, d_beta):
    m = p2.shape[0]
    tt = min(512, seq_len)
    assert seq_len % tt == 0 and m % seq_len == 0
    tiles_per_seq = seq_len // tt
    n_tiles = m // tt

    def cur(col):
        return pl.BlockSpec((tt, BR_W), lambda i: (i, col // BR_W))

    def prev(col, halo):
        per = tt // halo
        return pl.BlockSpec((halo, BR_W), lambda i: (jnp.maximum(i * per - 1, 0), col // BR_W))

    def nxt(col, halo):
        per = tt // halo
        last = m // halo - 1
        return pl.BlockSpec((halo, BR_W), lambda i: (jnp.minimum((i + 1) * per, last), col // BR_W))

    def small(rows):
        return pl.BlockSpec((rows, BR_W), lambda i: (0, 0))

    specs = [cur(COL_BG),
             cur(COL_BC), prev(COL_BC, B_HALO), nxt(COL_BC, B_HALO),
             cur(COL_BX), prev(COL_BX, B_HALO), nxt(COL_BX, B_HALO),
             cur(COL_BZ),
             cur(COL_DA), prev(COL_DA, D_HALO), nxt(COL_DA, D_HALO),
             cur(COL_DG), prev(COL_DG, D_HALO), nxt(COL_DG, D_HALO),
             cur(COL_DZ),
             small(B_CONV), pl.BlockSpec((D_CONV, SUBLANES, BR_W), lambda i: (0, 0, 0)),
             small(1), small(1), small(1)]
    kern = functools.partial(_local_kernel, tt=tt, tiles_per_seq=tiles_per_seq)
    out_spec = pl.BlockSpec((tt, BR_W), lambda i: (i, 0))
    return pl.pallas_call(
        kern,
        grid=(n_tiles,),
        in_specs=specs,
        out_specs=[out_spec, out_spec],
        out_shape=[jax.ShapeDtypeStruct((m, BR_W), BF16), jax.ShapeDtypeStruct((m, BR_W), BF16)],
        scratch_shapes=[pltpu.VMEM((tt + 2 * B_HALO, BR_W), F32),
                        pltpu.VMEM((SUBLANES, tt + 2 * D_HALO, BR_W), F32),
                        pltpu.VMEM((tt, BR_W), F32)],
        compiler_params=_cparams(1),
        name="local_branches",
    )(*([p2] * 15), b_w, jnp.broadcast_to(d_w[:, None, :], (D_CONV, SUBLANES, BR_W)),
      d_b.reshape(1, BR_W), d_g.reshape(1, BR_W), d_beta.reshape(1, BR_W))


CH = C_CHUNK
PAIR_K = 2 * C_HEAD_K
PAIR_V = 2 * C_HEAD_V
LEVELS = (8, 4, 2)
BLK = SUBLANES
N_BLK = CH // BLK
GLA_UNROLL = 16
ROW_EQ = 0
ROW_EK = CH
ROW_LAST = 2 * CH
ROW_FK = 2 * CH + 8
ROW_LVL = ROW_FK + CH
ROW_FQ = ROW_LVL + len(LEVELS) * CH
N_FQ = BLK * (N_BLK * (N_BLK - 1) // 2)
N_EROWS = ROW_FQ + N_FQ


def _slabs(reverse):
    if reverse:
        return [(jb, 0, BLK * jb) for jb in range(1, N_BLK)]
    return [(jb, BLK * (jb + 1), CH - BLK * (jb + 1)) for jb in range(N_BLK - 1)]


def _gla_constants():
    idx = np.arange(CH)
    cols = np.arange(2 * CH) % CH
    emats, masks = [], []
    for reverse in (False, True):
        tri = (idx[None, :] >= idx[:, None]) if reverse else (idx[None, :] <= idx[:, None])
        tri = tri.astype(np.float32)
        edge = tri[0] if reverse else tri[CH - 1]
        blk_edge = (idx // BLK) * BLK + (0 if reverse else BLK - 1)
        blocks = [tri, edge[None, :] - tri, np.tile(edge[None, :], (8, 1)), tri[blk_edge] - tri]
        lvl_masks = []
        for grp in LEVELS:
            half = grp // 2
            in_q = (idx % grp < half) if reverse else (idx % grp >= half)
            ref = (idx // grp) * grp + (half if reverse else half - 1)
            d = tri - tri[ref]
            blocks.append(np.where(in_q[:, None], d, -d))
            in_q_col = (cols % grp < half) if reverse else (cols % grp >= half)
            lvl_masks.append((idx[:, None] // grp == cols[None, :] // grp) & in_q[:, None] & ~in_q_col[None, :])
        lvl_masks.append(idx[:, None] == cols[None, :])
        for jb, q0, nq in _slabs(reverse):
            edge_row = BLK * jb + (0 if reverse else BLK - 1)
            blocks.append(tri[q0:q0 + nq] - tri[edge_row][None, :])
        emat = np.concatenate(blocks, axis=0)
        assert emat.shape == (N_EROWS, CH) and emat.min() >= -1 and emat.max() <= 1
        emats.append(np.concatenate([emat] * 3, axis=1))
        masks.append(np.stack(lvl_masks).astype(np.float32))
    return (jnp.asarray(emats[0], BF16), jnp.asarray(emats[1], BF16), jnp.asarray(np.stack(masks), F32))


def _split3(g):
    g1 = g.astype(BF16)
    r1 = g - g1.astype(F32)
    g2 = r1.astype(BF16)
    g3 = (r1 - g2.astype(F32)).astype(BF16)
    return g1, g2, g3


def _log2_sigmoid(x):
    e = jnp.exp2(jnp.abs(x) * -LOG2E)
    return jnp.minimum(x, 0.0) * LOG2E - jnp.log2(1.0 + e)


def _gla_kernel(*refs, t_lat, t_ctx, want_ctx):
    (ql_ref, kl_ref, vl_ref, zl_ref, rl_ref, qc_ref, kc_ref, vc_ref, zc_ref, rc_ref,
     w2_ref, b2_ref, gain_ref, ematf_ref, ematb_ref, masks_ref) = refs[:16]
    if want_ctx:
        yl_ref, yc_ref = refs[16:18]
        scratch = refs[18:]
    else:
        yl_ref = refs[16]
        scratch = refs[17:]
        yc_ref = zc_ref = None
    stf_ref, stb_ref, of_ref, ob_ref, g1_ref, g2_ref, g3_ref = scratch

    def gates(r_ref, n_rows):
        blk = min(512, n_rows)

        def body(t, carry):
            rows = pl.ds(pl.multiple_of(t * blk, blk), blk)
            x = _dot(r_ref[0, rows, :], w2_ref[...]) + b2_ref[...]
            g1, g2, g3 = _split3(_log2_sigmoid(x) * (1.0 / C_GATE_TAU))
            g1_ref[rows, :] = g1
            g2_ref[rows, :] = g2
            g3_ref[rows, :] = g3
            return carry

        lax.fori_loop(0, n_rows // blk, body, 0)

    def run(q_ref, k_ref, v_ref, n_chunks, want_out):
        unroll = min(GLA_UNROLL, n_chunks)
        assert n_chunks % unroll == 0
        lane_k = lax.broadcasted_iota(jnp.int32, (CH, PAIR_K), 1)
        head0 = lane_k < C_HEAD_K
        zero_v = jnp.zeros((CH, C_HEAD_V), BF16)
        key_blk = (lax.broadcasted_iota(jnp.int32, (BLK, 2 * CH), 1) % CH) // BLK

        def body(i, carry):
            work = []
            for reverse in (False, True):
                lanes = slice(PAIR_K, 2 * PAIR_K) if reverse else slice(0, PAIR_K)
                rows = []
                for u in range(unroll):
                    c = i * unroll + u
                    c = (n_chunks - 1 - c) if reverse else c
                    rows.append(pl.ds(pl.multiple_of(c * CH, CH), CH))
                rhs = jnp.concatenate(
                    [jnp.concatenate([g_ref[r, lanes] for r in rows], axis=1)
                     for g_ref in (g1_ref, g2_ref, g3_ref)], axis=0)
                emat = ematb_ref[...] if reverse else ematf_ref[...]
                e_all = jnp.exp2(_dot(emat, rhs))
                for u in range(unroll):
                    work.append(dict(reverse=reverse, rows=rows[u], u=u,
                                     e=e_all[:, u * PAIR_K:(u + 1) * PAIR_K]))
            work.sort(key=lambda w: w["u"])

            for w in work:
                st_ref = stb_ref if w["reverse"] else stf_ref
                e = w["e"]
                q = q_ref[0, w["rows"], :].astype(F32) * C_HEAD_K ** -0.5
                k = k_ref[0, w["rows"], :].astype(F32)
                v = v_ref[0, w["rows"], :]
                k0 = jnp.where(head0, k, 0.0)
                k1 = jnp.where(head0, 0.0, k)
                st = st_ref[...]
                e_k = e[ROW_EK:ROW_EK + CH]
                k_dec = jnp.concatenate([k0 * e_k, k1 * e_k], axis=0).astype(BF16)
                v_rows = jnp.concatenate([v[:, :C_HEAD_V], v[:, C_HEAD_V:]], axis=0)
                st_ref[...] = e[ROW_LAST:ROW_LAST + 1] * st + _dot_tn(v_rows, k_dec)
                if want_out:
                    qe = q * e[ROW_EQ:ROW_EQ + CH]
                    q_rows = jnp.concatenate([jnp.where(head0, qe, 0.0), jnp.where(head0, 0.0, qe)], axis=0)
                    o_st = _dot(q_rows.astype(BF16), st.T.astype(BF16))
                    w.update(q=q, k0=k0, k1=k1, v=v, o=jnp.concatenate([o_st[:CH], o_st[CH:]], axis=1))
            if not want_out:
                return carry

            for w in work:
                masks = masks_ref.at[1 if w["reverse"] else 0]
                q, k0, k1, e = w["q"], w["k0"], w["k1"], w["e"]
                fk = e[ROW_FK:ROW_FK + CH]
                kk_t = jnp.concatenate([k0 * fk, k1 * fk], axis=0).T.astype(BF16)
                slabs = _slabs(w["reverse"])
                lhs, off = [], ROW_FQ
                for jb, q0, nq in slabs:
                    lhs.append(q[q0:q0 + nq] * e[off:off + nq])
                    off += nq
                r = _dot(jnp.concatenate(lhs, axis=0).astype(BF16), kk_t)
                rows, off = [jnp.zeros((BLK, 2 * CH), F32)] * N_BLK, 0
                for jb, q0, nq in slabs:
                    for ib in range(q0 // BLK, (q0 + nq) // BLK):
                        piece = r[off + ib * BLK - q0:off + (ib + 1) * BLK - q0]
                        rows[ib] = jnp.where(key_blk == jb, piece, rows[ib])
                    off += nq
                scores = jnp.concatenate(rows, axis=0)
                k_heads = jnp.concatenate([k0, k1], axis=0)
                scores = scores + _dot(q.astype(BF16), k_heads.T.astype(BF16)) * masks[len(LEVELS)]
                for lvl in range(len(LEVELS)):
                    e_l = e[ROW_LVL + lvl * CH:ROW_LVL + (lvl + 1) * CH]
                    kk_t = jnp.concatenate([k0 * e_l, k1 * e_l], axis=0).T.astype(BF16)
                    scores = scores + _dot((q * e_l).astype(BF16), kk_t) * masks[lvl]
                w["scores"] = scores.astype(BF16)

            for w in work:
                v = w["v"]
                v_bd = jnp.concatenate([jnp.concatenate([v[:, :C_HEAD_V], zero_v], axis=1),
                                        jnp.concatenate([zero_v, v[:, C_HEAD_V:]], axis=1)], axis=0)
                (ob_ref if w["reverse"] else of_ref)[w["rows"], :] = w["o"] + _dot(w["scores"], v_bd)
            return carry

        lax.fori_loop(0, n_chunks // unroll, body, 0)

    def finish(z_ref, y_ref, n_rows):
        blk = min(256, n_rows)

        def body(t, carry):
            rows = pl.ds(pl.multiple_of(t * blk, blk), blk)
            o = of_ref[rows, :] + ob_ref[rows, :]
            z = z_ref[0, rows, :].astype(F32)
            gain = gain_ref[...]
            for h in range(2):
                lanes = slice(h * C_HEAD_V, (h + 1) * C_HEAD_V)
                y = _rms(o[:, lanes]) * gain[:, lanes]
                y_ref[0, rows, lanes] = (y * _silu(z[:, lanes])).astype(BF16)
            return carry

        lax.fori_loop(0, n_rows // blk, body, 0)

    stf_ref[...] = jnp.zeros_like(stf_ref)
    stb_ref[...] = jnp.zeros_like(stb_ref)
    gates(rc_ref, t_ctx)
    run(qc_ref, kc_ref, vc_ref, t_ctx // CH, want_ctx)
    if want_ctx:
        finish(zc_ref, yc_ref, t_ctx)
    gates(rl_ref, t_lat)
    run(ql_ref, kl_ref, vl_ref, t_lat // CH, True)
    finish(zl_ref, yl_ref, t_lat)


def _gla(p_lat, r_lat, p_ctx, r_ctx, w2, b2, gain, consts, want_ctx, ctx_cols=None):
    b, t_lat, _ = p_lat.shape
    t_ctx = p_ctx.shape[1]
    n_pairs = C_HEADS // 2

    def side(t, with_z, cols):
        z_rows = t if with_z else 16
        return [pl.BlockSpec((1, t, PAIR_K), lambda i, hp: (i, 0, cols["cq"] // PAIR_K + hp)),
                pl.BlockSpec((1, t, PAIR_K), lambda i, hp: (i, 0, cols["ck"] // PAIR_K + hp)),
                pl.BlockSpec((1, t, PAIR_V), lambda i, hp: (i, 0, cols["cv"] // PAIR_V + hp)),
                pl.BlockSpec((1, z_rows, PAIR_V), lambda i, hp: (i, 0, cols["cz"] // PAIR_V + hp)),
                pl.BlockSpec((1, t, LANES), lambda i, hp: (i, 0, 0))]

    def whole(shape):
        nd = len(shape)
        return pl.BlockSpec(shape, lambda i, hp: (0,) * nd)

    emat_f, emat_b, masks = consts
    specs = side(t_lat, True, FULL_COLS) + side(t_ctx, want_ctx, ctx_cols or FULL_COLS) + [
        pl.BlockSpec((1, LANES, 2 * PAIR_K), lambda i, hp: (hp, 0, 0)),
        pl.BlockSpec((1, 1, 2 * PAIR_K), lambda i, hp: (hp, 0, 0)),
        pl.BlockSpec((1, 1, PAIR_V), lambda i, hp: (hp, 0, 0)),
        whole(emat_f.shape), whole(emat_b.shape), whole(masks.shape)]
    out_specs = [pl.BlockSpec((1, t_lat, PAIR_V), lambda i, hp: (i, 0, hp))]
    out_shape = [jax.ShapeDtypeStruct((b, t_lat, BR_W), BF16)]
    if want_ctx:
        out_specs.append(pl.BlockSpec((1, t_ctx, PAIR_V), lambda i, hp: (i, 0, hp)))
        out_shape.append(jax.ShapeDtypeStruct((b, t_ctx, BR_W), BF16))

    def kern(*refs):
        refs = list(refs)
        for n in range(10, 13):
            refs[n] = refs[n].at[0]
        _gla_kernel(*refs, t_lat=t_lat, t_ctx=t_ctx, want_ctx=want_ctx)

    res = pl.pallas_call(
        kern,
        grid=(b, n_pairs),
        in_specs=specs,
        out_specs=out_specs,
        out_shape=out_shape,
        scratch_shapes=[pltpu.VMEM((C_HEAD_V, PAIR_K), F32), pltpu.VMEM((C_HEAD_V, PAIR_K), F32),
                        pltpu.VMEM((t_lat, PAIR_V), F32), pltpu.VMEM((t_lat, PAIR_V), F32),
                        pltpu.VMEM((t_lat, 2 * PAIR_K), BF16), pltpu.VMEM((t_lat, 2 * PAIR_K), BF16),
                        pltpu.VMEM((t_lat, 2 * PAIR_K), BF16)],
        compiler_params=_cparams(2),
        name="gla",
    )(p_lat, p_lat, p_lat, p_lat, r_lat, p_ctx, p_ctx, p_ctx, p_ctx, r_ctx,
      w2, b2, gain, emat_f, emat_b, masks)
    return (res[0], res[1]) if want_ctx else (res[0], None)


def _merge_kernel(*refs, alpha):
    n_mg = N_BRANCH * D_MODEL // MG_BLK
    mg_refs = refs[:n_mg]
    ya_ref, yb_ref, yc_ref, yd_ref, h_ref, gate_ref, wbr_ref, wout_ref, lng_ref, lnb_ref, o_ref = refs[n_mg:]
    per = D_MODEL // MG_BLK
    tm = o_ref.shape[0]
    rows_per = tm // MERGE_SPLIT
    for r0 in range(0, tm, rows_per):
        rows = slice(r0, r0 + rows_per)
        acc = None
        for n, y_ref in enumerate((ya_ref, yb_ref, yc_ref, yd_ref)):
            mg = jnp.concatenate([r[rows, :] for r in mg_refs[n * per:(n + 1) * per]], axis=1)
            gate = _sigmoid(mg.astype(F32))
            term = gate * _dot(y_ref[rows, :], wbr_ref[n])
            acc = term if acc is None else acc + term
        y = _dot(acc.astype(BF16), wout_ref[...])
        t = alpha * h_ref[rows, :] + gate_ref[0] * y
        mu = jnp.mean(t, axis=-1, keepdims=True)
        tc = t - mu
        var = jnp.mean(tc * tc, axis=-1, keepdims=True)
        o_ref[rows, :] = tc * lax.rsqrt(var + EPS) * lng_ref[...] + lnb_ref[...]


def _merge(p2, ya, yb, yc, yd, h, gate, rows_per_mod, w_br, w_out, ln_g, ln_b, alpha):
    m = h.shape[0]
    tm = min(1024, m)
    assert m % tm == 0 and rows_per_mod % tm == 0
    tiles_per_mod = rows_per_mod // tm
    n_mg = N_BRANCH * D_MODEL // MG_BLK
    br_spec = pl.BlockSpec((tm, BR_W), lambda i: (i, 0))
    vec_spec = pl.BlockSpec((1, D_MODEL), lambda i: (0, 0))
    return pl.pallas_call(
        functools.partial(_merge_kernel, alpha=alpha),
        grid=(m // tm,),
        in_specs=[pl.BlockSpec((tm, MG_BLK), functools.partial(lambda n, i: (i, COL_MG // MG_BLK + n), n))
                  for n in range(n_mg)] + [
            br_spec, br_spec, br_spec, br_spec,
            pl.BlockSpec((tm, D_MODEL), lambda i: (i, 0)),
            pl.BlockSpec((1, 1, D_MODEL), lambda i: (i // tiles_per_mod, 0, 0)),
            pl.BlockSpec((N_BRANCH, BR_W, D_MODEL), lambda i: (0, 0, 0), pipeline_mode=pl.Buffered(1)),
            pl.BlockSpec((D_MODEL, D_MODEL), lambda i: (0, 0), pipeline_mode=pl.Buffered(1)),
            vec_spec, vec_spec,
        ],
        out_specs=pl.BlockSpec((tm, D_MODEL), lambda i: (i, 0)),
        out_shape=jax.ShapeDtypeStruct((m, D_MODEL), F32),
        compiler_params=_cparams(1),
        name="merge",
    )(*([p2] * n_mg), ya, yb, yc, yd, h, gate, w_br, w_out, ln_g.reshape(1, D_MODEL), ln_b.reshape(1, D_MODEL))


def _projection_weights(w_in):
    r0 = int(sum(IN_WIDTHS[:12]))
    r1 = r0 + 2 * C_GATE_RANK
    w_main = jnp.concatenate([w_in[:, :, :r0], w_in[:, :, r1:]], axis=-1).astype(BF16)
    w_r = jnp.pad(w_in[:, :, r0:r1], ((0, 0), (0, 0), (0, LANES - 2 * C_GATE_RANK))).astype(BF16)
    return w_main, w_r


def _gate_weights(c_w2_l, c_b2_l):
    n_pairs = C_HEADS // 2

    def gate_w(i):
        w = jnp.pad(c_w2_l[i], ((i * C_GATE_RANK, LANES - (i + 1) * C_GATE_RANK), (0, 0)))
        return w.reshape(LANES, n_pairs, PAIR_K).transpose(1, 0, 2)

    w2 = jnp.concatenate([gate_w(0), gate_w(1)], axis=-1).astype(BF16)
    b2 = jnp.concatenate([c_b2_l[0].reshape(n_pairs, 1, PAIR_K), c_b2_l[1].reshape(n_pairs, 1, PAIR_K)], axis=-1)
    return w2, b2


def _rope_tables(t):
    rows = t // GRID_W
    row = np.repeat(np.arange(rows), GRID_W).astype(np.float32)
    col = np.tile(np.arange(GRID_W), rows).astype(np.float32)
    inv = (ROPE_THETA ** (-np.arange(0, AXIS_DIM, 2, dtype=np.float32) / AXIS_DIM)).astype(np.float32)
    ang = np.concatenate([row[:, None] * inv, col[:, None] * inv], -1).astype(np.float64)
    cos, sin = np.repeat(np.cos(ang), 2, axis=-1), np.repeat(np.sin(ang), 2, axis=-1)
    sin[:, 0::2] *= -1.0
    return jnp.asarray(cos, F32), jnp.asarray(sin, F32)


def kernel(x, c, ctx, c_ctx, w_mod, b_mod, w_in, q_norm, k_norm, b_conv, c_gate_w2, c_gate_b, c_norm,
           d_conv_w, d_conv_b, d_norm_g, d_norm_b, w_br, w_out, ln_g, ln_b):
    b, t, d = x.shape
    t_ctx = ctx.shape[1]
    depth = w_mod.shape[0]
    assert d == D_MODEL and b < 16
    alpha = (2 * depth) ** 0.25

    cc = jnp.concatenate([c, c_ctx[None], jnp.zeros((16 - b - 1, D_MODEL), F32)], axis=0)
    mod = _modulation(cc, w_mod, b_mod)
    cos, sin = _rope_tables(t)
    consts = _gla_constants()
    w_main, w_r = _projection_weights(w_in)
    w_br_bf, w_out_bf = w_br.astype(BF16), w_out.astype(BF16)
    n_pairs = C_HEADS // 2

    h_lat = x.reshape(b * t, D_MODEL)
    h_ctx = ctx.reshape(b * t_ctx, D_MODEL)
    for l in range(depth):
        want_ctx = l < depth - 1
        w2, b2 = _gate_weights(c_gate_w2[l], c_gate_b[l])
        q_gain, k_gain = q_norm[l].reshape(1, HEAD_DIM), k_norm[l].reshape(1, HEAD_DIM)
        c_gain = c_norm[l].reshape(n_pairs, 1, PAIR_V)
        shift, scale, gate = [mod[l, :, n * D_MODEL:(n + 1) * D_MODEL].reshape(16, 1, D_MODEL) for n in range(3)]
        p_lat, r_lat = _in_projection(h_lat, shift, scale, t, w_main, w_r, l)
        ctx_tiles = None if want_ctx else tuple(c // CTX_KV_TN for c in CTX_KV_TILES)
        ctx_cols = FULL_COLS if want_ctx else CTX_KV_COLS
        p_ctx, r_ctx = _in_projection(h_ctx, shift[b:], scale[b:], b * t_ctx, w_main, w_r, l, ctx_tiles,
                                      PROJ_TN if want_ctx else CTX_KV_TN)
        p_lat3 = p_lat.reshape(b, t, N_MAIN)
        p_ctx3 = p_ctx.reshape(b, t_ctx, p_ctx.shape[1])
        r_lat3 = r_lat.reshape(b, t, LANES)
        r_ctx3 = r_ctx.reshape(b, t_ctx, LANES)
        w_br_l, w_out_l = w_br_bf[l], w_out_bf[l]

        ya_l = _attention(p_lat3, p_lat3, p_ctx3, q_gain, k_gain, cos, sin, ctx_cols)
        yb_l, yd_l = _local_branches(p_lat, t, b_conv[l], d_conv_w[l], d_conv_b[l], d_norm_g[l], d_norm_b[l])
        yc_l, yc_c = _gla(p_lat3, r_lat3, p_ctx3, r_ctx3, w2, b2, c_gain, consts, want_ctx, ctx_cols)
        h_lat_new = _merge(p_lat, ya_l.reshape(b * t, BR_W), yb_l, yc_l.reshape(b * t, BR_W), yd_l,
                           h_lat, gate, t, w_br_l, w_out_l, ln_g[l], ln_b[l], alpha)
        if want_ctx:
            ya_c = _attention(p_ctx3, None, p_ctx3, q_gain, k_gain, None, None)
            yb_c, yd_c = _local_branches(p_ctx, t_ctx, b_conv[l], d_conv_w[l], d_conv_b[l], d_norm_g[l],
                                         d_norm_b[l])
            h_ctx = _merge(p_ctx, ya_c.reshape(b * t_ctx, BR_W), yb_c, yc_c.reshape(b * t_ctx, BR_W), yd_c,
                           h_ctx, gate[b:], b * t_ctx, w_br_l, w_out_l, ln_g[l], ln_b[l], alpha)
        h_lat = h_lat_new
    return h_lat.reshape(b, t, D_MODEL)
```

```python
import functools

import numpy as np
import jax
import jax.numpy as jnp
from jax import lax
from jax.experimental import pallas as pl
from jax.experimental.pallas import tpu as pltpu

F32 = jnp.float32
BF16 = jnp.bfloat16

D_MODEL = 1024
GRID_W = 64
N_BRANCH = 4
BR_W = D_MODEL // 2
HEAD_DIM = 128
A_HEADS = BR_W // HEAD_DIM
A_KV_HEADS = A_HEADS // 2
ROPE_THETA = 10000.0
AXIS_DIM = HEAD_DIM // 2
B_CONV = 3
C_HEADS = 4
C_HEAD_K = BR_W // (2 * C_HEADS)
C_HEAD_V = BR_W // C_HEADS
C_KEY_W = C_HEADS * C_HEAD_K
C_GATE_RANK = 16
C_GATE_TAU = 16.0
C_CHUNK = 64
D_CONV = 31
EPS = 1e-6
IN_WIDTHS = (
    A_HEADS * HEAD_DIM, A_KV_HEADS * HEAD_DIM, A_KV_HEADS * HEAD_DIM, BR_W,
    BR_W, BR_W, BR_W, BR_W,
    C_KEY_W, C_KEY_W, C_HEADS * C_HEAD_V, BR_W, 2 * C_GATE_RANK,
    2 * BR_W, BR_W,
    N_BRANCH * D_MODEL,
)

COL_AQ = 0
COL_AK = COL_AQ + 512
COL_AV = COL_AK + 256
COL_AZ = COL_AV + 256
COL_BG = COL_AZ + 512
COL_BC = COL_BG + 512
COL_BX = COL_BC + 512
COL_BZ = COL_BX + 512
COL_CQ = COL_BZ + 512
COL_CK = COL_CQ + 256
COL_CV = COL_CK + 256
COL_CZ = COL_CV + 512
COL_DA = COL_CZ + 512
COL_DG = COL_DA + 512
COL_DZ = COL_DG + 512
COL_MG = COL_DZ + 512
MG_BLK = 512
MERGE_SPLIT = 1
N_MAIN = COL_MG + N_BRANCH * D_MODEL
FULL_COLS = dict(ak=COL_AK, av=COL_AV, cq=COL_CQ, ck=COL_CK, cv=COL_CV, cz=COL_CZ)
CTX_KV_TILES = (COL_AK, COL_CQ, COL_CV)
CTX_KV_COLS = dict(ak=0, av=256, cq=512, ck=768, cv=1024, cz=1024)
LANES = 128
SUBLANES = 8
PROJ_TN = 1536
CTX_KV_TN = 512
LN_GROUP = 512
VMEM_LIMIT = 56 * 1024 * 1024
LOG2E = 1.4426950408889634


def _cparams(n_axes):
    return pltpu.CompilerParams(dimension_semantics=("arbitrary",) * n_axes,
                                vmem_limit_bytes=VMEM_LIMIT)


def _sigmoid(x):
    return 0.5 * jnp.tanh(0.5 * x) + 0.5


def _silu(x):
    h = 0.5 * x
    return h + h * jnp.tanh(h)


def _dot(a, b):
    return jnp.dot(a, b, preferred_element_type=F32)


def _dot_nt(a, b):
    return lax.dot_general(a, b, (((1,), (1,)), ((), ())), preferred_element_type=F32)


def _dot_tn(a, b):
    return lax.dot_general(a, b, (((0,), (0,)), ((), ())), preferred_element_type=F32)


def _mod_kernel(c_ref, w_ref, b_ref, o_ref):
    s = _silu(c_ref[...])
    o_ref[0] = _dot(s.astype(BF16), w_ref[0].astype(BF16)) + b_ref[0]


def _modulation(cc, w_mod, b_mod):
    depth = w_mod.shape[0]
    n_rows = cc.shape[0]
    return pl.pallas_call(
        _mod_kernel,
        grid=(depth, 1),
        in_specs=[
            pl.BlockSpec((n_rows, D_MODEL), lambda l, j: (0, 0)),
            pl.BlockSpec((1, D_MODEL, 3 * D_MODEL), lambda l, j: (l, 0, j)),
            pl.BlockSpec((1, 1, 3 * D_MODEL), lambda l, j: (l, 0, j)),
        ],
        out_specs=pl.BlockSpec((1, n_rows, 3 * D_MODEL), lambda l, j: (l, 0, j)),
        out_shape=jax.ShapeDtypeStruct((depth, n_rows, 3 * D_MODEL), F32),
        compiler_params=_cparams(2),
        name="modulation",
    )(cc, w_mod, b_mod.reshape(depth, 1, 3 * D_MODEL))


def _inproj_kernel(x_ref, shift_ref, scale_ref, w_ref, wr_ref, o_ref, or_ref, u_ref, *, tm, sub):
    j = pl.program_id(1)

    @pl.when(j == 0)
    def _():
        one_plus = 1.0 + scale_ref[0]
        shift = shift_ref[0]
        grp = min(LN_GROUP, tm)
        for g0 in range(0, tm, grp):
            for r0 in range(g0, g0 + grp, sub):
                x = x_ref[r0:r0 + sub, :]
                mu = jnp.mean(x, axis=-1, keepdims=True)
                xc = x - mu
                var = jnp.mean(xc * xc, axis=-1, keepdims=True)
                u = xc * lax.rsqrt(var + EPS) * one_plus + shift
                u_ref[r0:r0 + sub, :] = u.astype(BF16)
            u_grp = u_ref[g0:g0 + grp, :]
            or_ref[g0:g0 + grp, :] = _dot(u_grp, wr_ref[...]).astype(BF16)
            o_ref[g0:g0 + grp, :] = _dot(u_grp, w_ref[...]).astype(BF16)

    @pl.when(j != 0)
    def _():
        o_ref[...] = _dot(u_ref[...], w_ref[...]).astype(BF16)


def _in_projection(h, shift, scale, rows_per_mod, w_main, w_r, layer, tiles=None, tn=PROJ_TN):
    m = h.shape[0]
    tm = min(2048, rows_per_mod)
    assert m % tm == 0 and rows_per_mod % tm == 0
    tiles_per_mod = rows_per_mod // tm
    assert w_main.shape[2] % tn == 0
    n_tiles = len(tiles) if tiles else w_main.shape[2] // tn
    n_cols = n_tiles * tn

    def w_tile(j):
        if not tiles:
            return j
        idx = tiles[0]
        for n in range(1, len(tiles)):
            idx = idx + (tiles[n] - tiles[n - 1]) * jnp.minimum(jnp.maximum(j - n + 1, 0), 1)
        return idx

    kern = functools.partial(_inproj_kernel, tm=tm, sub=min(128, tm))
    return pl.pallas_call(
        kern,
        grid=(m // tm, n_tiles),
        in_specs=[
            pl.BlockSpec((tm, D_MODEL), lambda i, j: (i, 0)),
            pl.BlockSpec((1, 1, D_MODEL), lambda i, j: (i // tiles_per_mod, 0, 0)),
            pl.BlockSpec((1, 1, D_MODEL), lambda i, j: (i // tiles_per_mod, 0, 0)),
            pl.BlockSpec((None, D_MODEL, tn), lambda i, j: (layer, 0, w_tile(j))),
            pl.BlockSpec((None, D_MODEL, LANES), lambda i, j: (layer, 0, 0)),
        ],
        out_specs=[
            pl.BlockSpec((tm, tn), lambda i, j: (i, j)),
            pl.BlockSpec((tm, LANES), lambda i, j: (i, 0)),
        ],
        out_shape=[
            jax.ShapeDtypeStruct((m, n_cols), BF16),
            jax.ShapeDtypeStruct((m, LANES), BF16),
        ],
        scratch_shapes=[pltpu.VMEM((tm, D_MODEL), BF16)],
        compiler_params=_cparams(2),
        name="in_projection",
    )(h, shift, scale, w_main, w_r)


ATTN_KEY_BLOCKS = (768, 512, 256)
QK_AHEAD = 2


def _rms(x):
    return x * lax.rsqrt(jnp.mean(x * x, axis=-1, keepdims=True) + EPS)


def _rope(x, cos, sin):
    lane = lax.broadcasted_iota(jnp.int32, x.shape, 1)
    swapped = jnp.where(lane % 2 == 0, pltpu.roll(x, HEAD_DIM - 1, 1), pltpu.roll(x, 1, 1))
    return x * cos + swapped * sin


def _attn_kernel(*refs, n_lat, n_ctx, tq, kblk, kstep):
    if n_lat:
        (q_ref, z_ref, kl_ref, vl_ref, kc_ref, vc_ref, qg_ref, kg_ref, cos_ref, sin_ref,
         o_ref, kn_ref, vn_ref) = refs
    else:
        q_ref, z_ref, kc_ref, vc_ref, qg_ref, kg_ref, o_ref, kn_ref, vn_ref = refs
    qi = pl.program_id(2)
    kg = kg_ref[...]

    @pl.when(qi == 0)
    def _prep():
        if n_lat:
            def body(t, carry):
                r0 = pl.multiple_of(t * kblk, kblk)
                kn = _rms(kl_ref[0, pl.ds(r0, kblk), :].astype(F32)) * kg
                kn = _rope(kn, cos_ref[pl.ds(r0, kblk), :], sin_ref[pl.ds(r0, kblk), :])
                kn_ref[pl.ds(r0, kblk), :] = kn.astype(BF16)
                vn_ref[pl.ds(r0, kblk), 0:HEAD_DIM] = vl_ref[0, pl.ds(r0, kblk), :]
                return carry

            lax.fori_loop(0, n_lat // kblk, body, 0)
        kn_ref[n_lat:n_lat + n_ctx, :] = (_rms(kc_ref[0].astype(F32)) * kg).astype(BF16)
        vn_ref[n_lat:n_lat + n_ctx, 0:HEAD_DIM] = vc_ref[0]
        vn_ref[:, HEAD_DIM:2 * HEAD_DIM] = jnp.ones((n_lat + n_ctx, HEAD_DIM), BF16)

    qg = qg_ref[...]
    if n_lat:
        q0 = pl.multiple_of(qi * tq, tq)
        cos = cos_ref[pl.ds(q0, tq), :]
        sin = sin_ref[pl.ds(q0, tq), :]
    qns = []
    for g in range(2):
        qn = _rms(q_ref[0, :, g * HEAD_DIM:(g + 1) * HEAD_DIM].astype(F32)) * qg
        if n_lat:
            qn = _rope(qn, cos, sin)
        qns.append((qn * (HEAD_DIM ** -0.5 * LOG2E)).astype(BF16))

    items = [(g, k0) for k0 in range(0, n_lat + n_ctx, kstep) for g in range(2)]

    def scores(item):
        g, k0 = item
        return _dot_nt(qns[g], kn_ref[k0:k0 + kstep, :])

    m = [None, None]
    acc = [None, None]
    ahead = [scores(item) for item in items[:QK_AHEAD]]
    for n, (g, k0) in enumerate(items):
        s = ahead.pop(0)
        if n + QK_AHEAD < len(items):
            ahead.append(scores(items[n + QK_AHEAD]))
        m_blk = jnp.max(s, axis=-1, keepdims=True)
        m_new = m_blk if m[g] is None else jnp.maximum(m[g], m_blk)
        p = jnp.exp2((s - m_new).astype(BF16))
        pv = _dot(p, vn_ref[k0:k0 + kstep, :])
        acc[g] = pv if acc[g] is None else acc[g] * jnp.exp2(m[g] - m_new) + pv
        m[g] = m_new
    for g in range(2):
        lanes = slice(g * HEAD_DIM, (g + 1) * HEAD_DIM)
        o = acc[g][:, 0:HEAD_DIM] / acc[g][:, HEAD_DIM:2 * HEAD_DIM]
        o_ref[0, :, lanes] = (o * _silu(z_ref[0, :, lanes].astype(F32))).astype(BF16)


def _attention(p_q, p_lat, p_ctx, q_gain, k_gain, cos, sin, ctx_cols=None):
    b, t_q, _ = p_q.shape
    n_ctx = p_ctx.shape[1]
    n_lat = 0 if p_lat is None else p_lat.shape[1]
    tq = min(1024, t_q)
    assert t_q % tq == 0
    kblk = min(1024, n_lat) if n_lat else 0
    n_keys = n_lat + n_ctx
    kstep = next((c for c in ATTN_KEY_BLOCKS if n_keys % c == 0), n_keys)
    hw = 2 * HEAD_DIM
    q_spec = pl.BlockSpec((1, tq, hw), lambda i, kv, qi: (i, qi, COL_AQ // hw + kv))
    z_spec = pl.BlockSpec((1, tq, hw), lambda i, kv, qi: (i, qi, COL_AZ // hw + kv))

    def kv_spec(n, col):
        return pl.BlockSpec((1, n, HEAD_DIM), lambda i, kv, qi: (i, 0, col // HEAD_DIM + kv))

    gain_spec = pl.BlockSpec((1, HEAD_DIM), lambda i, kv, qi: (0, 0))
    args = [p_q, p_q]
    specs = [q_spec, z_spec]
    if n_lat:
        args += [p_lat, p_lat]
        specs += [kv_spec(n_lat, COL_AK), kv_spec(n_lat, COL_AV)]
    ctx_cols = ctx_cols or FULL_COLS
    args += [p_ctx, p_ctx, q_gain, k_gain]
    specs += [kv_spec(n_ctx, ctx_cols["ak"]), kv_spec(n_ctx, ctx_cols["av"]), gain_spec, gain_spec]
    if n_lat:
        tab_spec = pl.BlockSpec((n_lat, HEAD_DIM), lambda i, kv, qi: (0, 0))
        args += [cos, sin]
        specs += [tab_spec, tab_spec]
    kern = functools.partial(_attn_kernel, n_lat=n_lat, n_ctx=n_ctx, tq=tq, kblk=kblk, kstep=kstep)
    return pl.pallas_call(
        kern,
        grid=(b, A_KV_HEADS, t_q // tq),
        in_specs=specs,
        out_specs=pl.BlockSpec((1, tq, hw), lambda i, kv, qi: (i, qi, kv)),
        out_shape=jax.ShapeDtypeStruct((b, t_q, BR_W), BF16),
        scratch_shapes=[pltpu.VMEM((n_lat + n_ctx, HEAD_DIM), BF16),
                        pltpu.VMEM((n_lat + n_ctx, 2 * HEAD_DIM), BF16)],
        compiler_params=_cparams(3),
        name="attention_lat" if n_lat else "attention_ctx",
    )(*args)


D_HALO = 16
B_HALO = 16
CONV_RB = 16
CONV_ACCS = 2
ELEM_RB = 32
COPY_RB = 56


def _local_kernel(bg_ref, bc_ref, bcp_ref, bcn_ref, bx_ref, bxp_ref, bxn_ref, bz_ref,
                  da_ref, dap_ref, dan_ref, dg_ref, dgp_ref, dgn_ref, dz_ref,
                  bw_ref, dw_ref, db_ref, dgain_ref, dbeta_ref,
                  yb_ref, yd_ref, tbuf, gbuf, hbuf, *, tt, tiles_per_seq):
    i = pl.program_id(0)
    pos = i % tiles_per_seq
    keep_prev = (pos != 0).astype(F32)
    keep_next = (pos != tiles_per_seq - 1).astype(F32)

    def ld(ref, r0=0, n=None):
        return ref[r0:r0 + (n or ref.shape[0]), :].astype(F32)

    row_blocks = range(0, tt, ELEM_RB)

    tbuf[0:B_HALO, :] = ld(bcp_ref) * ld(bxp_ref) * keep_prev
    for r0 in row_blocks:
        tbuf[B_HALO + r0:B_HALO + r0 + ELEM_RB, :] = ld(bc_ref, r0, ELEM_RB) * ld(bx_ref, r0, ELEM_RB)
    tbuf[B_HALO + tt:2 * B_HALO + tt, :] = ld(bcn_ref) * ld(bxn_ref) * keep_next
    for r0 in row_blocks:
        lo = B_HALO + r0
        conv = (bw_ref[0:1, :] * tbuf[lo - 1:lo - 1 + ELEM_RB, :] + bw_ref[1:2, :] * tbuf[lo:lo + ELEM_RB, :]
                + bw_ref[2:3, :] * tbuf[lo + 1:lo + 1 + ELEM_RB, :])
        yb_ref[r0:r0 + ELEM_RB, :] = (ld(bg_ref, r0, ELEM_RB) * conv
                                      * _silu(ld(bz_ref, r0, ELEM_RB))).astype(BF16)

    gbuf[0, 0:D_HALO, :] = ld(dap_ref) * _sigmoid(ld(dgp_ref)) * keep_prev
    for r0 in row_blocks:
        gbuf[0, D_HALO + r0:D_HALO + r0 + ELEM_RB, :] = (ld(da_ref, r0, ELEM_RB)
                                                          * _sigmoid(ld(dg_ref, r0, ELEM_RB)))
    gbuf[0, D_HALO + tt:2 * D_HALO + tt, :] = ld(dan_ref) * _sigmoid(ld(dgn_ref)) * keep_next
    span = tt + 2 * D_HALO - SUBLANES
    for r in range(1, SUBLANES):
        for x0 in range(0, span, COPY_RB):
            n = min(COPY_RB, span - x0)
            gbuf[r, x0:x0 + n, :] = gbuf[0, r + x0:r + x0 + n, :]
    base = D_HALO - D_CONV // 2
    for cb in range(BR_W // LANES):
        lanes = slice(cb * LANES, (cb + 1) * LANES)
        for rb in range(tt // CONV_RB):
            accs = [None] * CONV_ACCS
            for k in range(D_CONV):
                off = base + k
                r0 = rb * CONV_RB + off - off % SUBLANES
                rows = gbuf[off % SUBLANES, r0:r0 + CONV_RB, lanes]
                term = rows.reshape(CONV_RB // SUBLANES, SUBLANES, LANES) * dw_ref[k, :, lanes][None]
                a = k % CONV_ACCS
                accs[a] = term if accs[a] is None else accs[a] + term
            while len(accs) > 1:
                accs = [accs[n] + accs[n + 1] for n in range(0, len(accs), 2)]
            hbuf[rb * CONV_RB:(rb + 1) * CONV_RB, lanes] = accs[0].reshape(CONV_RB, LANES)
    for r0 in row_blocks:
        hh = hbuf[r0:r0 + ELEM_RB, :] + db_ref[...]
        mu = jnp.mean(hh, axis=-1, keepdims=True)
        hc = hh - mu
        var = jnp.mean(hc * hc, axis=-1, keepdims=True)
        hn = hc * lax.rsqrt(var + EPS) * dgain_ref[...] + dbeta_ref[...]
        yd_ref[r0:r0 + ELEM_RB, :] = (_silu(hn) * _silu(ld(dz_ref, r0, ELEM_RB))).astype(BF16)


def _local_branches(p2, seq_len, b_w, d_w, d_b, d_g, d_beta):
    m = p2.shape[0]
    tt = min(512, seq_len)
    assert seq_len % tt == 0 and m % seq_len == 0
    tiles_per_seq = seq_len // tt
    n_tiles = m // tt

    def cur(col):
        return pl.BlockSpec((tt, BR_W), lambda i: (i, col // BR_W))

    def prev(col, halo):
        per = tt // halo
        return pl.BlockSpec((halo, BR_W), lambda i: (jnp.maximum(i * per - 1, 0), col // BR_W))

    def nxt(col, halo):
        per = tt // halo
        last = m // halo - 1
        return pl.BlockSpec((halo, BR_W), lambda i: (jnp.minimum((i + 1) * per, last), col // BR_W))

    def small(rows):
        return pl.BlockSpec((rows, BR_W), lambda i: (0, 0))

    specs = [cur(COL_BG),
             cur(COL_BC), prev(COL_BC, B_HALO), nxt(COL_BC, B_HALO),
             cur(COL_BX), prev(COL_BX, B_HALO), nxt(COL_BX, B_HALO),
             cur(COL_BZ),
             cur(COL_DA), prev(COL_DA, D_HALO), nxt(COL_DA, D_HALO),
             cur(COL_DG), prev(COL_DG, D_HALO), nxt(COL_DG, D_HALO),
             cur(COL_DZ),
             small(B_CONV), pl.BlockSpec((D_CONV, SUBLANES, BR_W), lambda i: (0, 0, 0)),
             small(1), small(1), small(1)]
    kern = functools.partial(_local_kernel, tt=tt, tiles_per_seq=tiles_per_seq)
    out_spec = pl.BlockSpec((tt, BR_W), lambda i: (i, 0))
    return pl.pallas_call(
        kern,
        grid=(n_tiles,),
        in_specs=specs,
        out_specs=[out_spec, out_spec],
        out_shape=[jax.ShapeDtypeStruct((m, BR_W), BF16), jax.ShapeDtypeStruct((m, BR_W), BF16)],
        scratch_shapes=[pltpu.VMEM((tt + 2 * B_HALO, BR_W), F32),
                        pltpu.VMEM((SUBLANES, tt + 2 * D_HALO, BR_W), F32),
                        pltpu.VMEM((tt, BR_W), F32)],
        compiler_params=_cparams(1),
        name="local_branches",
    )(*([p2] * 15), b_w, jnp.broadcast_to(d_w[:, None, :], (D_CONV, SUBLANES, BR_W)),
      d_b.reshape(1, BR_W), d_g.reshape(1, BR_W), d_beta.reshape(1, BR_W))


CH = C_CHUNK
PAIR_K = 2 * C_HEAD_K
PAIR_V = 2 * C_HEAD_V
LEVELS = (8, 4, 2)
BLK = SUBLANES
N_BLK = CH // BLK
GLA_UNROLL = 16
ROW_EQ = 0
ROW_EK = CH
ROW_LAST = 2 * CH
ROW_FK = 2 * CH + 8
ROW_LVL = ROW_FK + CH
ROW_FQ = ROW_LVL + len(LEVELS) * CH
N_FQ = BLK * (N_BLK * (N_BLK - 1) // 2)
N_EROWS = ROW_FQ + N_FQ


def _slabs(reverse):
    if reverse:
        return [(jb, 0, BLK * jb) for jb in range(1, N_BLK)]
    return [(jb, BLK * (jb + 1), CH - BLK * (jb + 1)) for jb in range(N_BLK - 1)]


def _gla_constants():
    idx = np.arange(CH)
    cols = np.arange(2 * CH) % CH
    emats, masks = [], []
    for reverse in (False, True):
        tri = (idx[None, :] >= idx[:, None]) if reverse else (idx[None, :] <= idx[:, None])
        tri = tri.astype(np.float32)
        edge = tri[0] if reverse else tri[CH - 1]
        blk_edge = (idx // BLK) * BLK + (0 if reverse else BLK - 1)
        blocks = [tri, edge[None, :] - tri, np.tile(edge[None, :], (8, 1)), tri[blk_edge] - tri]
        lvl_masks = []
        for grp in LEVELS:
            half = grp // 2
            in_q = (idx % grp < half) if reverse else (idx % grp >= half)
            ref = (idx // grp) * grp + (half if reverse else half - 1)
            d = tri - tri[ref]
            blocks.append(np.where(in_q[:, None], d, -d))
            in_q_col = (cols % grp < half) if reverse else (cols % grp >= half)
            lvl_masks.append((idx[:, None] // grp == cols[None, :] // grp) & in_q[:, None] & ~in_q_col[None, :])
        lvl_masks.append(idx[:, None] == cols[None, :])
        for jb, q0, nq in _slabs(reverse):
            edge_row = BLK * jb + (0 if reverse else BLK - 1)
            blocks.append(tri[q0:q0 + nq] - tri[edge_row][None, :])
        emat = np.concatenate(blocks, axis=0)
        assert emat.shape == (N_EROWS, CH) and emat.min() >= -1 and emat.max() <= 1
        emats.append(np.concatenate([emat] * 3, axis=1))
        masks.append(np.stack(lvl_masks).astype(np.float32))
    return (jnp.asarray(emats[0], BF16), jnp.asarray(emats[1], BF16), jnp.asarray(np.stack(masks), F32))


def _split3(g):
    g1 = g.astype(BF16)
    r1 = g - g1.astype(F32)
    g2 = r1.astype(BF16)
    g3 = (r1 - g2.astype(F32)).astype(BF16)
    return g1, g2, g3


def _log2_sigmoid(x):
    e = jnp.exp2(jnp.abs(x) * -LOG2E)
    return jnp.minimum(x, 0.0) * LOG2E - jnp.log2(1.0 + e)


def _gla_kernel(*refs, t_lat, t_ctx, want_ctx):
    (ql_ref, kl_ref, vl_ref, zl_ref, rl_ref, qc_ref, kc_ref, vc_ref, zc_ref, rc_ref,
     w2_ref, b2_ref, gain_ref, ematf_ref, ematb_ref, masks_ref) = refs[:16]
    if want_ctx:
        yl_ref, yc_ref = refs[16:18]
        scratch = refs[18:]
    else:
        yl_ref = refs[16]
        scratch = refs[17:]
        yc_ref = zc_ref = None
    stf_ref, stb_ref, of_ref, ob_ref, g1_ref, g2_ref, g3_ref = scratch

    def gates(r_ref, n_rows):
        blk = min(1024, n_rows)

        def body(t, carry):
            rows = pl.ds(pl.multiple_of(t * blk, blk), blk)
            x = _dot(r_ref[0, rows, :], w2_ref[...]) + b2_ref[...]
            g1, g2, g3 = _split3(_log2_sigmoid(x) * (1.0 / C_GATE_TAU))
            g1_ref[rows, :] = g1
            g2_ref[rows, :] = g2
            g3_ref[rows, :] = g3
            return carry

        lax.fori_loop(0, n_rows // blk, body, 0)

    def run(q_ref, k_ref, v_ref, n_chunks, want_out):
        unroll = min(GLA_UNROLL, n_chunks)
        assert n_chunks % unroll == 0
        lane_k = lax.broadcasted_iota(jnp.int32, (CH, PAIR_K), 1)
        head0 = lane_k < C_HEAD_K
        zero_v = jnp.zeros((CH, C_HEAD_V), BF16)
        key_blk = (lax.broadcasted_iota(jnp.int32, (BLK, 2 * CH), 1) % CH) // BLK

        def body(i, carry):
            work = []
            for reverse in (False, True):
                lanes = slice(PAIR_K, 2 * PAIR_K) if reverse else slice(0, PAIR_K)
                rows = []
                for u in range(unroll):
                    c = i * unroll + u
                    c = (n_chunks - 1 - c) if reverse else c
                    rows.append(pl.ds(pl.multiple_of(c * CH, CH), CH))
                rhs = jnp.concatenate(
                    [jnp.concatenate([g_ref[r, lanes] for r in rows], axis=1)
                     for g_ref in (g1_ref, g2_ref, g3_ref)], axis=0)
                emat = ematb_ref[...] if reverse else ematf_ref[...]
                e_all = jnp.exp2(_dot(emat, rhs))
                for u in range(unroll):
                    work.append(dict(reverse=reverse, rows=rows[u], u=u,
                                     e=e_all[:, u * PAIR_K:(u + 1) * PAIR_K]))
            work.sort(key=lambda w: w["u"])

            for w in work:
                st_ref = stb_ref if w["reverse"] else stf_ref
                e = w["e"]
                q = q_ref[0, w["rows"], :].astype(F32) * C_HEAD_K ** -0.5
                k = k_ref[0, w["rows"], :].astype(F32)
                v = v_ref[0, w["rows"], :]
                k0 = jnp.where(head0, k, 0.0)
                k1 = jnp.where(head0, 0.0, k)
                st = st_ref[...]
                e_k = e[ROW_EK:ROW_EK + CH]
                k_dec = jnp.concatenate([k0 * e_k, k1 * e_k], axis=0).astype(BF16)
                v_rows = jnp.concatenate([v[:, :C_HEAD_V], v[:, C_HEAD_V:]], axis=0)
                st_ref[...] = e[ROW_LAST:ROW_LAST + 1] * st + _dot_tn(v_rows, k_dec)
                if want_out:
                    qe = q * e[ROW_EQ:ROW_EQ + CH]
                    q_rows = jnp.concatenate([jnp.where(head0, qe, 0.0), jnp.where(head0, 0.0, qe)], axis=0)
                    o_st = _dot(q_rows.astype(BF16), st.T.astype(BF16))
                    w.update(q=q, k0=k0, k1=k1, v=v, o=jnp.concatenate([o_st[:CH], o_st[CH:]], axis=1))
            if not want_out:
                return carry

            for w in work:
                masks = masks_ref.at[1 if w["reverse"] else 0]
                q, k0, k1, e = w["q"], w["k0"], w["k1"], w["e"]
                fk = e[ROW_FK:ROW_FK + CH]
                kk_t = jnp.concatenate([k0 * fk, k1 * fk], axis=0).T.astype(BF16)
                slabs = _slabs(w["reverse"])
                lhs, off = [], ROW_FQ
                for jb, q0, nq in slabs:
                    lhs.append(q[q0:q0 + nq] * e[off:off + nq])
                    off += nq
                r = _dot(jnp.concatenate(lhs, axis=0).astype(BF16), kk_t)
                rows, off = [jnp.zeros((BLK, 2 * CH), F32)] * N_BLK, 0
                for jb, q0, nq in slabs:
                    for ib in range(q0 // BLK, (q0 + nq) // BLK):
                        piece = r[off + ib * BLK - q0:off + (ib + 1) * BLK - q0]
                        rows[ib] = jnp.where(key_blk == jb, piece, rows[ib])
                    off += nq
                scores = jnp.concatenate(rows, axis=0)
                k_heads = jnp.concatenate([k0, k1], axis=0)
                scores = scores + _dot(q.astype(BF16), k_heads.T.astype(BF16)) * masks[len(LEVELS)]
                for lvl in range(len(LEVELS)):
                    e_l = e[ROW_LVL + lvl * CH:ROW_LVL + (lvl + 1) * CH]
                    kk_t = jnp.concatenate([k0 * e_l, k1 * e_l], axis=0).T.astype(BF16)
                    scores = scores + _dot((q * e_l).astype(BF16), kk_t) * masks[lvl]
                w["scores"] = scores.astype(BF16)

            for w in work:
                v = w["v"]
                v_bd = jnp.concatenate([jnp.concatenate([v[:, :C_HEAD_V], zero_v], axis=1),
                                        jnp.concatenate([zero_v, v[:, C_HEAD_V:]], axis=1)], axis=0)
                (ob_ref if w["reverse"] else of_ref)[w["rows"], :] = w["o"] + _dot(w["scores"], v_bd)
            return carry

        lax.fori_loop(0, n_chunks // unroll, body, 0)

    def finish(z_ref, y_ref, n_rows):
        blk = min(512, n_rows)

        def body(t, carry):
            rows = pl.ds(pl.multiple_of(t * blk, blk), blk)
            o = of_ref[rows, :] + ob_ref[rows, :]
            z = z_ref[0, rows, :].astype(F32)
            gain = gain_ref[...]
            for h in range(2):
                lanes = slice(h * C_HEAD_V, (h + 1) * C_HEAD_V)
                y = _rms(o[:, lanes]) * gain[:, lanes]
                y_ref[0, rows, lanes] = (y * _silu(z[:, lanes])).astype(BF16)
            return carry

        lax.fori_loop(0, n_rows // blk, body, 0)

    stf_ref[...] = jnp.zeros_like(stf_ref)
    stb_ref[...] = jnp.zeros_like(stb_ref)
    gates(rc_ref, t_ctx)
    run(qc_ref, kc_ref, vc_ref, t_ctx // CH, want_ctx)
    if want_ctx:
        finish(zc_ref, yc_ref, t_ctx)
    gates(rl_ref, t_lat)
    run(ql_ref, kl_ref, vl_ref, t_lat // CH, True)
    finish(zl_ref, yl_ref, t_lat)


def _gla(p_lat, r_lat, p_ctx, r_ctx, w2, b2, gain, consts, want_ctx, ctx_cols=None):
    b, t_lat, _ = p_lat.shape
    t_ctx = p_ctx.shape[1]
    n_pairs = C_HEADS // 2

    def side(t, with_z, cols):
        z_rows = t if with_z else 16
        return [pl.BlockSpec((1, t, PAIR_K), lambda i, hp: (i, 0, cols["cq"] // PAIR_K + hp)),
                pl.BlockSpec((1, t, PAIR_K), lambda i, hp: (i, 0, cols["ck"] // PAIR_K + hp)),
                pl.BlockSpec((1, t, PAIR_V), lambda i, hp: (i, 0, cols["cv"] // PAIR_V + hp)),
                pl.BlockSpec((1, z_rows, PAIR_V), lambda i, hp: (i, 0, cols["cz"] // PAIR_V + hp)),
                pl.BlockSpec((1, t, LANES), lambda i, hp: (i, 0, 0))]

    def whole(shape):
        nd = len(shape)
        return pl.BlockSpec(shape, lambda i, hp: (0,) * nd)

    emat_f, emat_b, masks = consts
    specs = side(t_lat, True, FULL_COLS) + side(t_ctx, want_ctx, ctx_cols or FULL_COLS) + [
        pl.BlockSpec((1, LANES, 2 * PAIR_K), lambda i, hp: (hp, 0, 0)),
        pl.BlockSpec((1, 1, 2 * PAIR_K), lambda i, hp: (hp, 0, 0)),
        pl.BlockSpec((1, 1, PAIR_V), lambda i, hp: (hp, 0, 0)),
        whole(emat_f.shape), whole(emat_b.shape), whole(masks.shape)]
    out_specs = [pl.BlockSpec((1, t_lat, PAIR_V), lambda i, hp: (i, 0, hp))]
    out_shape = [jax.ShapeDtypeStruct((b, t_lat, BR_W), BF16)]
    if want_ctx:
        out_specs.append(pl.BlockSpec((1, t_ctx, PAIR_V), lambda i, hp: (i, 0, hp)))
        out_shape.append(jax.ShapeDtypeStruct((b, t_ctx, BR_W), BF16))

    def kern(*refs):
        refs = list(refs)
        for n in range(10, 13):
            refs[n] = refs[n].at[0]
        _gla_kernel(*refs, t_lat=t_lat, t_ctx=t_ctx, want_ctx=want_ctx)

    res = pl.pallas_call(
        kern,
        grid=(b, n_pairs),
        in_specs=specs,
        out_specs=out_specs,
        out_shape=out_shape,
        scratch_shapes=[pltpu.VMEM((C_HEAD_V, PAIR_K), F32), pltpu.VMEM((C_HEAD_V, PAIR_K), F32),
                        pltpu.VMEM((t_lat, PAIR_V), F32), pltpu.VMEM((t_lat, PAIR_V), F32),
                        pltpu.VMEM((t_lat, 2 * PAIR_K), BF16), pltpu.VMEM((t_lat, 2 * PAIR_K), BF16),
                        pltpu.VMEM((t_lat, 2 * PAIR_K), BF16)],
        compiler_params=_cparams(2),
        name="gla",
    )(p_lat, p_lat, p_lat, p_lat, r_lat, p_ctx, p_ctx, p_ctx, p_ctx, r_ctx,
      w2, b2, gain, emat_f, emat_b, masks)
    return (res[0], res[1]) if want_ctx else (res[0], None)


def _merge_kernel(*refs, alpha):
    n_mg = N_BRANCH * D_MODEL // MG_BLK
    mg_refs = refs[:n_mg]
    ya_ref, yb_ref, yc_ref, yd_ref, h_ref, gate_ref, wbr_ref, wout_ref, lng_ref, lnb_ref, o_ref = refs[n_mg:]
    per = D_MODEL // MG_BLK
    tm = o_ref.shape[0]
    rows_per = tm // MERGE_SPLIT
    for r0 in range(0, tm, rows_per):
        rows = slice(r0, r0 + rows_per)
        acc = None
        for n, y_ref in enumerate((ya_ref, yb_ref, yc_ref, yd_ref)):
            mg = jnp.concatenate([r[rows, :] for r in mg_refs[n * per:(n + 1) * per]], axis=1)
            gate = _sigmoid(mg.astype(F32))
            term = gate * _dot(y_ref[rows, :], wbr_ref[n])
            acc = term if acc is None else acc + term
        y = _dot(acc.astype(BF16), wout_ref[...])
        t = alpha * h_ref[rows, :] + gate_ref[0] * y
        mu = jnp.mean(t, axis=-1, keepdims=True)
        tc = t - mu
        var = jnp.mean(tc * tc, axis=-1, keepdims=True)
        o_ref[rows, :] = tc * lax.rsqrt(var + EPS) * lng_ref[...] + lnb_ref[...]


def _merge(p2, ya, yb, yc, yd, h, gate, rows_per_mod, w_br, w_out, ln_g, ln_b, alpha):
    m = h.shape[0]
    tm = min(1024, m)
    assert m % tm == 0 and rows_per_mod % tm == 0
    tiles_per_mod = rows_per_mod // tm
    n_mg = N_BRANCH * D_MODEL // MG_BLK
    br_spec = pl.BlockSpec((tm, BR_W), lambda i: (i, 0))
    vec_spec = pl.BlockSpec((1, D_MODEL), lambda i: (0, 0))
    return pl.pallas_call(
        functools.partial(_merge_kernel, alpha=alpha),
        grid=(m // tm,),
        in_specs=[pl.BlockSpec((tm, MG_BLK), functools.partial(lambda n, i: (i, COL_MG // MG_BLK + n), n))
                  for n in range(n_mg)] + [
            br_spec, br_spec, br_spec, br_spec,
            pl.BlockSpec((tm, D_MODEL), lambda i: (i, 0)),
            pl.BlockSpec((1, 1, D_MODEL), lambda i: (i // tiles_per_mod, 0, 0)),
            pl.BlockSpec((N_BRANCH, BR_W, D_MODEL), lambda i: (0, 0, 0), pipeline_mode=pl.Buffered(1)),
            pl.BlockSpec((D_MODEL, D_MODEL), lambda i: (0, 0), pipeline_mode=pl.Buffered(1)),
            vec_spec, vec_spec,
        ],
        out_specs=pl.BlockSpec((tm, D_MODEL), lambda i: (i, 0)),
        out_shape=jax.ShapeDtypeStruct((m, D_MODEL), F32),
        compiler_params=_cparams(1),
        name="merge",
    )(*([p2] * n_mg), ya, yb, yc, yd, h, gate, w_br, w_out, ln_g.reshape(1, D_MODEL), ln_b.reshape(1, D_MODEL))


def _projection_weights(w_in):
    r0 = int(sum(IN_WIDTHS[:12]))
    r1 = r0 + 2 * C_GATE_RANK
    w_main = jnp.concatenate([w_in[:, :, :r0], w_in[:, :, r1:]], axis=-1).astype(BF16)
    w_r = jnp.pad(w_in[:, :, r0:r1], ((0, 0), (0, 0), (0, LANES - 2 * C_GATE_RANK))).astype(BF16)
    return w_main, w_r


def _gate_weights(c_w2_l, c_b2_l):
    n_pairs = C_HEADS // 2

    def gate_w(i):
        w = jnp.pad(c_w2_l[i], ((i * C_GATE_RANK, LANES - (i + 1) * C_GATE_RANK), (0, 0)))
        return w.reshape(LANES, n_pairs, PAIR_K).transpose(1, 0, 2)

    w2 = jnp.concatenate([gate_w(0), gate_w(1)], axis=-1).astype(BF16)
    b2 = jnp.concatenate([c_b2_l[0].reshape(n_pairs, 1, PAIR_K), c_b2_l[1].reshape(n_pairs, 1, PAIR_K)], axis=-1)
    return w2, b2


def _rope_tables(t):
    rows = t // GRID_W
    row = np.repeat(np.arange(rows), GRID_W).astype(np.float32)
    col = np.tile(np.arange(GRID_W), rows).astype(np.float32)
    inv = (ROPE_THETA ** (-np.arange(0, AXIS_DIM, 2, dtype=np.float32) / AXIS_DIM)).astype(np.float32)
    ang = np.concatenate([row[:, None] * inv, col[:, None] * inv], -1).astype(np.float64)
    cos, sin = np.repeat(np.cos(ang), 2, axis=-1), np.repeat(np.sin(ang), 2, axis=-1)
    sin[:, 0::2] *= -1.0
    return jnp.asarray(cos, F32), jnp.asarray(sin, F32)


def kernel(x, c, ctx, c_ctx, w_mod, b_mod, w_in, q_norm, k_norm, b_conv, c_gate_w2, c_gate_b, c_norm,
           d_conv_w, d_conv_b, d_norm_g, d_norm_b, w_br, w_out, ln_g, ln_b):
    b, t, d = x.shape
    t_ctx = ctx.shape[1]
    depth = w_mod.shape[0]
    assert d == D_MODEL and b < 16
    alpha = (2 * depth) ** 0.25

    cc = jnp.concatenate([c, c_ctx[None], jnp.zeros((16 - b - 1, D_MODEL), F32)], axis=0)
    mod = _modulation(cc, w_mod, b_mod)
    cos, sin = _rope_tables(t)
    consts = _gla_constants()
    w_main, w_r = _projection_weights(w_in)
    w_br_bf, w_out_bf = w_br.astype(BF16), w_out.astype(BF16)
    n_pairs = C_HEADS // 2

    h_lat = x.reshape(b * t, D_MODEL)
    h_ctx = ctx.reshape(b * t_ctx, D_MODEL)
    for l in range(depth):
        want_ctx = l < depth - 1
        w2, b2 = _gate_weights(c_gate_w2[l], c_gate_b[l])
        q_gain, k_gain = q_norm[l].reshape(1, HEAD_DIM), k_norm[l].reshape(1, HEAD_DIM)
        c_gain = c_norm[l].reshape(n_pairs, 1, PAIR_V)
        shift, scale, gate = [mod[l, :, n * D_MODEL:(n + 1) * D_MODEL].reshape(16, 1, D_MODEL) for n in range(3)]
        p_lat, r_lat = _in_projection(h_lat, shift, scale, t, w_main, w_r, l)
        ctx_tiles = None if want_ctx else tuple(c // CTX_KV_TN for c in CTX_KV_TILES)
        ctx_cols = FULL_COLS if want_ctx else CTX_KV_COLS
        p_ctx, r_ctx = _in_projection(h_ctx, shift[b:], scale[b:], b * t_ctx, w_main, w_r, l, ctx_tiles,
                                      PROJ_TN if want_ctx else CTX_KV_TN)
        p_lat3 = p_lat.reshape(b, t, N_MAIN)
        p_ctx3 = p_ctx.reshape(b, t_ctx, p_ctx.shape[1])
        r_lat3 = r_lat.reshape(b, t, LANES)
        r_ctx3 = r_ctx.reshape(b, t_ctx, LANES)
        w_br_l, w_out_l = w_br_bf[l], w_out_bf[l]

        ya_l = _attention(p_lat3, p_lat3, p_ctx3, q_gain, k_gain, cos, sin, ctx_cols)
        yb_l, yd_l = _local_branches(p_lat, t, b_conv[l], d_conv_w[l], d_conv_b[l], d_norm_g[l], d_norm_b[l])
        yc_l, yc_c = _gla(p_lat3, r_lat3, p_ctx3, r_ctx3, w2, b2, c_gain, consts, want_ctx, ctx_cols)
        h_lat_new = _merge(p_lat, ya_l.reshape(b * t, BR_W), yb_l, yc_l.reshape(b * t, BR_W), yd_l,
                           h_lat, gate, t, w_br_l, w_out_l, ln_g[l], ln_b[l], alpha)
        if want_ctx:
            ya_c = _attention(p_ctx3, None, p_ctx3, q_gain, k_gain, None, None)
            yb_c, yd_c = _local_branches(p_ctx, t_ctx, b_conv[l], d_conv_w[l], d_conv_b[l], d_norm_g[l],
                                         d_norm_b[l])
            h_ctx = _merge(p_ctx, ya_c.reshape(b * t_ctx, BR_W), yb_c, yc_c.reshape(b * t_ctx, BR_W), yd_c,
                           h_ctx, gate[b:], b * t_ctx, w_br_l, w_out_l, ln_g[l], ln_b[l], alpha)
        h_lat = h_lat_new
    return h_lat.reshape(b, t, D_MODEL)
```
